```python
import jax, jax.numpy as jnp
from jax import lax
import numpy as np

D_MODEL = 1024
BATCH = 2
SEQ = 8192
DEPTH = 1

HEAD_DIM = 64
D_MIX = D_MODEL
SWA_WIDTH = D_MIX // 2
SWA_Q_HEADS = SWA_WIDTH // HEAD_DIM
SWA_KV_HEADS = 2
SWA_GROUP = SWA_Q_HEADS // SWA_KV_HEADS
DSA_WIDTH = D_MIX - SWA_WIDTH
DSA_Q_HEADS = DSA_WIDTH // HEAD_DIM
IDX_HEADS = 4
IDX_DIM = 64
WINDOW = 128
BLOCK = 128
TOPK_MAX = 256
ROPE_THETA = 10000.0
LN_EPS = 1e-5
ALPHA = (2.0 * DEPTH) ** 0.25
BETA = (8.0 * DEPTH) ** -0.25
COLUMN_SIZES = (
    SWA_Q_HEADS * HEAD_DIM,
    SWA_KV_HEADS * HEAD_DIM,
    SWA_KV_HEADS * HEAD_DIM,
    SWA_WIDTH,
    DSA_Q_HEADS * HEAD_DIM,
    HEAD_DIM,
    HEAD_DIM,
    DSA_WIDTH,
    IDX_HEADS * IDX_DIM,
    IDX_DIM,
    IDX_HEADS,
)
VALUE_COLUMNS = (2, 6)
N_COLS = sum(COLUMN_SIZES)

kernel_name = "hybrid_swa_sink_dsa_deepnorm_layer"


def rope(x, pos):
    d = x.shape[-1]
    inv = ROPE_THETA ** (-jnp.arange(0, d, 2, dtype=jnp.float32) / d)
    ang = pos.astype(jnp.float32)[:, :, None, None] * inv
    cos, sin = jnp.cos(ang), jnp.sin(ang)
    x1, x2 = jnp.split(x.astype(jnp.float32), 2, axis=-1)
    return jnp.concatenate([x1 * cos - x2 * sin, x2 * cos + x1 * sin], axis=-1).astype(x.dtype)


def layer_norm(x, g, b):
    xf = x.astype(jnp.float32)
    mu = jnp.mean(xf, axis=-1, keepdims=True)
    var = jnp.mean(jnp.square(xf - mu), axis=-1, keepdims=True)
    y = (xf - mu) * lax.rsqrt(var + LN_EPS) * g.astype(jnp.float32) + b.astype(jnp.float32)
    return y.astype(x.dtype)


def sliding_window_sink_attention(q, k, v, sinks):
    b, s, _, d = q.shape
    nb = s // BLOCK
    qb = q.reshape(b, nb, BLOCK, SWA_KV_HEADS, SWA_GROUP, d)

    def with_prev(t):
        t = t.reshape(b, nb, BLOCK, SWA_KV_HEADS, d)
        prev = jnp.pad(t[:, :-1], ((0, 0), (1, 0), (0, 0), (0, 0), (0, 0)))
        return jnp.concatenate([prev, t], axis=2)

    kk, vv = with_prev(k), with_prev(v)
    logits = jnp.einsum('bnqgrd,bnkgd->bngrqk', qb, kk).astype(jnp.float32) * (d ** -0.5)
    blk = jnp.arange(nb)[:, None, None]
    qpos = blk * BLOCK + jnp.arange(BLOCK)[None, :, None]
    kpos = (blk - 1) * BLOCK + jnp.arange(2 * BLOCK)[None, None, :]
    rel = qpos - kpos
    mask = (rel >= 0) & (rel < WINDOW) & (kpos >= 0)
    logits = jnp.where(mask[None, :, None, None], logits, -jnp.inf)
    sink = jnp.broadcast_to(
        sinks.astype(jnp.float32).reshape(SWA_KV_HEADS, SWA_GROUP)[None, None, :, :, None, None],
        logits.shape[:-1] + (1,))
    probs = jax.nn.softmax(jnp.concatenate([logits, sink], axis=-1), axis=-1)[..., :-1]
    out = jnp.einsum('bngrqk,bnkgd->bnqgrd', probs.astype(v.dtype), vv)
    return out.reshape(b, s, SWA_Q_HEADS * d)


def dsa_sparse_attention(q, k, v, q_idx, k_idx, w_idx):
    b, s, h, d = q.shape
    n_sel = min(TOPK_MAX, s // 4)
    nb = s // BLOCK
    key_pos = jnp.arange(s)

    def to_blocks(t):
        return jnp.moveaxis(t.reshape((b, nb, BLOCK) + t.shape[2:]), 1, 0)

    def one_block(args):
        qb, qib, wb, tpos = args
        rel = jax.nn.relu(jnp.einsum('bqhd,bsd->bqhs', qib, k_idx).astype(jnp.float32))
        score = jnp.einsum('bqhs,bqh->bqs', rel, wb.astype(jnp.float32))
        causal = key_pos[None, :] <= tpos[:, None]
        score = jnp.where(causal[None], score, -jnp.inf)
        _, idx = lax.top_k(score, n_sel)
        k_sel = jax.vmap(lambda kk, ii: kk[ii])(k, idx)
        v_sel = jax.vmap(lambda vv, ii: vv[ii])(v, idx)
        logits = jnp.einsum('bqhd,bqkd->bhqk', qb, k_sel).astype(jnp.float32) * (d ** -0.5)
        valid = idx <= tpos[None, :, None]
        logits = jnp.where(valid[:, None], logits, -jnp.inf)
        p = jax.nn.softmax(logits, axis=-1).astype(v.dtype)
        return jnp.einsum('bhqk,bqkd->bqhd', p, v_sel)

    qpos = jnp.arange(s).reshape(nb, BLOCK)
    out = lax.map(one_block, (to_blocks(q), to_blocks(q_idx), to_blocks(w_idx), qpos))
    return jnp.moveaxis(out, 0, 1).reshape(b, s, h * d)


def hybrid_layer(x, positions, w_in, b_in, sinks, w_out, b_out, ln_gain, ln_bias):
    b, s, _ = x.shape
    hcat = jnp.einsum('bsd,dc->bsc', x, w_in) + b_in
    splits = np.cumsum(COLUMN_SIZES)[:-1].tolist()
    aq, ak, av, ag, bq, bk, bv, bg, iq, ik, iw = jnp.split(hcat, splits, axis=-1)
    aq = rope(aq.reshape(b, s, SWA_Q_HEADS, HEAD_DIM), positions)
    ak = rope(ak.reshape(b, s, SWA_KV_HEADS, HEAD_DIM), positions)
    av = av.reshape(b, s, SWA_KV_HEADS, HEAD_DIM)
    a_out = sliding_window_sink_attention(aq, ak, av, sinks) * jax.nn.silu(ag)
    bq = rope(bq.reshape(b, s, DSA_Q_HEADS, HEAD_DIM), positions)
    bk = rope(bk.reshape(b, s, 1, HEAD_DIM), positions)[:, :, 0]
    iq = rope(iq.reshape(b, s, IDX_HEADS, IDX_DIM), positions)
    ik = rope(ik.reshape(b, s, 1, IDX_DIM), positions)[:, :, 0]
    iw = iw * (IDX_HEADS ** -0.5 * IDX_DIM ** -0.5)
    b_mix = dsa_sparse_attention(bq, bk, bv, iq, ik, iw) * jax.nn.silu(bg)
    y = jnp.einsum('bsc,cd->bsd', jnp.concatenate([a_out, b_mix], axis=-1), w_out) + b_out
    return layer_norm(ALPHA * x + y, ln_gain, ln_bias)


def setup_inputs(seed: int = 0) -> dict:
    key = jax.random.key(seed)
    ks = jax.random.split(key, 9)
    x = jax.random.normal(ks[0], (BATCH, SEQ, D_MODEL), jnp.float32)
    positions = jnp.broadcast_to(jnp.arange(SEQ, dtype=jnp.int32), (BATCH, SEQ))
    col_scale = jnp.concatenate([
        jnp.full((n,), BETA if i in VALUE_COLUMNS else 1.0, jnp.float32)
        for i, n in enumerate(COLUMN_SIZES)])
    w_in = jax.random.normal(ks[1], (DEPTH, D_MODEL, N_COLS), jnp.float32) * (D_MODEL ** -0.5) * col_scale
    b_in = 0.02 * jax.random.normal(ks[2], (DEPTH, N_COLS), jnp.float32)
    swa_sinks = jax.random.normal(ks[3], (DEPTH, SWA_Q_HEADS), jnp.float32)
    w_out = jax.random.normal(ks[4], (DEPTH, D_MIX, D_MODEL), jnp.float32) * (D_MIX ** -0.5) * BETA
    b_out = 0.02 * jax.random.normal(ks[5], (DEPTH, D_MODEL), jnp.float32)
    ln_gain = 1.0 + 0.02 * jax.random.normal(ks[6], (DEPTH, D_MODEL), jnp.float32)
    ln_bias = 0.02 * jax.random.normal(ks[7], (DEPTH, D_MODEL), jnp.float32)
    return {"x": x, "positions": positions, "w_in": w_in, "b_in": b_in,
            "swa_sinks": swa_sinks, "w_out": w_out, "b_out": b_out,
            "ln_gain": ln_gain, "ln_bias": ln_bias}


def reference(x, positions, w_in, b_in, swa_sinks, w_out, b_out, ln_gain, ln_bias):
    h = x
    for layer in range(DEPTH):
        h = hybrid_layer(h, positions, w_in[layer], b_in[layer], swa_sinks[layer],
                         w_out[layer], b_out[layer], ln_gain[layer], ln_bias[layer])
    return h
```

```python
import functools

import numpy as np
import jax
import jax.numpy as jnp
from jax import lax
from jax.experimental import pallas as pl
from jax.experimental.pallas import tpu as pltpu

HEAD_DIM = 64
SWA_Q_HEADS = 8
SWA_KV_HEADS = 2
DSA_Q_HEADS = 8
IDX_HEADS = 4
IDX_DIM = 64
WINDOW = 128
BLOCK = 128
TOPK_MAX = 256
ROPE_THETA = 10000.0
LN_EPS = 1e-5

LANES = 128
PROJ_ROWS = 512
KEY_CHUNK = 512
VMEM_LIMIT = 56 * 1024 * 1024

NEG_BIG = -1e30
INT_MIN = -2 ** 31
KEY_NEG_INF = 0x807FFFFF - 2 ** 32
KEY_LOWEST_FINITE = KEY_NEG_INF + 1

_F32 = jnp.float32
_BF16 = jnp.bfloat16
_I32 = jnp.int32
_NT = (((1,), (1,)), ((), ()))


def _column_plan(d_model):
    swa_w = SWA_Q_HEADS * HEAD_DIM
    kv_w = SWA_KV_HEADS * HEAD_DIM
    dsa_w = DSA_Q_HEADS * HEAD_DIM
    sizes = (swa_w, kv_w, kv_w, swa_w, dsa_w, HEAD_DIM, HEAD_DIM, dsa_w,
             IDX_HEADS * IDX_DIM, IDX_DIM, IDX_HEADS)
    o = np.cumsum((0,) + sizes)
    seg = [np.arange(o[i], o[i + 1]) for i in range(len(sizes))]
    aq, ak, av, ag, bq, bk, bv, bg, iq, ik, iw = seg
    dup = lambda c: np.concatenate([np.concatenate([h, h]) for h in c.reshape(-1, HEAD_DIM)])
    roped = np.concatenate([aq, dup(ak), bq, iq, dup(bk), dup(ik)])
    rot = roped.reshape(-1, 2, HEAD_DIM // 2)[:, ::-1, :].reshape(-1)
    plain = np.concatenate([dup(av), ag, bg])
    return roped, rot, plain, bv, iw


def _proj_kernel(x_ref, pos_ref, inv_ref, sgn_ref, wm_ref, wr_ref, bm_ref, br_ref,
                 wvt_ref, bvt_ref, wiw_ref, biw_ref,
                 aq_ref, ak2_ref, bq_ref, iq_ref, ki2_ref, av2_ref, g_ref, bvt_out_ref, iwt_ref):
    xb = x_ref[...].astype(_BF16)
    ang = pos_ref[...].astype(_F32) * inv_ref[...]
    cos = jnp.cos(ang)
    sin = jnp.sin(ang) * sgn_ref[...]

    roped_dst = ((aq_ref, 0, 0.125), (aq_ref, 256, 0.125), (ak2_ref, 0, 1.0), (bq_ref, 0, 0.125),
                 (bq_ref, 256, 0.125), (iq_ref, 0, 1.0), (ki2_ref, 0, 1.0))
    for g, (dst, off, scale) in enumerate(roped_dst):
        c0 = 256 * g
        hm = jnp.dot(xb, wm_ref[:, c0:c0 + 256], preferred_element_type=_F32) + bm_ref[:, c0:c0 + 256]
        hr = jnp.dot(xb, wr_ref[:, c0:c0 + 256], preferred_element_type=_F32) + br_ref[:, c0:c0 + 256]
        for s in range(2):
            sl = slice(LANES * s, LANES * (s + 1))
            o = hm[:, sl] * cos + hr[:, sl] * sin
            if scale != 1.0:
                o = o * scale
            dst[:, off + LANES * s:off + LANES * (s + 1)] = o.astype(dst.dtype)

    n_roped = 256 * len(roped_dst)
    hv = jnp.dot(xb, wm_ref[:, n_roped:n_roped + 256], preferred_element_type=_F32)
    av2_ref[...] = (hv + bm_ref[:, n_roped:n_roped + 256]).astype(av2_ref.dtype)
    for g in range(4):
        c0 = n_roped + 256 * (g + 1)
        h = jnp.dot(xb, wm_ref[:, c0:c0 + 256], preferred_element_type=_F32) + bm_ref[:, c0:c0 + 256]
        g_ref[:, 256 * g:256 * (g + 1)] = h * (1.0 / (1.0 + jnp.exp(-h)))

    vt = lax.dot_general(wvt_ref[...], xb, _NT, preferred_element_type=_F32) + bvt_ref[...]
    bvt_out_ref[0] = vt.astype(bvt_out_ref.dtype)
    iwt = lax.dot_general(wiw_ref[...], xb, _NT, preferred_element_type=_F32) + biw_ref[...]
    iwt_ref[...] = iwt * (IDX_HEADS ** -0.5 * IDX_DIM ** -0.5)


def _projection(x2, pos2, w_in, b_in):
    n, d = x2.shape
    tm = PROJ_ROWS
    roped, rot, plain, bv, iw = _column_plan(d)
    main = np.concatenate([roped, plain])
    wm = w_in[:, main].astype(_BF16)
    wr = w_in[:, rot].astype(_BF16)
    bm = b_in[main][None, :]
    br = b_in[rot][None, :]
    wv_t = w_in[:, bv].T
    zeros = jnp.zeros_like(wv_t)
    wvt = jnp.concatenate([wv_t, zeros, zeros, wv_t], axis=0).astype(_BF16)
    bv_b = b_in[bv]
    bvt = jnp.concatenate([bv_b, 0 * bv_b, 0 * bv_b, bv_b])[:, None]
    wiw = jnp.zeros((16, d), w_in.dtype).at[:IDX_HEADS].set(w_in[:, iw].T).astype(_BF16)
    biw = jnp.zeros((16, 1), b_in.dtype).at[:IDX_HEADS, 0].set(b_in[iw])
    half = HEAD_DIM // 2
    inv = ROPE_THETA ** (-jnp.arange(0, HEAD_DIM, 2, dtype=_F32) / HEAD_DIM)
    inv128 = jnp.tile(jnp.concatenate([inv, inv]), LANES // HEAD_DIM)[None, :]
    sgn128 = jnp.tile(jnp.concatenate([-jnp.ones((half,), _F32), jnp.ones((half,), _F32)]),
                      LANES // HEAD_DIM)[None, :]

    row = lambda w: pl.BlockSpec((tm, w), lambda i: (i, 0))
    full = lambda a: pl.BlockSpec(a.shape, lambda i: (0,) * a.ndim)
    out_shape = (
        jax.ShapeDtypeStruct((n, 512), _BF16),
        jax.ShapeDtypeStruct((n, 256), _BF16),
        jax.ShapeDtypeStruct((n, 512), _BF16),
        jax.ShapeDtypeStruct((n, 256), _BF16),
        jax.ShapeDtypeStruct((n, 256), _BF16),
        jax.ShapeDtypeStruct((n, 256), _BF16),
        jax.ShapeDtypeStruct((n, 1024), _F32),
        jax.ShapeDtypeStruct((n // tm, 256, tm), _BF16),
        jax.ShapeDtypeStruct((16, n), _F32),
    )
    out_specs = (row(512), row(256), row(512), row(256), row(256), row(256), row(1024),
                 pl.BlockSpec((1, 256, tm), lambda i: (i, 0, 0)),
                 pl.BlockSpec((16, tm), lambda i: (0, i)))
    consts = (inv128, sgn128, wm, wr, bm, br, wvt, bvt, wiw, biw)
    return pl.pallas_call(
        _proj_kernel,
        grid=(n // tm,),
        in_specs=[row(d), row(1)] + [full(a) for a in consts],
        out_specs=out_specs,
        out_shape=out_shape,
        compiler_params=pltpu.CompilerParams(dimension_semantics=("arbitrary",),
                                             vmem_limit_bytes=VMEM_LIMIT),
        name="in_projection",
    )(x2, pos2, *consts)


def _swa_kernel(q_ref, kc_ref, kp_ref, vc_ref, vp_ref, g_ref, sink_ref, out_ref):
    i = pl.program_id(1)
    lane = lax.broadcasted_iota(_I32, (BLOCK, LANES), 1)
    lo_half = lane < HEAD_DIM
    r = lax.broadcasted_iota(_I32, (BLOCK, 2 * BLOCK), 0)
    c = lax.broadcasted_iota(_I32, (BLOCK, 2 * BLOCK), 1)
    rel = r - c + BLOCK
    valid = (rel >= 0) & (rel < WINDOW) & ((c >= BLOCK) | (i > 0))
    group = SWA_Q_HEADS // SWA_KV_HEADS
    for j in range(SWA_Q_HEADS // 2):
        slab = q_ref[:, LANES * j:LANES * (j + 1)].astype(_F32)
        halves = []
        for s in range(2):
            h = 2 * j + s
            kv = h // group
            qm = jnp.where(lo_half if s == 0 else ~lo_half, slab, 0.0).astype(_BF16)
            kk = jnp.concatenate([kp_ref[:, LANES * kv:LANES * (kv + 1)],
                                  kc_ref[:, LANES * kv:LANES * (kv + 1)]], axis=0)
            vv = jnp.concatenate([vp_ref[:, LANES * kv:LANES * (kv + 1)],
                                  vc_ref[:, LANES * kv:LANES * (kv + 1)]], axis=0)
            logit = lax.dot_general(qm, kk, _NT, preferred_element_type=_F32)
            logit = jnp.where(valid, logit, -jnp.inf)
            sink = sink_ref[h:h + 1, 0:1]
            m = jnp.maximum(jnp.max(logit, axis=1, keepdims=True), sink)
            p = jnp.exp(logit - m)
            denom = jnp.sum(p, axis=1, keepdims=True) + jnp.exp(sink - m)
            o = jnp.dot(p.astype(_BF16), vv, preferred_element_type=_F32)
            halves.append(o / denom)
        o = jnp.where(lo_half, halves[0], halves[1])
        out_ref[:, LANES * j:LANES * (j + 1)] = (o * g_ref[:, LANES * j:LANES * (j + 1)]).astype(out_ref.dtype)


def _swa_attention(aq, ak2, av2, gates, sinks, batch, seq):
    nb = seq // BLOCK
    n = batch * seq
    sink_b = jnp.broadcast_to(sinks.astype(_F32)[:, None], (SWA_Q_HEADS, LANES))
    cur = lambda w: pl.BlockSpec((BLOCK, w), lambda b, i: (b * nb + i, 0))
    prev = lambda w: pl.BlockSpec((BLOCK, w), lambda b, i: (b * nb + jnp.maximum(i - 1, 0), 0))
    return pl.pallas_call(
        _swa_kernel,
        grid=(batch, nb),
        in_specs=[cur(512), cur(256), prev(256), cur(256), prev(256), cur(512),
                  pl.BlockSpec((SWA_Q_HEADS, LANES), lambda b, i: (0, 0))],
        out_specs=cur(512),
        out_shape=jax.ShapeDtypeStruct((n, 512), _BF16),
        compiler_params=pltpu.CompilerParams(dimension_semantics=("arbitrary", "arbitrary"),
                                             vmem_limit_bytes=VMEM_LIMIT),
        name="swa_attention",
    )(aq, ak2, ak2, av2, av2, gates, sink_b)


_DSA_HEAD_ORDER = (0, 2, 4, 6, 1, 3, 5, 7)


def _dsa_kernel(iq_ref, bq_ref, iwt_ref, g_ref, bk2_ref, ik2_ref, bvt_ref, out_ref,
                skey_ref, iqm_ref, bqm_ref, acc_e_ref, acc_o_ref, *, n_sel):
    tq, kc = BLOCK, KEY_CHUNK
    i = pl.program_id(1)
    q0 = i * tq
    nch = (q0 + tq + kc - 1) // kc
    lane = lax.broadcasted_iota(_I32, (tq, LANES), 1)
    lo_half = lane < HEAD_DIM

    def masked_half(ref, h):
        slab = ref[:, LANES * (h // 2):LANES * (h // 2 + 1)].astype(_F32)
        return jnp.where(lo_half if h % 2 == 0 else ~lo_half, slab, 0.0).astype(_BF16)

    for h in range(IDX_HEADS):
        iqm_ref[h * tq:(h + 1) * tq, :] = masked_half(iq_ref, h)
    for p, h in enumerate(_DSA_HEAD_ORDER):
        bqm_ref[p * tq:(p + 1) * tq, :] = masked_half(bq_ref, h)

    w = [iwt_ref[h:h + 1, :] for h in range(IDX_HEADS)]
    qidx = q0 + lax.broadcasted_iota(_I32, (kc, LANES), 1)
    krow = lax.broadcasted_iota(_I32, (kc, LANES), 0)

    def score_body(c, carry):
        k0 = pl.multiple_of(c * kc, kc)
        z = lax.dot_general(ik2_ref[pl.ds(k0, kc), :], iqm_ref[...], _NT, preferred_element_type=_F32)
        sc = w[0] * jnp.maximum(z[:, 0:LANES], 0.0)
        for h in range(1, IDX_HEADS):
            sc = sc + w[h] * jnp.maximum(z[:, LANES * h:LANES * (h + 1)], 0.0)
        sc = jnp.where(k0 + krow <= qidx, sc, -jnp.inf)
        bits = pltpu.bitcast(sc, _I32)
        skey_ref[pl.ds(k0, kc), :] = jnp.where(bits < 0, bits ^ 0x7FFFFFFF, bits)
        return carry

    lax.fori_loop(0, nch, score_body, 0)

    def count_ge(t):
        def body(c, acc):
            k0 = pl.multiple_of(c * kc, kc)
            ind = jnp.where(skey_ref[pl.ds(k0, kc), :] >= t, 1, 0)
            return acc + jnp.sum(ind.reshape(kc // 8, 8, LANES), axis=0)
        acc = lax.fori_loop(0, nch, body, jnp.zeros((8, LANES), _I32))
        return jnp.sum(acc, axis=0, keepdims=True)

    c_nonneg = count_ge(jnp.zeros((1, LANES), _I32))
    nonneg = c_nonneg >= n_sel
    cand0 = jnp.where(nonneg, 0, INT_MIN)
    cnt0 = jnp.where(nonneg, c_nonneg, nch * kc)

    def bit_body(j, carry):
        cand, cnt = carry
        t = cand | jnp.left_shift(jnp.int32(1), 30 - j)
        ct = count_ge(t)
        ok = ct >= n_sel
        return jnp.where(ok, t, cand), jnp.where(ok, ct, cnt)

    cand, cnt = lax.fori_loop(0, 31, bit_body, (cand0, cnt0))

    need = (cnt > n_sel) & (cand > KEY_NEG_INF)

    @pl.when(jnp.max(jnp.where(need, 1, 0)) > 0)
    def _():
        room = (n_sel - count_ge(cand + 1)).astype(_F32)
        before = (lax.broadcasted_iota(_I32, (kc, kc), 0) > lax.broadcasted_iota(_I32, (kc, kc), 1))
        before = jnp.where(before, 1.0, 0.0).astype(_BF16)

        def body(c, seen):
            k0 = pl.multiple_of(c * kc, kc)
            keys = skey_ref[pl.ds(k0, kc), :]
            tie = keys == cand
            tf = jnp.where(tie, 1.0, 0.0)
            rank = seen + jnp.dot(before, tf.astype(_BF16), preferred_element_type=_F32)
            skey_ref[pl.ds(k0, kc), :] = jnp.where(tie & (rank >= room), KEY_NEG_INF, keys)
            return seen + jnp.sum(tf, axis=0, keepdims=True)

        lax.fori_loop(0, nch, body, jnp.zeros((1, LANES), _F32))

    thr = jnp.maximum(cand, KEY_LOWEST_FINITE)

    nh = DSA_Q_HEADS
    acc_e_ref[...] = jnp.zeros_like(acc_e_ref)
    acc_o_ref[...] = jnp.zeros_like(acc_o_ref)

    def att_body(c, carry):
        m, l = carry
        k0 = pl.multiple_of(c * kc, kc)
        logits = lax.dot_general(bk2_ref[pl.ds(k0, kc), :], bqm_ref[...], _NT,
                                 preferred_element_type=_F32)
        sel = skey_ref[pl.ds(k0, kc), :] >= thr
        ps, ms, ls = [], [], []
        for p in range(nh):
            sl = slice(LANES * p, LANES * (p + 1))
            lg = jnp.where(sel, logits[:, sl], NEG_BIG)
            m_new = jnp.maximum(m[:, sl], jnp.max(lg, axis=0, keepdims=True))
            pr = jnp.exp(lg - m_new)
            alpha = jnp.exp(m[:, sl] - m_new)
            ls.append(alpha * l[:, sl] + jnp.sum(pr, axis=0, keepdims=True))
            ms.append(m_new)
            ps.append((pr.astype(_BF16), alpha))
        half = nh // 2
        vt = bvt_ref[c]
        for ref, lo, v in ((acc_e_ref, 0, vt[0:LANES, :]), (acc_o_ref, half, vt[LANES:2 * LANES, :])):
            pcat = jnp.concatenate([ps[lo + p][0] for p in range(half)], axis=1)
            acat = jnp.concatenate([ps[lo + p][1] for p in range(half)], axis=1)
            ref[...] = ref[...] * acat + jnp.dot(v, pcat, preferred_element_type=_F32)
        return jnp.concatenate(ms, axis=1), jnp.concatenate(ls, axis=1)

    m0 = jnp.full((1, nh * LANES), NEG_BIG, _F32)
    l0 = jnp.zeros((1, nh * LANES), _F32)
    _, l = lax.fori_loop(0, nch, att_body, (m0, l0))

    inv_l = 1.0 / l
    for j in range(nh // 2):
        sl = slice(LANES * j, LANES * (j + 1))
        so = slice(LANES * (nh // 2 + j), LANES * (nh // 2 + j + 1))
        tile = acc_e_ref[:, sl] * inv_l[:, sl] + acc_o_ref[:, sl] * inv_l[:, so]
        out_ref[:, sl] = (tile.T * g_ref[:, sl]).astype(out_ref.dtype)


def _dsa_attention(iq, bq, iwt, gates, ki2, bvt, batch, seq):
    nb = seq // BLOCK
    n = batch * seq
    nck = seq // KEY_CHUNK
    n_sel = min(TOPK_MAX, seq // 4)
    blk = lambda w, col=0: pl.BlockSpec((BLOCK, w), lambda b, i: (b * nb + i, col))
    return pl.pallas_call(
        functools.partial(_dsa_kernel, n_sel=n_sel),
        grid=(batch, nb),
        in_specs=[blk(256), blk(512),
                  pl.BlockSpec((16, BLOCK), lambda b, i: (0, b * nb + i)),
                  blk(512, 1),
                  pl.BlockSpec((seq, LANES), lambda b, i: (b, 0)),
                  pl.BlockSpec((seq, LANES), lambda b, i: (b, 1)),
                  pl.BlockSpec((nck, 256, KEY_CHUNK), lambda b, i: (b, 0, 0))],
        out_specs=blk(512),
        out_shape=jax.ShapeDtypeStruct((n, 512), _BF16),
        scratch_shapes=[pltpu.VMEM((seq, LANES), _I32),
                        pltpu.VMEM((IDX_HEADS * BLOCK, LANES), _BF16),
                        pltpu.VMEM((DSA_Q_HEADS * BLOCK, LANES), _BF16),
                        pltpu.VMEM((LANES, 512), _F32),
                        pltpu.VMEM((LANES, 512), _F32)],
        compiler_params=pltpu.CompilerParams(dimension_semantics=("arbitrary", "arbitrary"),
                                             vmem_limit_bytes=VMEM_LIMIT),
        name="dsa_attention",
    )(iq, bq, iwt, gates, ki2, ki2, bvt)


def _out_kernel(a_ref, b_ref, x_ref, w_ref, bo_ref, gain_ref, bias_ref, out_ref, *, alpha):
    half = a_ref.shape[1]
    y = jnp.dot(a_ref[...], w_ref[0:half, :], preferred_element_type=_F32)
    y = y + jnp.dot(b_ref[...], w_ref[half:2 * half, :], preferred_element_type=_F32)
    z = alpha * x_ref[...] + (y + bo_ref[...])
    mu = jnp.mean(z, axis=-1, keepdims=True)
    zc = z - mu
    var = jnp.mean(zc * zc, axis=-1, keepdims=True)
    out_ref[...] = zc * lax.rsqrt(var + LN_EPS) * gain_ref[...] + bias_ref[...]


def _out_projection(a, b, x2, w_out, b_out, gain, bias, alpha):
    n, d = x2.shape
    tm = PROJ_ROWS
    row = lambda w: pl.BlockSpec((tm, w), lambda i: (i, 0))
    full = lambda a_: pl.BlockSpec(a_.shape, lambda i: (0,) * a_.ndim)
    consts = (w_out.astype(_BF16), b_out[None, :], gain[None, :], bias[None, :])
    return pl.pallas_call(
        functools.partial(_out_kernel, alpha=alpha),
        grid=(n // tm,),
        in_specs=[row(a.shape[1]), row(b.shape[1]), row(d)] + [full(c) for c in consts],
        out_specs=row(d),
        out_shape=jax.ShapeDtypeStruct((n, d), x2.dtype),
        compiler_params=pltpu.CompilerParams(dimension_semantics=("arbitrary",),
                                             vmem_limit_bytes=VMEM_LIMIT),
        name="out_projection",
    )(a, b, x2, *consts)


def _layer(h, pos2, w_in, b_in, sinks, w_out, b_out, gain, bias, alpha):
    batch, seq, d = h.shape
    x2 = h.reshape(batch * seq, d)
    aq, ak2, bq, iq, ki2, av2, gates, bvt, iwt = _projection(x2, pos2, w_in, b_in)
    a = _swa_attention(aq, ak2, av2, gates, sinks, batch, seq)
    b = _dsa_attention(iq, bq, iwt, gates, ki2, bvt, batch, seq)
    return _out_projection(a, b, x2, w_out, b_out, gain, bias, alpha).reshape(batch, seq, d)


def kernel(x, positions, w_in, b_in, swa_sinks, w_out, b_out, ln_gain, ln_bias):
    batch, seq, d = x.shape
    depth = w_in.shape[0]
    assert d == 1024 and seq % KEY_CHUNK == 0 and (batch * seq) % PROJ_ROWS == 0
    assert KEY_CHUNK == PROJ_ROWS
    alpha = (2.0 * depth) ** 0.25
    pos2 = positions.reshape(batch * seq, 1)
    h = x
    for layer in range(depth):
        h = _layer(h, pos2, w_in[layer], b_in[layer], swa_sinks[layer], w_out[layer], b_out[layer],
                   ln_gain[layer], ln_bias[layer], alpha)
    return h
```

```python
import functools

import numpy as np
import jax
import jax.numpy as jnp
from jax import lax
from jax.experimental import pallas as pl
from jax.experimental.pallas import tpu as pltpu

HEAD_DIM = 64
SWA_Q_HEADS = 8
SWA_KV_HEADS = 2
DSA_Q_HEADS = 8
IDX_HEADS = 4
IDX_DIM = 64
WINDOW = 128
BLOCK = 128
TOPK_MAX = 256
ROPE_THETA = 10000.0
LN_EPS = 1e-5

LANES = 128
PROJ_ROWS = 512
KEY_CHUNK = 512
VMEM_LIMIT = 56 * 1024 * 1024

LOG2E = 1.4426950408889634
NEG_BIG = -1e30
SHIFT_LIMIT = 60.0
F32_LOWEST = -3.4028234663852886e38
INT_MAX = 2 ** 31 - 1
KEY_NEG_INF = 0x807FFFFF - 2 ** 32
KEY_MIN_NORMAL = 0x00800000
SEARCH_GROUP = 4

_F32 = jnp.float32
_BF16 = jnp.bfloat16
_I32 = jnp.int32
_NT = (((1,), (1,)), ((), ()))


def _split_columns(m):
    swa_w = SWA_Q_HEADS * HEAD_DIM
    kv_w = SWA_KV_HEADS * HEAD_DIM
    dsa_w = DSA_Q_HEADS * HEAD_DIM
    sizes = (swa_w, kv_w, kv_w, swa_w, dsa_w, HEAD_DIM, HEAD_DIM, dsa_w,
             IDX_HEADS * IDX_DIM, IDX_DIM, IDX_HEADS)
    o = np.cumsum((0,) + sizes)
    aq, ak, av, ag, bq, bk, bv, bg, iq, ik, iw = [m[:, o[i]:o[i + 1]] for i in range(len(sizes))]

    def dup(c):
        return jnp.concatenate([c[:, j:j + HEAD_DIM] for j in range(0, c.shape[1], HEAD_DIM) for _ in (0, 1)],
                               axis=1)

    def rot(c):
        r = c.reshape(c.shape[0], -1, 2, HEAD_DIM // 2)[:, :, ::-1, :]
        return r.reshape(c.shape[0], -1)

    roped = jnp.concatenate([aq, dup(ak), bq, iq, dup(bk), dup(ik)], axis=1)
    plain = jnp.concatenate([dup(av), ag, bg], axis=1)
    return jnp.concatenate([roped, plain], axis=1), rot(roped), bv, iw


def _proj_kernel(x_ref, pos_ref, inv_ref, sgn_ref, wm_ref, wr_ref, bm_ref, br_ref,
                 wvt_ref, bvt_ref, wiw_ref, biw_ref,
                 aq_ref, ak2_ref, bq_ref, iq_ref, ki2_ref, av2_ref, g_ref, bvt_out_ref, iwt_ref):
    xb = x_ref[...].astype(_BF16)
    ang = pos_ref[...].astype(_F32) * inv_ref[...]
    cos = jnp.cos(ang)
    sin = jnp.sin(ang) * sgn_ref[...]

    dsa_scale = 0.125 * LOG2E
    roped_dst = ((aq_ref, 0, 0.125), (aq_ref, 256, 0.125), (ak2_ref, 0, 1.0), (bq_ref, 0, dsa_scale),
                 (bq_ref, 256, dsa_scale), (iq_ref, 0, 1.0), (ki2_ref, 0, 1.0))
    for g, (dst, off, scale) in enumerate(roped_dst):
        c0 = 256 * g
        hm = jnp.dot(xb, wm_ref[:, c0:c0 + 256], preferred_element_type=_F32) + bm_ref[:, c0:c0 + 256]
        hr = jnp.dot(xb, wr_ref[:, c0:c0 + 256], preferred_element_type=_F32) + br_ref[:, c0:c0 + 256]
        for s in range(2):
            sl = slice(LANES * s, LANES * (s + 1))
            o = hm[:, sl] * cos + hr[:, sl] * sin
            if scale != 1.0:
                o = o * scale
            dst[:, off + LANES * s:off + LANES * (s + 1)] = o.astype(dst.dtype)

    n_roped = 256 * len(roped_dst)
    hv = jnp.dot(xb, wm_ref[:, n_roped:n_roped + 256], preferred_element_type=_F32)
    av2_ref[...] = (hv + bm_ref[:, n_roped:n_roped + 256]).astype(av2_ref.dtype)
    for g in range(4):
        c0 = n_roped + 256 * (g + 1)
        h = jnp.dot(xb, wm_ref[:, c0:c0 + 256], preferred_element_type=_F32) + bm_ref[:, c0:c0 + 256]
        g_ref[:, 256 * g:256 * (g + 1)] = h * (1.0 / (1.0 + jnp.exp(-h)))

    vt = lax.dot_general(wvt_ref[...], xb, _NT, preferred_element_type=_F32) + bvt_ref[...]
    bvt_out_ref[0] = vt.astype(bvt_out_ref.dtype)
    iwt = lax.dot_general(wiw_ref[...], xb, _NT, preferred_element_type=_F32) + biw_ref[...]
    iwt_ref[...] = iwt * (IDX_HEADS ** -0.5 * IDX_DIM ** -0.5)


def _projection(x2, pos2, w_in, b_in):
    n, d = x2.shape
    tm = PROJ_ROWS
    wm, wr, wv, wiw = _split_columns(w_in)
    bm, br, bv_b, biw = _split_columns(b_in[None, :])
    wm, wr = wm.astype(_BF16), wr.astype(_BF16)
    zpad = jnp.zeros((HEAD_DIM, d), w_in.dtype)
    wvt = jnp.concatenate([wv.T, zpad, zpad, wv.T], axis=0).astype(_BF16)
    one_row = jnp.zeros((HEAD_DIM,), b_in.dtype).at[0].set(1.0)
    bvt = jnp.concatenate([bv_b[0], one_row, one_row, bv_b[0]])[:, None]
    wiw_t = jnp.concatenate([wiw.T, jnp.zeros((16 - IDX_HEADS, d), w_in.dtype)], axis=0).astype(_BF16)
    biw_t = jnp.concatenate([biw[0], jnp.zeros((16 - IDX_HEADS,), b_in.dtype)])[:, None]
    half = HEAD_DIM // 2
    inv = ROPE_THETA ** (-jnp.arange(0, HEAD_DIM, 2, dtype=_F32) / HEAD_DIM)
    inv128 = jnp.tile(jnp.concatenate([inv, inv]), LANES // HEAD_DIM)[None, :]
    sgn128 = jnp.tile(jnp.concatenate([-jnp.ones((half,), _F32), jnp.ones((half,), _F32)]),
                      LANES // HEAD_DIM)[None, :]

    row = lambda w: pl.BlockSpec((tm, w), lambda i: (i, 0))
    full = lambda a: pl.BlockSpec(a.shape, lambda i: (0,) * a.ndim)
    out_shape = (
        jax.ShapeDtypeStruct((n, 512), _BF16),
        jax.ShapeDtypeStruct((n, 256), _BF16),
        jax.ShapeDtypeStruct((n, 512), _BF16),
        jax.ShapeDtypeStruct((n, 256), _BF16),
        jax.ShapeDtypeStruct((n, 256), _BF16),
        jax.ShapeDtypeStruct((n, 256), _BF16),
        jax.ShapeDtypeStruct((n, 1024), _F32),
        jax.ShapeDtypeStruct((n // tm, 256, tm), _BF16),
        jax.ShapeDtypeStruct((16, n), _F32),
    )
    out_specs = (row(512), row(256), row(512), row(256), row(256), row(256), row(1024),
                 pl.BlockSpec((1, 256, tm), lambda i: (i, 0, 0)),
                 pl.BlockSpec((16, tm), lambda i: (0, i)))
    consts = (inv128, sgn128, wm, wr, bm, br, wvt, bvt, wiw_t, biw_t)
    return pl.pallas_call(
        _proj_kernel,
        grid=(n // tm,),
        in_specs=[row(d), row(1)] + [full(a) for a in consts],
        out_specs=out_specs,
        out_shape=out_shape,
        compiler_params=pltpu.CompilerParams(dimension_semantics=("arbitrary",),
                                             vmem_limit_bytes=VMEM_LIMIT),
        name="in_projection",
    )(x2, pos2, *consts)


def _swa_kernel(q_ref, kc_ref, kp_ref, vc_ref, vp_ref, g_ref, sink_ref, out_ref):
    i = pl.program_id(1)
    lane = lax.broadcasted_iota(_I32, (BLOCK, LANES), 1)
    lo_half = lane < HEAD_DIM
    r = lax.broadcasted_iota(_I32, (BLOCK, 2 * BLOCK), 0)
    c = lax.broadcasted_iota(_I32, (BLOCK, 2 * BLOCK), 1)
    rel = r - c + BLOCK
    valid = (rel >= 0) & (rel < WINDOW) & ((c >= BLOCK) | (i > 0))
    group = SWA_Q_HEADS // SWA_KV_HEADS
    for j in range(SWA_Q_HEADS // 2):
        slab = q_ref[:, LANES * j:LANES * (j + 1)].astype(_F32)
        halves = []
        for s in range(2):
            h = 2 * j + s
            kv = h // group
            qm = jnp.where(lo_half if s == 0 else ~lo_half, slab, 0.0).astype(_BF16)
            kk = jnp.concatenate([kp_ref[:, LANES * kv:LANES * (kv + 1)],
                                  kc_ref[:, LANES * kv:LANES * (kv + 1)]], axis=0)
            vv = jnp.concatenate([vp_ref[:, LANES * kv:LANES * (kv + 1)],
                                  vc_ref[:, LANES * kv:LANES * (kv + 1)]], axis=0)
            logit = lax.dot_general(qm, kk, _NT, preferred_element_type=_F32)
            logit = jnp.where(valid, logit, -jnp.inf)
            sink = sink_ref[h:h + 1, 0:1]
            m = jnp.maximum(jnp.max(logit, axis=1, keepdims=True), sink)
            p = jnp.exp(logit - m)
            denom = jnp.sum(p, axis=1, keepdims=True) + jnp.exp(sink - m)
            o = jnp.dot(p.astype(_BF16), vv, preferred_element_type=_F32)
            halves.append(o / denom)
        o = jnp.where(lo_half, halves[0], halves[1])
        out_ref[:, LANES * j:LANES * (j + 1)] = (o * g_ref[:, LANES * j:LANES * (j + 1)]).astype(out_ref.dtype)


def _swa_attention(aq, ak2, av2, gates, sinks, batch, seq):
    nb = seq // BLOCK
    n = batch * seq
    sink_b = jnp.broadcast_to(sinks.astype(_F32)[:, None], (SWA_Q_HEADS, LANES))
    cur = lambda w: pl.BlockSpec((BLOCK, w), lambda b, i: (b * nb + i, 0))
    prev = lambda w: pl.BlockSpec((BLOCK, w), lambda b, i: (b * nb + jnp.maximum(i - 1, 0), 0))
    return pl.pallas_call(
        _swa_kernel,
        grid=(batch, nb),
        in_specs=[cur(512), cur(256), prev(256), cur(256), prev(256), cur(512),
                  pl.BlockSpec((SWA_Q_HEADS, LANES), lambda b, i: (0, 0))],
        out_specs=cur(512),
        out_shape=jax.ShapeDtypeStruct((n, 512), _BF16),
        compiler_params=pltpu.CompilerParams(dimension_semantics=("arbitrary", "arbitrary"),
                                             vmem_limit_bytes=VMEM_LIMIT),
        name="swa_attention",
    )(aq, ak2, ak2, av2, av2, gates, sink_b)


_DSA_HEAD_ORDER = (0, 2, 4, 6, 1, 3, 5, 7)


def _key_to_float(k):
    return lax.bitcast_convert_type(jnp.where(k < 0, k ^ 0x7FFFFFFF, k), _F32)


def _dsa_kernel(iq_ref, bq_ref, iwt_ref, g_ref, bk2_ref, ik2_ref, bvt_ref, out_ref,
                sc_ref, iqm_ref, bqm_ref, ka_ref, kb_ref, kmax_ref, acc_e_ref, acc_o_ref, *, n_sel):
    tq, kc = BLOCK, KEY_CHUNK
    nh, half = DSA_Q_HEADS, DSA_Q_HEADS // 2
    i = pl.program_id(1)
    q0 = i * tq
    nch = (q0 + tq + kc - 1) // kc
    lane = lax.broadcasted_iota(_I32, (tq, LANES), 1)
    lo_half = lane < HEAD_DIM

    @pl.when(i == 0)
    def _():
        klane = lax.broadcasted_iota(_I32, (kc, LANES), 1)
        klo = klane < HEAD_DIM

        def body(c, mx):
            k0 = pl.multiple_of(c * kc, kc)
            k = bk2_ref[pl.ds(k0, kc), :].astype(_F32)
            ka_ref[pl.ds(k0, kc), :] = jnp.where(klo, k, jnp.where(klane == HEAD_DIM, 1.0, 0.0)).astype(_BF16)
            kb_ref[pl.ds(k0, kc), :] = jnp.where(klo, jnp.where(klane == 0, 1.0, 0.0), k).astype(_BF16)
            return jnp.maximum(mx, jnp.sum(jnp.where(klo, k * k, 0.0), axis=1, keepdims=True))

        mx = lax.fori_loop(0, sc_ref.shape[0] // kc, body, jnp.zeros((kc, 1), _F32))
        kmax_ref[...] = jnp.broadcast_to(jnp.sqrt(jnp.max(mx, axis=0, keepdims=True)), kmax_ref.shape)

    def masked_half(ref, h):
        slab = ref[:, LANES * (h // 2):LANES * (h // 2 + 1)].astype(_F32)
        return jnp.where(lo_half if h % 2 == 0 else ~lo_half, slab, 0.0)

    for h in range(IDX_HEADS):
        iqm_ref[h * tq:(h + 1) * tq, :] = masked_half(iq_ref, h).astype(_BF16)
    kmax = kmax_ref[0:1, 0:1]
    shift_max = jnp.zeros((tq, 1), _F32)
    for p, h in enumerate(_DSA_HEAD_ORDER):
        qm = masked_half(bq_ref, h)
        shift = jnp.sqrt(jnp.sum(qm * qm, axis=1, keepdims=True)) * kmax
        shift_max = jnp.maximum(shift_max, shift)
        one_lane = HEAD_DIM if h % 2 == 0 else 0
        bqm_ref[p * tq:(p + 1) * tq, :] = jnp.where(lane == one_lane, -shift, qm).astype(_BF16)
    bounded = jnp.max(shift_max) <= SHIFT_LIMIT

    w = [iwt_ref[h:h + 1, :] for h in range(IDX_HEADS)]
    qidx = q0 + lax.broadcasted_iota(_I32, (kc, LANES), 1)
    krow = lax.broadcasted_iota(_I32, (kc, LANES), 0)

    def score_chunk(c, diagonal):
        k0 = pl.multiple_of(c * kc, kc)
        z = lax.dot_general(ik2_ref[pl.ds(k0, kc), :], iqm_ref[...], _NT, preferred_element_type=_F32)
        sc = w[0] * jnp.maximum(z[:, 0:LANES], 0.0)
        for h in range(1, IDX_HEADS):
            sc = sc + w[h] * jnp.maximum(z[:, LANES * h:LANES * (h + 1)], 0.0)
        if diagonal:
            sc = jnp.where(k0 + krow <= qidx, sc, -jnp.inf)
        sc_ref[pl.ds(k0, kc), :] = sc

    def score_body(c, carry):
        score_chunk(c, False)
        return carry

    lax.fori_loop(0, nch - 1, score_body, 0)
    score_chunk(nch - 1, True)

    def count_ge(t):
        def body(c, acc):
            k0 = pl.multiple_of(c * kc, kc)
            ind = jnp.where(sc_ref[pl.ds(k0, kc), :] >= t, 1, 0)
            return acc + jnp.sum(ind.reshape(kc // 8, 8, LANES), axis=0)
        acc = lax.fori_loop(0, nch, body, jnp.zeros((8, LANES), _I32))
        return jnp.sum(acc, axis=0, keepdims=True)

    def active(lo, hi, clo):
        settled = (clo == n_sel) | (hi - 1 <= lo) | ((lo == 0) & (hi <= KEY_MIN_NORMAL))
        return ~settled

    def search_pass(it, st):
        lo, hi, clo, chi = st
        mid = (lo >> 1) + (hi >> 1) + (lo & hi & 1)
        t = jnp.where(it == 0, 0, jnp.where((it == 1) & (lo == 0), KEY_MIN_NORMAL, mid))
        t = jnp.minimum(jnp.maximum(t, lo + 1), hi - 1)
        cnt = count_ge(_key_to_float(t))
        act = active(lo, hi, clo)
        up = act & (cnt >= n_sel)
        dn = act & (cnt < n_sel)
        return jnp.where(up, t, lo), jnp.where(dn, t, hi), jnp.where(up, cnt, clo), jnp.where(dn, cnt, chi)

    def search_group(c):
        it, st = c[0], c[1:5]
        for j in range(SEARCH_GROUP):
            st = search_pass(it + j, st)
        return (it + SEARCH_GROUP,) + st + (jnp.max(jnp.where(active(*st[:3]), 1, 0)),)

    vec = lambda v: jnp.full((1, LANES), v, _I32)
    init = (jnp.int32(0), vec(KEY_NEG_INF), vec(INT_MAX), vec(0) + nch * kc, vec(0), jnp.int32(1))
    _, lo, hi, clo, chi, _ = lax.while_loop(lambda c: (c[5] > 0) & (c[0] < 96), search_group, init)
    t_lo = _key_to_float(lo)
    t_hi = _key_to_float(hi)

    need = (clo > n_sel) & (lo > KEY_NEG_INF)

    @pl.when(jnp.max(jnp.where(need, 1, 0)) > 0)
    def _():
        room = (n_sel - chi).astype(_F32)
        before = (lax.broadcasted_iota(_I32, (kc, kc), 0) > lax.broadcasted_iota(_I32, (kc, kc), 1))
        before = jnp.where(before, 1.0, 0.0).astype(_BF16)

        def body(c, seen):
            k0 = pl.multiple_of(c * kc, kc)
            x = sc_ref[pl.ds(k0, kc), :]
            tie = (x >= t_lo) & ~(x >= t_hi)
            tf = jnp.where(tie, 1.0, 0.0)
            rank = seen + jnp.dot(before, tf.astype(_BF16), preferred_element_type=_F32)
            sc_ref[pl.ds(k0, kc), :] = jnp.where(tie & (rank >= room), -jnp.inf, x)
            return seen + jnp.sum(tf, axis=0, keepdims=True)

        lax.fori_loop(0, nch, body, jnp.zeros((1, LANES), _F32))

    thr = jnp.maximum(t_lo, F32_LOWEST)

    acc_e_ref[...] = jnp.zeros_like(acc_e_ref)
    acc_o_ref[...] = jnp.zeros_like(acc_o_ref)

    def masked_logits(c):
        k0 = pl.multiple_of(c * kc, kc)
        sel = sc_ref[pl.ds(k0, kc), :] >= thr
        lge = lax.dot_general(ka_ref[pl.ds(k0, kc), :], bqm_ref[0:half * tq, :], _NT,
                              preferred_element_type=_F32)
        lgo = lax.dot_general(kb_ref[pl.ds(k0, kc), :], bqm_ref[half * tq:nh * tq, :], _NT,
                              preferred_element_type=_F32)
        tiles = lambda lg: [jnp.where(sel, lg[:, LANES * j:LANES * (j + 1)], NEG_BIG) for j in range(half)]
        vt = bvt_ref[c]
        return ((acc_e_ref, tiles(lge), vt[0:LANES, :]), (acc_o_ref, tiles(lgo), vt[LANES:2 * LANES, :]))

    @pl.when(bounded)
    def _():
        def body(c, carry):
            for ref, tiles, v in masked_logits(c):
                p = jnp.concatenate([jnp.exp2(t).astype(_BF16) for t in tiles], axis=1)
                ref[...] += jnp.dot(v, p, preferred_element_type=_F32)
            return carry
        lax.fori_loop(0, nch, body, 0)

    @pl.when(jnp.logical_not(bounded))
    def _():
        def body(c, m):
            m_out = []
            for g, (ref, tiles, v) in enumerate(masked_logits(c)):
                ps, alphas = [], []
                for j, t in enumerate(tiles):
                    sl = slice(LANES * (g * half + j), LANES * (g * half + j + 1))
                    m_new = jnp.maximum(m[:, sl], jnp.max(t, axis=0, keepdims=True))
                    ps.append(jnp.exp2(t - m_new).astype(_BF16))
                    alphas.append(jnp.exp2(m[:, sl] - m_new))
                    m_out.append(m_new)
                ref[...] = ref[...] * jnp.concatenate(alphas, axis=1) + jnp.dot(
                    v, jnp.concatenate(ps, axis=1), preferred_element_type=_F32)
            return jnp.concatenate(m_out, axis=1)
        lax.fori_loop(0, nch, body, jnp.full((1, nh * LANES), NEG_BIG, _F32))

    row = lax.broadcasted_iota(_I32, (LANES, LANES), 0)
    for j in range(half):
        sl = slice(LANES * j, LANES * (j + 1))
        even = acc_e_ref[:, sl] / acc_e_ref[HEAD_DIM:HEAD_DIM + 1, sl]
        odd = acc_o_ref[:, sl] / acc_o_ref[0:1, sl]
        tile = jnp.where(row < HEAD_DIM, even, odd)
        out_ref[:, sl] = (tile.T * g_ref[:, sl]).astype(out_ref.dtype)


def _dsa_attention(iq, bq, iwt, gates, ki2, bvt, batch, seq):
    nb = seq // BLOCK
    n = batch * seq
    nck = seq // KEY_CHUNK
    n_sel = min(TOPK_MAX, seq // 4)
    blk = lambda w, col=0: pl.BlockSpec((BLOCK, w), lambda b, i: (b * nb + i, col))
    return pl.pallas_call(
        functools.partial(_dsa_kernel, n_sel=n_sel),
        grid=(batch, nb),
        in_specs=[blk(256), blk(512),
                  pl.BlockSpec((16, BLOCK), lambda b, i: (0, b * nb + i)),
                  blk(512, 1),
                  pl.BlockSpec((seq, LANES), lambda b, i: (b, 0)),
                  pl.BlockSpec((seq, LANES), lambda b, i: (b, 1)),
                  pl.BlockSpec((nck, 256, KEY_CHUNK), lambda b, i: (b, 0, 0))],
        out_specs=blk(512),
        out_shape=jax.ShapeDtypeStruct((n, 512), _BF16),
        scratch_shapes=[pltpu.VMEM((seq, LANES), _F32),
                        pltpu.VMEM((IDX_HEADS * BLOCK, LANES), _BF16),
                        pltpu.VMEM((DSA_Q_HEADS * BLOCK, LANES), _BF16),
                        pltpu.VMEM((seq, LANES), _BF16),
                        pltpu.VMEM((seq, LANES), _BF16),
                        pltpu.VMEM((8, LANES), _F32),
                        pltpu.VMEM((LANES, 512), _F32),
                        pltpu.VMEM((LANES, 512), _F32)],
        compiler_params=pltpu.CompilerParams(dimension_semantics=("arbitrary", "arbitrary"),
                                             vmem_limit_bytes=VMEM_LIMIT),
        name="dsa_attention",
    )(iq, bq, iwt, gates, ki2, ki2, bvt)


def _out_kernel(a_ref, b_ref, x_ref, w_ref, bo_ref, gain_ref, bias_ref, out_ref, *, alpha):
    half = a_ref.shape[1]
    y = jnp.dot(a_ref[...], w_ref[0:half, :], preferred_element_type=_F32)
    y = y + jnp.dot(b_ref[...], w_ref[half:2 * half, :], preferred_element_type=_F32)
    z = alpha * x_ref[...] + (y + bo_ref[...])
    mu = jnp.mean(z, axis=-1, keepdims=True)
    zc = z - mu
    var = jnp.mean(zc * zc, axis=-1, keepdims=True)
    out_ref[...] = zc * lax.rsqrt(var + LN_EPS) * gain_ref[...] + bias_ref[...]


def _out_projection(a, b, x2, w_out, b_out, gain, bias, alpha):
    n, d = x2.shape
    tm = PROJ_ROWS
    row = lambda w: pl.BlockSpec((tm, w), lambda i: (i, 0))
    full = lambda a_: pl.BlockSpec(a_.shape, lambda i: (0,) * a_.ndim)
    consts = (w_out.astype(_BF16), b_out[None, :], gain[None, :], bias[None, :])
    return pl.pallas_call(
        functools.partial(_out_kernel, alpha=alpha),
        grid=(n // tm,),
        in_specs=[row(a.shape[1]), row(b.shape[1]), row(d)] + [full(c) for c in consts],
        out_specs=row(d),
        out_shape=jax.ShapeDtypeStruct((n, d), x2.dtype),
        compiler_params=pltpu.CompilerParams(dimension_semantics=("arbitrary",),
                                             vmem_limit_bytes=VMEM_LIMIT),
        name="out_projection",
    )(a, b, x2, *consts)


def _layer(h, pos2, w_in, b_in, sinks, w_out, b_out, gain, bias, alpha):
    batch, seq, d = h.shape
    x2 = h.reshape(batch * seq, d)
    aq, ak2, bq, iq, ki2, av2, gates, bvt, iwt = _projection(x2, pos2, w_in, b_in)
    a = _swa_attention(aq, ak2, av2, gates, sinks, batch, seq)
    b = _dsa_attention(iq, bq, iwt, gates, ki2, bvt, batch, seq)
    return _out_projection(a, b, x2, w_out, b_out, gain, bias, alpha).reshape(batch, seq, d)


def kernel(x, positions, w_in, b_in, swa_sinks, w_out, b_out, ln_gain, ln_bias):
    batch, seq, d = x.shape
    depth = w_in.shape[0]
    assert d == 1024 and seq % KEY_CHUNK == 0 and (batch * seq) % PROJ_ROWS == 0
    assert KEY_CHUNK == PROJ_ROWS
    alpha = (2.0 * depth) ** 0.25
    pos2 = positions.reshape(batch * seq, 1)
    h = x
    for layer in range(depth):
        h = _layer(h, pos2, w_in[layer], b_in[layer], swa_sinks[layer], w_out[layer], b_out[layer],
                   ln_gain[layer], ln_bias[layer], alpha)
    return h
```

```python
import functools

import numpy as np
import jax
import jax.numpy as jnp
from jax import lax
from jax.experimental import pallas as pl
from jax.experimental.pallas import tpu as pltpu

HEAD_DIM = 64
SWA_Q_HEADS = 8
SWA_KV_HEADS = 2
DSA_Q_HEADS = 8
IDX_HEADS = 4
IDX_DIM = 64
WINDOW = 128
BLOCK = 128
TOPK_MAX = 256
ROPE_THETA = 10000.0
LN_EPS = 1e-5

LANES = 128
PROJ_ROWS = 512
KEY_CHUNK = 1024
TIE_CHUNK = 512
VMEM_LIMIT = 56 * 1024 * 1024

LOG2E = 1.4426950408889634
NEG_BIG = -1e30
SHIFT_LIMIT = 60.0
F32_LOWEST = -3.4028234663852886e38
INT_MAX = 2 ** 31 - 1
KEY_NEG_INF = 0x807FFFFF - 2 ** 32
KEY_MIN_NORMAL = 0x00800000
SEARCH_GROUP = 3

_F32 = jnp.float32
_BF16 = jnp.bfloat16
_I32 = jnp.int32
_NT = (((1,), (1,)), ((), ()))


def _split_columns(m):
    swa_w = SWA_Q_HEADS * HEAD_DIM
    kv_w = SWA_KV_HEADS * HEAD_DIM
    dsa_w = DSA_Q_HEADS * HEAD_DIM
    sizes = (swa_w, kv_w, kv_w, swa_w, dsa_w, HEAD_DIM, HEAD_DIM, dsa_w,
             IDX_HEADS * IDX_DIM, IDX_DIM, IDX_HEADS)
    o = np.cumsum((0,) + sizes)
    aq, ak, av, ag, bq, bk, bv, bg, iq, ik, iw = [m[:, o[i]:o[i + 1]] for i in range(len(sizes))]

    def dup(c):
        return jnp.concatenate([c[:, j:j + HEAD_DIM] for j in range(0, c.shape[1], HEAD_DIM) for _ in (0, 1)],
                               axis=1)

    def rot(c):
        r = c.reshape(c.shape[0], -1, 2, HEAD_DIM // 2)[:, :, ::-1, :]
        return r.reshape(c.shape[0], -1)

    roped = jnp.concatenate([aq, dup(ak), bq, iq, dup(bk), dup(ik)], axis=1)
    plain = jnp.concatenate([dup(av), ag, bg], axis=1)
    return jnp.concatenate([roped, plain], axis=1), rot(roped), bv, iw


def _proj_kernel(x_ref, pos_ref, inv_ref, sgn_ref, wm_ref, wr_ref, bm_ref, br_ref,
                 wvt_ref, bvt_ref, wiw_ref, biw_ref,
                 aq_ref, ak2_ref, bq_ref, iq_ref, ki2_ref, av2_ref, g_ref, bvt_out_ref, iwt_ref):
    xb = x_ref[...].astype(_BF16)
    ang = pos_ref[...].astype(_F32) * inv_ref[...]
    cos = jnp.cos(ang)
    sin = jnp.sin(ang) * sgn_ref[...]

    dsa_scale = 0.125 * LOG2E
    roped_dst = ((aq_ref, 0, 0.125), (aq_ref, 256, 0.125), (ak2_ref, 0, 1.0), (bq_ref, 0, dsa_scale),
                 (bq_ref, 256, dsa_scale), (iq_ref, 0, 1.0), (ki2_ref, 0, 1.0))
    for g, (dst, off, scale) in enumerate(roped_dst):
        c0 = 256 * g
        hm = jnp.dot(xb, wm_ref[:, c0:c0 + 256], preferred_element_type=_F32) + bm_ref[:, c0:c0 + 256]
        hr = jnp.dot(xb, wr_ref[:, c0:c0 + 256], preferred_element_type=_F32) + br_ref[:, c0:c0 + 256]
        for s in range(2):
            sl = slice(LANES * s, LANES * (s + 1))
            o = hm[:, sl] * cos + hr[:, sl] * sin
            if scale != 1.0:
                o = o * scale
            dst[:, off + LANES * s:off + LANES * (s + 1)] = o.astype(dst.dtype)

    n_roped = 256 * len(roped_dst)
    hv = jnp.dot(xb, wm_ref[:, n_roped:n_roped + 256], preferred_element_type=_F32)
    av2_ref[...] = (hv + bm_ref[:, n_roped:n_roped + 256]).astype(av2_ref.dtype)
    for g in range(4):
        c0 = n_roped + 256 * (g + 1)
        h = jnp.dot(xb, wm_ref[:, c0:c0 + 256], preferred_element_type=_F32) + bm_ref[:, c0:c0 + 256]
        g_ref[:, 256 * g:256 * (g + 1)] = h * (1.0 / (1.0 + jnp.exp(-h)))

    vt = lax.dot_general(wvt_ref[...], xb, _NT, preferred_element_type=_F32) + bvt_ref[...]
    bvt_out_ref[0] = vt.astype(bvt_out_ref.dtype)
    iwt = lax.dot_general(wiw_ref[...], xb, _NT, preferred_element_type=_F32) + biw_ref[...]
    iwt_ref[...] = iwt * (IDX_HEADS ** -0.5 * IDX_DIM ** -0.5)


def _projection(x2, pos2, w_in, b_in):
    n, d = x2.shape
    tm = PROJ_ROWS
    wm, wr, wv, wiw = _split_columns(w_in)
    bm, br, bv_b, biw = _split_columns(b_in[None, :])
    wm, wr = wm.astype(_BF16), wr.astype(_BF16)
    zpad = jnp.zeros((HEAD_DIM, d), w_in.dtype)
    wvt = jnp.concatenate([wv.T, zpad, zpad, wv.T], axis=0).astype(_BF16)
    one_row = jnp.zeros((HEAD_DIM,), b_in.dtype).at[0].set(1.0)
    bvt = jnp.concatenate([bv_b[0], one_row, one_row, bv_b[0]])[:, None]
    wiw_t = jnp.concatenate([wiw.T, jnp.zeros((16 - IDX_HEADS, d), w_in.dtype)], axis=0).astype(_BF16)
    biw_t = jnp.concatenate([biw[0], jnp.zeros((16 - IDX_HEADS,), b_in.dtype)])[:, None]
    half = HEAD_DIM // 2
    inv = ROPE_THETA ** (-jnp.arange(0, HEAD_DIM, 2, dtype=_F32) / HEAD_DIM)
    inv128 = jnp.tile(jnp.concatenate([inv, inv]), LANES // HEAD_DIM)[None, :]
    sgn128 = jnp.tile(jnp.concatenate([-jnp.ones((half,), _F32), jnp.ones((half,), _F32)]),
                      LANES // HEAD_DIM)[None, :]

    row = lambda w: pl.BlockSpec((tm, w), lambda i: (i, 0))
    full = lambda a: pl.BlockSpec(a.shape, lambda i: (0,) * a.ndim)
    out_shape = (
        jax.ShapeDtypeStruct((n, 512), _BF16),
        jax.ShapeDtypeStruct((n, 256), _BF16),
        jax.ShapeDtypeStruct((n, 512), _BF16),
        jax.ShapeDtypeStruct((n, 256), _BF16),
        jax.ShapeDtypeStruct((n, 256), _BF16),
        jax.ShapeDtypeStruct((n, 256), _BF16),
        jax.ShapeDtypeStruct((n, 1024), _F32),
        jax.ShapeDtypeStruct((n // tm, 256, tm), _BF16),
        jax.ShapeDtypeStruct((16, n), _F32),
    )
    out_specs = (row(512), row(256), row(512), row(256), row(256), row(256), row(1024),
                 pl.BlockSpec((1, 256, tm), lambda i: (i, 0, 0)),
                 pl.BlockSpec((16, tm), lambda i: (0, i)))
    consts = (inv128, sgn128, wm, wr, bm, br, wvt, bvt, wiw_t, biw_t)
    return pl.pallas_call(
        _proj_kernel,
        grid=(n // tm,),
        in_specs=[row(d), row(1)] + [full(a) for a in consts],
        out_specs=out_specs,
        out_shape=out_shape,
        compiler_params=pltpu.CompilerParams(dimension_semantics=("arbitrary",),
                                             vmem_limit_bytes=VMEM_LIMIT),
        name="in_projection",
    )(x2, pos2, *consts)


def _swa_kernel(q_ref, kc_ref, kp_ref, vc_ref, vp_ref, g_ref, sink_ref, out_ref):
    i = pl.program_id(1)
    lane = lax.broadcasted_iota(_I32, (BLOCK, LANES), 1)
    lo_half = lane < HEAD_DIM
    r = lax.broadcasted_iota(_I32, (BLOCK, 2 * BLOCK), 0)
    c = lax.broadcasted_iota(_I32, (BLOCK, 2 * BLOCK), 1)
    rel = r - c + BLOCK
    valid = (rel >= 0) & (rel < WINDOW) & ((c >= BLOCK) | (i > 0))
    group = SWA_Q_HEADS // SWA_KV_HEADS
    for j in range(SWA_Q_HEADS // 2):
        slab = q_ref[:, LANES * j:LANES * (j + 1)].astype(_F32)
        halves = []
        for s in range(2):
            h = 2 * j + s
            kv = h // group
            qm = jnp.where(lo_half if s == 0 else ~lo_half, slab, 0.0).astype(_BF16)
            kk = jnp.concatenate([kp_ref[:, LANES * kv:LANES * (kv + 1)],
                                  kc_ref[:, LANES * kv:LANES * (kv + 1)]], axis=0)
            vv = jnp.concatenate([vp_ref[:, LANES * kv:LANES * (kv + 1)],
                                  vc_ref[:, LANES * kv:LANES * (kv + 1)]], axis=0)
            logit = lax.dot_general(qm, kk, _NT, preferred_element_type=_F32)
            logit = jnp.where(valid, logit, -jnp.inf)
            sink = sink_ref[h:h + 1, 0:1]
            m = jnp.maximum(jnp.max(logit, axis=1, keepdims=True), sink)
            p = jnp.exp(logit - m)
            denom = jnp.sum(p, axis=1, keepdims=True) + jnp.exp(sink - m)
            o = jnp.dot(p.astype(_BF16), vv, preferred_element_type=_F32)
            halves.append(o / denom)
        o = jnp.where(lo_half, halves[0], halves[1])
        out_ref[:, LANES * j:LANES * (j + 1)] = (o * g_ref[:, LANES * j:LANES * (j + 1)]).astype(out_ref.dtype)


def _swa_attention(aq, ak2, av2, gates, sinks, batch, seq):
    nb = seq // BLOCK
    n = batch * seq
    sink_b = jnp.broadcast_to(sinks.astype(_F32)[:, None], (SWA_Q_HEADS, LANES))
    cur = lambda w: pl.BlockSpec((BLOCK, w), lambda b, i: (b * nb + i, 0))
    prev = lambda w: pl.BlockSpec((BLOCK, w), lambda b, i: (b * nb + jnp.maximum(i - 1, 0), 0))
    return pl.pallas_call(
        _swa_kernel,
        grid=(batch, nb),
        in_specs=[cur(512), cur(256), prev(256), cur(256), prev(256), cur(512),
                  pl.BlockSpec((SWA_Q_HEADS, LANES), lambda b, i: (0, 0))],
        out_specs=cur(512),
        out_shape=jax.ShapeDtypeStruct((n, 512), _BF16),
        compiler_params=pltpu.CompilerParams(dimension_semantics=("arbitrary", "arbitrary"),
                                             vmem_limit_bytes=VMEM_LIMIT),
        name="swa_attention",
    )(aq, ak2, ak2, av2, av2, gates, sink_b)


_DSA_HEAD_ORDER = (0, 2, 4, 6, 1, 3, 5, 7)


def _key_to_float(k):
    return lax.bitcast_convert_type(jnp.where(k < 0, k ^ 0x7FFFFFFF, k), _F32)


def _key16_to_float(k):
    return lax.bitcast_convert_type(jnp.where(k < 0, k ^ 0x7FFF, k) << 16, _F32)


def _midpoint(a, b):
    return (a >> 1) + (b >> 1) + (a & b & 1)


def _tree_sum(parts, chains=8):
    accs = list(parts[:chains])
    for j in range(chains, len(parts)):
        accs[j % chains] = accs[j % chains] + parts[j]
    while len(accs) > 1:
        accs = [accs[2 * j] + accs[2 * j + 1] for j in range(len(accs) // 2)]
    return accs[0]


def _dsa_kernel(iq_ref, bq_ref, iwt_ref, g_ref, bk2_ref, ik2_ref, bvt_ref, out_ref,
                sc_ref, sc16_ref, iqm_ref, bqm_ref, ka_ref, kb_ref, kmax_ref, acc_e_ref, acc_o_ref, *, n_sel):
    tq, kc = BLOCK, KEY_CHUNK
    nh, half = DSA_Q_HEADS, DSA_Q_HEADS // 2
    i = pl.program_id(1)
    q0 = i * tq
    nch = (q0 + tq + kc - 1) // kc
    lane = lax.broadcasted_iota(_I32, (tq, LANES), 1)
    lo_half = lane < HEAD_DIM

    @pl.when(i == 0)
    def _():
        klane = lax.broadcasted_iota(_I32, (kc, LANES), 1)
        klo = klane < HEAD_DIM

        def body(c, mx):
            k0 = pl.multiple_of(c * kc, kc)
            k = bk2_ref[pl.ds(k0, kc), :].astype(_F32)
            ka_ref[pl.ds(k0, kc), :] = jnp.where(klo, k, jnp.where(klane == HEAD_DIM, 1.0, 0.0)).astype(_BF16)
            kb_ref[pl.ds(k0, kc), :] = jnp.where(klo, jnp.where(klane == 0, 1.0, 0.0), k).astype(_BF16)
            return jnp.maximum(mx, jnp.sum(jnp.where(klo, k * k, 0.0), axis=1, keepdims=True))

        mx = lax.fori_loop(0, sc_ref.shape[0] // kc, body, jnp.zeros((kc, 1), _F32))
        kmax_ref[...] = jnp.broadcast_to(jnp.sqrt(jnp.max(mx, axis=0, keepdims=True)), kmax_ref.shape)

    def masked_half(ref, h):
        slab = ref[:, LANES * (h // 2):LANES * (h // 2 + 1)].astype(_F32)
        return jnp.where(lo_half if h % 2 == 0 else ~lo_half, slab, 0.0)

    for h in range(IDX_HEADS):
        iqm_ref[h * tq:(h + 1) * tq, :] = masked_half(iq_ref, h).astype(_BF16)
    kmax = kmax_ref[0:1, 0:1]
    shift_max = jnp.zeros((tq, 1), _F32)
    for p, h in enumerate(_DSA_HEAD_ORDER):
        qm = masked_half(bq_ref, h)
        shift = jnp.sqrt(jnp.sum(qm * qm, axis=1, keepdims=True)) * kmax
        shift_max = jnp.maximum(shift_max, shift)
        one_lane = HEAD_DIM if h % 2 == 0 else 0
        bqm_ref[p * tq:(p + 1) * tq, :] = jnp.where(lane == one_lane, -shift, qm).astype(_BF16)
    bounded = jnp.max(shift_max) <= SHIFT_LIMIT

    w = [iwt_ref[h:h + 1, :] for h in range(IDX_HEADS)]
    qidx = q0 + lax.broadcasted_iota(_I32, (kc, LANES), 1)
    krow = lax.broadcasted_iota(_I32, (kc, LANES), 0)

    def score_chunk(c, diagonal):
        k0 = pl.multiple_of(c * kc, kc)
        z = lax.dot_general(ik2_ref[pl.ds(k0, kc), :], iqm_ref[...], _NT, preferred_element_type=_F32)
        sc = w[0] * jnp.maximum(z[:, 0:LANES], 0.0)
        for h in range(1, IDX_HEADS):
            sc = sc + w[h] * jnp.maximum(z[:, LANES * h:LANES * (h + 1)], 0.0)
        if diagonal:
            sc = jnp.where(k0 + krow <= qidx, sc, -jnp.inf)
        sc_ref[pl.ds(k0, kc), :] = sc
        top = lax.bitcast_convert_type(sc, _I32) & jnp.int32(-65536)
        sc16_ref[pl.ds(k0, kc), :] = lax.bitcast_convert_type(top, _F32).astype(_BF16)

    def score_body(c, carry):
        score_chunk(c, False)
        return carry

    lax.fori_loop(0, nch - 1, score_body, 0)
    score_chunk(nch - 1, True)

    def count_ge(t):
        def body(c, acc):
            k0 = pl.multiple_of(c * kc, kc)
            for h in range(0, kc, 512):
                ind = jnp.where(sc_ref[pl.ds(k0 + h, 512), :] >= t, 1, 0)
                acc = acc + jnp.sum(ind.reshape(64, 8, LANES), axis=0)
            return acc
        acc = lax.fori_loop(0, nch, body, jnp.zeros((8, LANES), _I32))
        return jnp.sum(acc, axis=0, keepdims=True)

    one16, zero16 = jnp.ones((), _BF16), jnp.zeros((), _BF16)

    def count_ge16(t):
        t = t.astype(_BF16)

        def body(c, acc):
            k0 = pl.multiple_of(c * kc, kc)
            ind = jnp.where(sc16_ref[pl.ds(k0, kc), :] >= t, one16, zero16).reshape(kc // 16, 16, LANES)
            return acc + _tree_sum([ind[j] for j in range(kc // 16)]).astype(_F32)
        acc = lax.fori_loop(0, nch, body, jnp.zeros((16, LANES), _F32))
        return jnp.sum(acc, axis=0, keepdims=True).astype(_I32)

    def search_pass(count, to_thr, st):
        lo, hi, clo, chi, t, thr = st
        t_up, t_dn = _midpoint(t, hi), _midpoint(lo, t)
        thr_up, thr_dn = to_thr(t_up), to_thr(t_dn)
        cnt = count(thr)
        ok = cnt >= n_sel
        return (jnp.where(ok, t, lo), jnp.where(ok, hi, t), jnp.where(ok, cnt, clo), jnp.where(ok, chi, cnt),
                jnp.where(ok, t_up, t_dn), jnp.where(ok, thr_up, thr_dn))

    vec = lambda v: jnp.full((1, LANES), v, _I32)
    st = (vec(KEY_NEG_INF >> 16), vec(1 << 15), vec(0) + nch * kc, vec(0), vec(0), _key16_to_float(vec(0)))
    st = lax.fori_loop(0, 16, lambda j, s: search_pass(count_ge16, _key16_to_float, s), st)
    hint_lo = jnp.maximum(st[0] << 16, KEY_NEG_INF)
    hint_hi = jnp.where(st[1] >= (1 << 15), INT_MAX, st[1] << 16)
    c1 = count_ge(_key_to_float(hint_lo))
    ok1 = c1 >= n_sel
    lo, clo = jnp.where(ok1, hint_lo, KEY_NEG_INF), jnp.where(ok1, c1, nch * kc)
    hi, chi = jnp.where(ok1, INT_MAX, hint_lo), jnp.where(ok1, 0, c1)
    c2 = count_ge(_key_to_float(hint_hi))
    up2 = (c2 >= n_sel) & (hint_hi > lo)
    dn2 = (c2 < n_sel) & (hint_hi < hi)
    lo, clo = jnp.where(up2, hint_hi, lo), jnp.where(up2, c2, clo)
    hi, chi = jnp.where(dn2, hint_hi, hi), jnp.where(dn2, c2, chi)

    def unsettled(lo, hi, clo):
        return ~((clo == n_sel) | (hi - 1 <= lo) | ((lo == 0) & (hi <= KEY_MIN_NORMAL)))

    def fine_group(c):
        s = c[1:7]
        for _ in range(SEARCH_GROUP):
            s = search_pass(count_ge, _key_to_float, s)
        return (c[0] + 1,) + s + (jnp.max(jnp.where(unsettled(*s[:3]), 1, 0)),)

    t0 = _midpoint(lo, hi)
    init = (jnp.int32(0), lo, hi, clo, chi, t0, _key_to_float(t0),
            jnp.max(jnp.where(unsettled(lo, hi, clo), 1, 0)))
    _, lo, hi, clo, chi, _, _, _ = lax.while_loop(lambda c: (c[7] > 0) & (c[0] < 16), fine_group, init)
    t_lo = _key_to_float(lo)
    t_hi = _key_to_float(hi)

    need = (clo > n_sel) & (lo > KEY_NEG_INF)

    @pl.when(jnp.max(jnp.where(need, 1, 0)) > 0)
    def _():
        tk = TIE_CHUNK
        room = (n_sel - chi).astype(_F32)
        before = (lax.broadcasted_iota(_I32, (tk, tk), 0) > lax.broadcasted_iota(_I32, (tk, tk), 1))
        before = jnp.where(before, 1.0, 0.0).astype(_BF16)

        def body(c, seen):
            k0 = pl.multiple_of(c * tk, tk)
            x = sc_ref[pl.ds(k0, tk), :]
            tie = (x >= t_lo) & ~(x >= t_hi)
            tf = jnp.where(tie, 1.0, 0.0)
            rank = seen + jnp.dot(before, tf.astype(_BF16), preferred_element_type=_F32)
            sc_ref[pl.ds(k0, tk), :] = jnp.where(tie & (rank >= room), -jnp.inf, x)
            return seen + jnp.sum(tf, axis=0, keepdims=True)

        lax.fori_loop(0, nch * (kc // tk), body, jnp.zeros((1, LANES), _F32))

    thr = jnp.maximum(t_lo, F32_LOWEST)

    acc_e_ref[...] = jnp.zeros_like(acc_e_ref)
    acc_o_ref[...] = jnp.zeros_like(acc_o_ref)
    per_vt = kc // bvt_ref.shape[2]

    def masked_logits(c):
        k0 = pl.multiple_of(c * kc, kc)
        sel = sc_ref[pl.ds(k0, kc), :] >= thr
        lge = lax.dot_general(ka_ref[pl.ds(k0, kc), :], bqm_ref[0:half * tq, :], _NT,
                              preferred_element_type=_F32)
        lgo = lax.dot_general(kb_ref[pl.ds(k0, kc), :], bqm_ref[half * tq:nh * tq, :], _NT,
                              preferred_element_type=_F32)
        tiles = lambda lg: [jnp.where(sel, lg[:, LANES * j:LANES * (j + 1)], NEG_BIG) for j in range(half)]
        vt = jnp.concatenate([bvt_ref[c * per_vt + j] for j in range(per_vt)], axis=1)
        return ((acc_e_ref, tiles(lge), vt[0:LANES, :]), (acc_o_ref, tiles(lgo), vt[LANES:2 * LANES, :]))

    @pl.when(bounded)
    def _():
        def body(c, carry):
            for ref, tiles, v in masked_logits(c):
                p = jnp.concatenate([jnp.exp2(t).astype(_BF16) for t in tiles], axis=1)
                ref[...] += jnp.dot(v, p, preferred_element_type=_F32)
            return carry
        lax.fori_loop(0, nch, body, 0)

    @pl.when(jnp.logical_not(bounded))
    def _():
        def body(c, m):
            m_out = []
            for g, (ref, tiles, v) in enumerate(masked_logits(c)):
                ps, alphas = [], []
                for j, t in enumerate(tiles):
                    sl = slice(LANES * (g * half + j), LANES * (g * half + j + 1))
                    m_new = jnp.maximum(m[:, sl], jnp.max(t, axis=0, keepdims=True))
                    ps.append(jnp.exp2(t - m_new).astype(_BF16))
                    alphas.append(jnp.exp2(m[:, sl] - m_new))
                    m_out.append(m_new)
                ref[...] = ref[...] * jnp.concatenate(alphas, axis=1) + jnp.dot(
                    v, jnp.concatenate(ps, axis=1), preferred_element_type=_F32)
            return jnp.concatenate(m_out, axis=1)
        lax.fori_loop(0, nch, body, jnp.full((1, nh * LANES), NEG_BIG, _F32))

    row = lax.broadcasted_iota(_I32, (LANES, LANES), 0)
    for j in range(half):
        sl = slice(LANES * j, LANES * (j + 1))
        even = acc_e_ref[:, sl] / acc_e_ref[HEAD_DIM:HEAD_DIM + 1, sl]
        odd = acc_o_ref[:, sl] / acc_o_ref[0:1, sl]
        tile = jnp.where(row < HEAD_DIM, even, odd)
        out_ref[:, sl] = (tile.T * g_ref[:, sl]).astype(out_ref.dtype)


def _dsa_attention(iq, bq, iwt, gates, ki2, bvt, batch, seq):
    nb = seq // BLOCK
    n = batch * seq
    nvt = seq // PROJ_ROWS
    n_sel = min(TOPK_MAX, seq // 4)
    blk = lambda w, col=0: pl.BlockSpec((BLOCK, w), lambda b, i: (b * nb + i, col))
    return pl.pallas_call(
        functools.partial(_dsa_kernel, n_sel=n_sel),
        grid=(batch, nb),
        in_specs=[blk(256), blk(512),
                  pl.BlockSpec((16, BLOCK), lambda b, i: (0, b * nb + i)),
                  blk(512, 1),
                  pl.BlockSpec((seq, LANES), lambda b, i: (b, 0)),
                  pl.BlockSpec((seq, LANES), lambda b, i: (b, 1)),
                  pl.BlockSpec((nvt, 256, PROJ_ROWS), lambda b, i: (b, 0, 0))],
        out_specs=blk(512),
        out_shape=jax.ShapeDtypeStruct((n, 512), _BF16),
        scratch_shapes=[pltpu.VMEM((seq, LANES), _F32),
                        pltpu.VMEM((seq, LANES), _BF16),
                        pltpu.VMEM((IDX_HEADS * BLOCK, LANES), _BF16),
                        pltpu.VMEM((DSA_Q_HEADS * BLOCK, LANES), _BF16),
                        pltpu.VMEM((seq, LANES), _BF16),
                        pltpu.VMEM((seq, LANES), _BF16),
                        pltpu.VMEM((8, LANES), _F32),
                        pltpu.VMEM((LANES, 512), _F32),
                        pltpu.VMEM((LANES, 512), _F32)],
        compiler_params=pltpu.CompilerParams(dimension_semantics=("arbitrary", "arbitrary"),
                                             vmem_limit_bytes=VMEM_LIMIT),
        name="dsa_attention",
    )(iq, bq, iwt, gates, ki2, ki2, bvt)


def _out_kernel(a_ref, b_ref, x_ref, w_ref, bo_ref, gain_ref, bias_ref, out_ref, *, alpha):
    half = a_ref.shape[1]
    y = jnp.dot(a_ref[...], w_ref[0:half, :], preferred_element_type=_F32)
    y = y + jnp.dot(b_ref[...], w_ref[half:2 * half, :], preferred_element_type=_F32)
    z = alpha * x_ref[...] + (y + bo_ref[...])
    mu = jnp.mean(z, axis=-1, keepdims=True)
    zc = z - mu
    var = jnp.mean(zc * zc, axis=-1, keepdims=True)
    out_ref[...] = zc * lax.rsqrt(var + LN_EPS) * gain_ref[...] + bias_ref[...]


def _out_projection(a, b, x2, w_out, b_out, gain, bias, alpha):
    n, d = x2.shape
    tm = PROJ_ROWS
    row = lambda w: pl.BlockSpec((tm, w), lambda i: (i, 0))
    full = lambda a_: pl.BlockSpec(a_.shape, lambda i: (0,) * a_.ndim)
    consts = (w_out.astype(_BF16), b_out[None, :], gain[None, :], bias[None, :])
    return pl.pallas_call(
        functools.partial(_out_kernel, alpha=alpha),
        grid=(n // tm,),
        in_specs=[row(a.shape[1]), row(b.shape[1]), row(d)] + [full(c) for c in consts],
        out_specs=row(d),
        out_shape=jax.ShapeDtypeStruct((n, d), x2.dtype),
        compiler_params=pltpu.CompilerParams(dimension_semantics=("arbitrary",),
                                             vmem_limit_bytes=VMEM_LIMIT),
        name="out_projection",
    )(a, b, x2, *consts)


def _layer(h, pos2, w_in, b_in, sinks, w_out, b_out, gain, bias, alpha):
    batch, seq, d = h.shape
    x2 = h.reshape(batch * seq, d)
    aq, ak2, bq, iq, ki2, av2, gates, bvt, iwt = _projection(x2, pos2, w_in, b_in)
    a = _swa_attention(aq, ak2, av2, gates, sinks, batch, seq)
    b = _dsa_attention(iq, bq, iwt, gates, ki2, bvt, batch, seq)
    return _out_projection(a, b, x2, w_out, b_out, gain, bias, alpha).reshape(batch, seq, d)


def kernel(x, positions, w_in, b_in, swa_sinks, w_out, b_out, ln_gain, ln_bias):
    batch, seq, d = x.shape
    depth = w_in.shape[0]
    assert d == 1024 and seq % KEY_CHUNK == 0 and (batch * seq) % PROJ_ROWS == 0
    assert KEY_CHUNK % PROJ_ROWS == 0 and KEY_CHUNK % TIE_CHUNK == 0
    alpha = (2.0 * depth) ** 0.25
    pos2 = positions.reshape(batch * seq, 1)
    h = x
    for layer in range(depth):
        h = _layer(h, pos2, w_in[layer], b_in[layer], swa_sinks[layer], w_out[layer], b_out[layer],
                   ln_gain[layer], ln_bias[layer], alpha)
    return h
```

```python
import functools

import numpy as np
import jax
import jax.numpy as jnp
from jax import lax
from jax.experimental import pallas as pl
from jax.experimental.pallas import tpu as pltpu

HEAD_DIM = 64
SWA_Q_HEADS = 8
SWA_KV_HEADS = 2
DSA_Q_HEADS = 8
IDX_HEADS = 4
IDX_DIM = 64
WINDOW = 128
BLOCK = 128
TOPK_MAX = 256
ROPE_THETA = 10000.0
LN_EPS = 1e-5

LANES = 128
PROJ_ROWS = 512
KEY_CHUNK = 1024
TIE_CHUNK = 512
VMEM_LIMIT = 56 * 1024 * 1024

LOG2E = 1.4426950408889634
NEG_BIG = -1e30
SHIFT_LIMIT = 60.0
F32_LOWEST = -3.4028234663852886e38
INT_MAX = 2 ** 31 - 1
KEY_NEG_INF = 0x807FFFFF - 2 ** 32
KEY_MIN_NORMAL = 0x00800000
SEARCH_GROUP = 3

_F32 = jnp.float32
_BF16 = jnp.bfloat16
_I32 = jnp.int32
_NT = (((1,), (1,)), ((), ()))


def _split_columns(m):
    swa_w = SWA_Q_HEADS * HEAD_DIM
    kv_w = SWA_KV_HEADS * HEAD_DIM
    dsa_w = DSA_Q_HEADS * HEAD_DIM
    sizes = (swa_w, kv_w, kv_w, swa_w, dsa_w, HEAD_DIM, HEAD_DIM, dsa_w,
             IDX_HEADS * IDX_DIM, IDX_DIM, IDX_HEADS)
    o = np.cumsum((0,) + sizes)
    aq, ak, av, ag, bq, bk, bv, bg, iq, ik, iw = [m[:, o[i]:o[i + 1]] for i in range(len(sizes))]

    def dup(c):
        return jnp.concatenate([c[:, j:j + HEAD_DIM] for j in range(0, c.shape[1], HEAD_DIM) for _ in (0, 1)],
                               axis=1)

    def rot(c):
        r = c.reshape(c.shape[0], -1, 2, HEAD_DIM // 2)[:, :, ::-1, :]
        return r.reshape(c.shape[0], -1)

    roped = jnp.concatenate([aq, dup(ak), bq, iq, dup(bk), dup(ik)], axis=1)
    plain = jnp.concatenate([dup(av), ag, bg], axis=1)
    return jnp.concatenate([roped, plain], axis=1), rot(roped), bv, iw


def _proj_kernel(x_ref, pos_ref, inv_ref, sgn_ref, wm_ref, wr_ref, bm_ref, br_ref,
                 wvt_ref, bvt_ref, wiw_ref, biw_ref,
                 aq_ref, ak2_ref, bq_ref, iq_ref, ki2_ref, av2_ref, g_ref, bvt_out_ref, iwt_ref):
    xb = x_ref[...].astype(_BF16)
    ang = pos_ref[...].astype(_F32) * inv_ref[...]
    cos = jnp.cos(ang)
    sin = jnp.sin(ang) * sgn_ref[...]

    dsa_scale = 0.125 * LOG2E
    roped_dst = ((aq_ref, 0, 0.125), (aq_ref, 256, 0.125), (ak2_ref, 0, 1.0), (bq_ref, 0, dsa_scale),
                 (bq_ref, 256, dsa_scale), (iq_ref, 0, 1.0), (ki2_ref, 0, 1.0))
    for g, (dst, off, scale) in enumerate(roped_dst):
        c0 = 256 * g
        hm = jnp.dot(xb, wm_ref[:, c0:c0 + 256], preferred_element_type=_F32) + bm_ref[:, c0:c0 + 256]
        hr = jnp.dot(xb, wr_ref[:, c0:c0 + 256], preferred_element_type=_F32) + br_ref[:, c0:c0 + 256]
        for s in range(2):
            sl = slice(LANES * s, LANES * (s + 1))
            o = hm[:, sl] * cos + hr[:, sl] * sin
            if scale != 1.0:
                o = o * scale
            dst[:, off + LANES * s:off + LANES * (s + 1)] = o.astype(dst.dtype)

    n_roped = 256 * len(roped_dst)
    hv = jnp.dot(xb, wm_ref[:, n_roped:n_roped + 256], preferred_element_type=_F32)
    av2_ref[...] = (hv + bm_ref[:, n_roped:n_roped + 256]).astype(av2_ref.dtype)
    for g in range(4):
        c0 = n_roped + 256 * (g + 1)
        h = jnp.dot(xb, wm_ref[:, c0:c0 + 256], preferred_element_type=_F32) + bm_ref[:, c0:c0 + 256]
        g_ref[:, 256 * g:256 * (g + 1)] = h * (1.0 / (1.0 + jnp.exp(-h)))

    vt = lax.dot_general(wvt_ref[...], xb, _NT, preferred_element_type=_F32) + bvt_ref[...]
    bvt_out_ref[0] = vt.astype(bvt_out_ref.dtype)
    iwt = lax.dot_general(wiw_ref[...], xb, _NT, preferred_element_type=_F32) + biw_ref[...]
    iwt_ref[...] = iwt * (IDX_HEADS ** -0.5 * IDX_DIM ** -0.5)


def _projection(x2, pos2, w_in, b_in):
    n, d = x2.shape
    tm = PROJ_ROWS
    wm, wr, wv, wiw = _split_columns(w_in)
    bm, br, bv_b, biw = _split_columns(b_in[None, :])
    wm, wr = wm.astype(_BF16), wr.astype(_BF16)
    zpad = jnp.zeros((HEAD_DIM, d), w_in.dtype)
    wvt = jnp.concatenate([wv.T, zpad, zpad, wv.T], axis=0).astype(_BF16)
    one_row = jnp.zeros((HEAD_DIM,), b_in.dtype).at[0].set(1.0)
    bvt = jnp.concatenate([bv_b[0], one_row, one_row, bv_b[0]])[:, None]
    wiw_t = jnp.concatenate([wiw.T, jnp.zeros((16 - IDX_HEADS, d), w_in.dtype)], axis=0).astype(_BF16)
    biw_t = jnp.concatenate([biw[0], jnp.zeros((16 - IDX_HEADS,), b_in.dtype)])[:, None]
    half = HEAD_DIM // 2
    inv = ROPE_THETA ** (-jnp.arange(0, HEAD_DIM, 2, dtype=_F32) / HEAD_DIM)
    inv128 = jnp.tile(jnp.concatenate([inv, inv]), LANES // HEAD_DIM)[None, :]
    sgn128 = jnp.tile(jnp.concatenate([-jnp.ones((half,), _F32), jnp.ones((half,), _F32)]),
                      LANES // HEAD_DIM)[None, :]

    row = lambda w: pl.BlockSpec((tm, w), lambda i: (i, 0))
    full = lambda a: pl.BlockSpec(a.shape, lambda i: (0,) * a.ndim)
    out_shape = (
        jax.ShapeDtypeStruct((n, 512), _BF16),
        jax.ShapeDtypeStruct((n, 256), _BF16),
        jax.ShapeDtypeStruct((n, 512), _BF16),
        jax.ShapeDtypeStruct((n, 256), _BF16),
        jax.ShapeDtypeStruct((n, 256), _BF16),
        jax.ShapeDtypeStruct((n, 256), _BF16),
        jax.ShapeDtypeStruct((n, 1024), _F32),
        jax.ShapeDtypeStruct((n // tm, 256, tm), _BF16),
        jax.ShapeDtypeStruct((16, n), _F32),
    )
    out_specs = (row(512), row(256), row(512), row(256), row(256), row(256), row(1024),
                 pl.BlockSpec((1, 256, tm), lambda i: (i, 0, 0)),
                 pl.BlockSpec((16, tm), lambda i: (0, i)))
    consts = (inv128, sgn128, wm, wr, bm, br, wvt, bvt, wiw_t, biw_t)
    return pl.pallas_call(
        _proj_kernel,
        grid=(n // tm,),
        in_specs=[row(d), row(1)] + [full(a) for a in consts],
        out_specs=out_specs,
        out_shape=out_shape,
        compiler_params=pltpu.CompilerParams(dimension_semantics=("arbitrary",),
                                             vmem_limit_bytes=VMEM_LIMIT),
        name="in_projection",
    )(x2, pos2, *consts)


def _swa_kernel(q_ref, kc_ref, kp_ref, vc_ref, vp_ref, g_ref, sink_ref, out_ref):
    i = pl.program_id(1)
    lane = lax.broadcasted_iota(_I32, (BLOCK, LANES), 1)
    lo_half = lane < HEAD_DIM
    r = lax.broadcasted_iota(_I32, (BLOCK, 2 * BLOCK), 0)
    c = lax.broadcasted_iota(_I32, (BLOCK, 2 * BLOCK), 1)
    rel = r - c + BLOCK
    valid = (rel >= 0) & (rel < WINDOW) & ((c >= BLOCK) | (i > 0))
    group = SWA_Q_HEADS // SWA_KV_HEADS
    for j in range(SWA_Q_HEADS // 2):
        slab = q_ref[:, LANES * j:LANES * (j + 1)].astype(_F32)
        halves = []
        for s in range(2):
            h = 2 * j + s
            kv = h // group
            qm = jnp.where(lo_half if s == 0 else ~lo_half, slab, 0.0).astype(_BF16)
            kk = jnp.concatenate([kp_ref[:, LANES * kv:LANES * (kv + 1)],
                                  kc_ref[:, LANES * kv:LANES * (kv + 1)]], axis=0)
            vv = jnp.concatenate([vp_ref[:, LANES * kv:LANES * (kv + 1)],
                                  vc_ref[:, LANES * kv:LANES * (kv + 1)]], axis=0)
            logit = lax.dot_general(qm, kk, _NT, preferred_element_type=_F32)
            logit = jnp.where(valid, logit, -jnp.inf)
            sink = sink_ref[h:h + 1, 0:1]
            m = jnp.maximum(jnp.max(logit, axis=1, keepdims=True), sink)
            p = jnp.exp(logit - m)
            denom = jnp.sum(p, axis=1, keepdims=True) + jnp.exp(sink - m)
            o = jnp.dot(p.astype(_BF16), vv, preferred_element_type=_F32)
            halves.append(o / denom)
        o = jnp.where(lo_half, halves[0], halves[1])
        out_ref[:, LANES * j:LANES * (j + 1)] = (o * g_ref[:, LANES * j:LANES * (j + 1)]).astype(out_ref.dtype)


def _swa_attention(aq, ak2, av2, gates, sinks, batch, seq):
    nb = seq // BLOCK
    n = batch * seq
    sink_b = jnp.broadcast_to(sinks.astype(_F32)[:, None], (SWA_Q_HEADS, LANES))
    cur = lambda w: pl.BlockSpec((BLOCK, w), lambda b, i: (b * nb + i, 0))
    prev = lambda w: pl.BlockSpec((BLOCK, w), lambda b, i: (b * nb + jnp.maximum(i - 1, 0), 0))
    return pl.pallas_call(
        _swa_kernel,
        grid=(batch, nb),
        in_specs=[cur(512), cur(256), prev(256), cur(256), prev(256), cur(512),
                  pl.BlockSpec((SWA_Q_HEADS, LANES), lambda b, i: (0, 0))],
        out_specs=cur(512),
        out_shape=jax.ShapeDtypeStruct((n, 512), _BF16),
        compiler_params=pltpu.CompilerParams(dimension_semantics=("arbitrary", "arbitrary"),
                                             vmem_limit_bytes=VMEM_LIMIT),
        name="swa_attention",
    )(aq, ak2, ak2, av2, av2, gates, sink_b)


_DSA_HEAD_ORDER = (0, 2, 4, 6, 1, 3, 5, 7)


def _key_to_float(k):
    return lax.bitcast_convert_type(jnp.where(k < 0, k ^ 0x7FFFFFFF, k), _F32)


def _key16_to_float(k):
    return lax.bitcast_convert_type(jnp.where(k <= 0, k ^ 0x7FFF ^ jnp.where(k == 0, -1, 0), k) << 16, _F32)


def _midpoint(a, b):
    return (a >> 1) + (b >> 1) + (a & b & 1)


def _tree_sum(parts, chains=8):
    accs = list(parts[:chains])
    for j in range(chains, len(parts)):
        accs[j % chains] = accs[j % chains] + parts[j]
    while len(accs) > 1:
        accs = [accs[2 * j] + accs[2 * j + 1] for j in range(len(accs) // 2)]
    return accs[0]


def _dsa_kernel(iq_ref, bq_ref, iwt_ref, g_ref, bk2_ref, ik2_ref, bvt_ref, out_ref,
                sc_ref, sc16_ref, iqm_ref, bqm_ref, ka_ref, kb_ref, kmax_ref, acc_e_ref, acc_o_ref, *, n_sel):
    tq, kc = BLOCK, KEY_CHUNK
    nh, half = DSA_Q_HEADS, DSA_Q_HEADS // 2
    i = pl.program_id(1)
    q0 = i * tq
    nch = (q0 + tq + kc - 1) // kc
    lane = lax.broadcasted_iota(_I32, (tq, LANES), 1)
    lo_half = lane < HEAD_DIM

    @pl.when(i == 0)
    def _():
        klane = lax.broadcasted_iota(_I32, (kc, LANES), 1)
        klo = klane < HEAD_DIM

        def body(c, mx):
            k0 = pl.multiple_of(c * kc, kc)
            k = bk2_ref[pl.ds(k0, kc), :].astype(_F32)
            ka_ref[pl.ds(k0, kc), :] = jnp.where(klo, k, jnp.where(klane == HEAD_DIM, 1.0, 0.0)).astype(_BF16)
            kb_ref[pl.ds(k0, kc), :] = jnp.where(klo, jnp.where(klane == 0, 1.0, 0.0), k).astype(_BF16)
            return jnp.maximum(mx, jnp.sum(jnp.where(klo, k * k, 0.0), axis=1, keepdims=True))

        mx = lax.fori_loop(0, sc_ref.shape[0] // kc, body, jnp.zeros((kc, 1), _F32))
        kmax_ref[...] = jnp.broadcast_to(jnp.sqrt(jnp.max(mx, axis=0, keepdims=True)), kmax_ref.shape)

    def masked_half(ref, h):
        slab = ref[:, LANES * (h // 2):LANES * (h // 2 + 1)].astype(_F32)
        return jnp.where(lo_half if h % 2 == 0 else ~lo_half, slab, 0.0)

    for h in range(IDX_HEADS):
        iqm_ref[h * tq:(h + 1) * tq, :] = masked_half(iq_ref, h).astype(_BF16)
    kmax = kmax_ref[0:1, 0:1]
    shift_max = jnp.zeros((tq, 1), _F32)
    for p, h in enumerate(_DSA_HEAD_ORDER):
        qm = masked_half(bq_ref, h)
        shift = jnp.sqrt(jnp.sum(qm * qm, axis=1, keepdims=True)) * kmax
        shift_max = jnp.maximum(shift_max, shift)
        one_lane = HEAD_DIM if h % 2 == 0 else 0
        bqm_ref[p * tq:(p + 1) * tq, :] = jnp.where(lane == one_lane, -shift, qm).astype(_BF16)
    bounded = jnp.max(shift_max) <= SHIFT_LIMIT

    w = [iwt_ref[h:h + 1, :] for h in range(IDX_HEADS)]
    qidx = q0 + lax.broadcasted_iota(_I32, (kc, LANES), 1)
    krow = lax.broadcasted_iota(_I32, (kc, LANES), 0)

    def score_chunk(c, diagonal):
        k0 = pl.multiple_of(c * kc, kc)
        z = lax.dot_general(ik2_ref[pl.ds(k0, kc), :], iqm_ref[...], _NT, preferred_element_type=_F32)
        sc = w[0] * jnp.maximum(z[:, 0:LANES], 0.0)
        for h in range(1, IDX_HEADS):
            sc = sc + w[h] * jnp.maximum(z[:, LANES * h:LANES * (h + 1)], 0.0)
        if diagonal:
            sc = jnp.where(k0 + krow <= qidx, sc, -jnp.inf)
        sc_ref[pl.ds(k0, kc), :] = sc
        top = lax.bitcast_convert_type(sc, _I32) & jnp.int32(-65536)
        sc16_ref[pl.ds(k0, kc), :] = lax.bitcast_convert_type(top, _F32).astype(_BF16)

    def score_body(c, carry):
        score_chunk(c, False)
        return carry

    lax.fori_loop(0, nch - 1, score_body, 0)
    score_chunk(nch - 1, True)

    def count_ge(t):
        def body(c, acc):
            k0 = pl.multiple_of(c * kc, kc)
            for h in range(0, kc, 512):
                ind = jnp.where(sc_ref[pl.ds(k0 + h, 512), :] >= t, 1, 0)
                acc = acc + jnp.sum(ind.reshape(64, 8, LANES), axis=0)
            return acc
        acc = lax.fori_loop(0, nch, body, jnp.zeros((8, LANES), _I32))
        return jnp.sum(acc, axis=0, keepdims=True)

    one16, zero16 = jnp.ones((), _BF16), jnp.zeros((), _BF16)

    def count_ge16(t):
        t = t.astype(_BF16)

        def body(c, acc):
            k0 = pl.multiple_of(c * kc, kc)
            d = pltpu.bitcast(sc16_ref[pl.ds(k0, kc), :] - t, _I32)
            neg = lax.shift_right_logical(d, 15) & 0x00010001
            return acc + jnp.sum(neg.reshape(kc // 16, 8, LANES), axis=0)
        acc = lax.fori_loop(0, nch, body, jnp.zeros((8, LANES), _I32))
        below = jnp.sum((acc & 0xFFFF) + lax.shift_right_logical(acc, 16), axis=0, keepdims=True)
        return nch * kc - below

    def search_pass(count, to_thr, st):
        lo, hi, clo, chi, t, thr = st
        t_up, t_dn = _midpoint(t, hi), _midpoint(lo, t)
        thr_up, thr_dn = to_thr(t_up), to_thr(t_dn)
        cnt = count(thr)
        ok = cnt >= n_sel
        return (jnp.where(ok, t, lo), jnp.where(ok, hi, t), jnp.where(ok, cnt, clo), jnp.where(ok, chi, cnt),
                jnp.where(ok, t_up, t_dn), jnp.where(ok, thr_up, thr_dn))

    vec = lambda v: jnp.full((1, LANES), v, _I32)
    st = (vec(KEY_NEG_INF >> 16), vec(1 << 15), vec(0) + nch * kc, vec(0), vec(0), _key16_to_float(vec(0)))
    st = lax.fori_loop(0, 16, lambda j, s: search_pass(count_ge16, _key16_to_float, s), st)
    hint_lo = jnp.maximum(st[0] << 16, KEY_NEG_INF)
    hint_hi = jnp.where(st[1] >= (1 << 15), INT_MAX, st[1] << 16)
    hint_hi = jnp.where((hint_hi > 0) & (hint_hi < KEY_MIN_NORMAL), KEY_MIN_NORMAL, hint_hi)
    c1 = count_ge(_key_to_float(hint_lo))
    ok1 = c1 >= n_sel
    lo, clo = jnp.where(ok1, hint_lo, KEY_NEG_INF), jnp.where(ok1, c1, nch * kc)
    hi, chi = jnp.where(ok1, INT_MAX, hint_lo), jnp.where(ok1, 0, c1)
    c2 = count_ge(_key_to_float(hint_hi))
    up2 = (c2 >= n_sel) & (hint_hi > lo)
    dn2 = (c2 < n_sel) & (hint_hi < hi)
    lo, clo = jnp.where(up2, hint_hi, lo), jnp.where(up2, c2, clo)
    hi, chi = jnp.where(dn2, hint_hi, hi), jnp.where(dn2, c2, chi)

    def unsettled(lo, hi, clo):
        return ~((clo == n_sel) | (hi - 1 <= lo) | ((lo >= 0) & (hi <= KEY_MIN_NORMAL)))

    def fine_group(c):
        s = c[1:7]
        for _ in range(SEARCH_GROUP):
            s = search_pass(count_ge, _key_to_float, s)
        return (c[0] + 1,) + s + (jnp.max(jnp.where(unsettled(*s[:3]), 1, 0)),)

    t0 = _midpoint(lo, hi)
    init = (jnp.int32(0), lo, hi, clo, chi, t0, _key_to_float(t0),
            jnp.max(jnp.where(unsettled(lo, hi, clo), 1, 0)))
    _, lo, hi, clo, chi, _, _, _ = lax.while_loop(lambda c: (c[7] > 0) & (c[0] < 16), fine_group, init)
    t_lo = _key_to_float(lo)
    t_hi = _key_to_float(hi)

    need = (clo > n_sel) & (lo > KEY_NEG_INF)

    @pl.when(jnp.max(jnp.where(need, 1, 0)) > 0)
    def _():
        tk = TIE_CHUNK
        room = (n_sel - chi).astype(_F32)
        before = (lax.broadcasted_iota(_I32, (tk, tk), 0) > lax.broadcasted_iota(_I32, (tk, tk), 1))
        before = jnp.where(before, 1.0, 0.0).astype(_BF16)

        def body(c, seen):
            k0 = pl.multiple_of(c * tk, tk)
            x = sc_ref[pl.ds(k0, tk), :]
            tie = (x >= t_lo) & ~(x >= t_hi)
            tf = jnp.where(tie, 1.0, 0.0)
            rank = seen + jnp.dot(before, tf.astype(_BF16), preferred_element_type=_F32)
            sc_ref[pl.ds(k0, tk), :] = jnp.where(tie & (rank >= room), -jnp.inf, x)
            return seen + jnp.sum(tf, axis=0, keepdims=True)

        lax.fori_loop(0, nch * (kc // tk), body, jnp.zeros((1, LANES), _F32))

    thr = jnp.maximum(t_lo, F32_LOWEST)

    acc_e_ref[...] = jnp.zeros_like(acc_e_ref)
    acc_o_ref[...] = jnp.zeros_like(acc_o_ref)
    per_vt = kc // bvt_ref.shape[2]

    def masked_logits(c):
        k0 = pl.multiple_of(c * kc, kc)
        sel = sc_ref[pl.ds(k0, kc), :] >= thr
        lge = lax.dot_general(ka_ref[pl.ds(k0, kc), :], bqm_ref[0:half * tq, :], _NT,
                              preferred_element_type=_F32)
        lgo = lax.dot_general(kb_ref[pl.ds(k0, kc), :], bqm_ref[half * tq:nh * tq, :], _NT,
                              preferred_element_type=_F32)
        tiles = lambda lg: [jnp.where(sel, lg[:, LANES * j:LANES * (j + 1)], NEG_BIG) for j in range(half)]
        vt = jnp.concatenate([bvt_ref[c * per_vt + j] for j in range(per_vt)], axis=1)
        return ((acc_e_ref, tiles(lge), vt[0:LANES, :]), (acc_o_ref, tiles(lgo), vt[LANES:2 * LANES, :]))

    @pl.when(bounded)
    def _():
        def body(c, carry):
            for ref, tiles, v in masked_logits(c):
                p = jnp.concatenate([jnp.exp2(t).astype(_BF16) for t in tiles], axis=1)
                ref[...] += jnp.dot(v, p, preferred_element_type=_F32)
            return carry
        lax.fori_loop(0, nch, body, 0)

    @pl.when(jnp.logical_not(bounded))
    def _():
        def body(c, m):
            m_out = []
            for g, (ref, tiles, v) in enumerate(masked_logits(c)):
                ps, alphas = [], []
                for j, t in enumerate(tiles):
                    sl = slice(LANES * (g * half + j), LANES * (g * half + j + 1))
                    m_new = jnp.maximum(m[:, sl], jnp.max(t, axis=0, keepdims=True))
                    ps.append(jnp.exp2(t - m_new).astype(_BF16))
                    alphas.append(jnp.exp2(m[:, sl] - m_new))
                    m_out.append(m_new)
                ref[...] = ref[...] * jnp.concatenate(alphas, axis=1) + jnp.dot(
                    v, jnp.concatenate(ps, axis=1), preferred_element_type=_F32)
            return jnp.concatenate(m_out, axis=1)
        lax.fori_loop(0, nch, body, jnp.full((1, nh * LANES), NEG_BIG, _F32))

    row = lax.broadcasted_iota(_I32, (LANES, LANES), 0)
    for j in range(half):
        sl = slice(LANES * j, LANES * (j + 1))
        even = acc_e_ref[:, sl] / acc_e_ref[HEAD_DIM:HEAD_DIM + 1, sl]
        odd = acc_o_ref[:, sl] / acc_o_ref[0:1, sl]
        tile = jnp.where(row < HEAD_DIM, even, odd)
        out_ref[:, sl] = (tile.T * g_ref[:, sl]).astype(out_ref.dtype)


def _dsa_attention(iq, bq, iwt, gates, ki2, bvt, batch, seq):
    nb = seq // BLOCK
    n = batch * seq
    nvt = seq // PROJ_ROWS
    n_sel = min(TOPK_MAX, seq // 4)
    blk = lambda w, col=0: pl.BlockSpec((BLOCK, w), lambda b, i: (b * nb + i, col))
    return pl.pallas_call(
        functools.partial(_dsa_kernel, n_sel=n_sel),
        grid=(batch, nb),
        in_specs=[blk(256), blk(512),
                  pl.BlockSpec((16, BLOCK), lambda b, i: (0, b * nb + i)),
                  blk(512, 1),
                  pl.BlockSpec((seq, LANES), lambda b, i: (b, 0)),
                  pl.BlockSpec((seq, LANES), lambda b, i: (b, 1)),
                  pl.BlockSpec((nvt, 256, PROJ_ROWS), lambda b, i: (b, 0, 0))],
        out_specs=blk(512),
        out_shape=jax.ShapeDtypeStruct((n, 512), _BF16),
        scratch_shapes=[pltpu.VMEM((seq, LANES), _F32),
                        pltpu.VMEM((seq, LANES), _BF16),
                        pltpu.VMEM((IDX_HEADS * BLOCK, LANES), _BF16),
                        pltpu.VMEM((DSA_Q_HEADS * BLOCK, LANES), _BF16),
                        pltpu.VMEM((seq, LANES), _BF16),
                        pltpu.VMEM((seq, LANES), _BF16),
                        pltpu.VMEM((8, LANES), _F32),
                        pltpu.VMEM((LANES, 512), _F32),
                        pltpu.VMEM((LANES, 512), _F32)],
        compiler_params=pltpu.CompilerParams(dimension_semantics=("arbitrary", "arbitrary"),
                                             vmem_limit_bytes=VMEM_LIMIT),
        name="dsa_attention",
    )(iq, bq, iwt, gates, ki2, ki2, bvt)


def _out_kernel(a_ref, b_ref, x_ref, w_ref, bo_ref, gain_ref, bias_ref, out_ref, *, alpha):
    half = a_ref.shape[1]
    y = jnp.dot(a_ref[...], w_ref[0:half, :], preferred_element_type=_F32)
    y = y + jnp.dot(b_ref[...], w_ref[half:2 * half, :], preferred_element_type=_F32)
    z = alpha * x_ref[...] + (y + bo_ref[...])
    mu = jnp.mean(z, axis=-1, keepdims=True)
    zc = z - mu
    var = jnp.mean(zc * zc, axis=-1, keepdims=True)
    out_ref[...] = zc * lax.rsqrt(var + LN_EPS) * gain_ref[...] + bias_ref[...]


def _out_projection(a, b, x2, w_out, b_out, gain, bias, alpha):
    n, d = x2.shape
    tm = PROJ_ROWS
    row = lambda w: pl.BlockSpec((tm, w), lambda i: (i, 0))
    full = lambda a_: pl.BlockSpec(a_.shape, lambda i: (0,) * a_.ndim)
    consts = (w_out.astype(_BF16), b_out[None, :], gain[None, :], bias[None, :])
    return pl.pallas_call(
        functools.partial(_out_kernel, alpha=alpha),
        grid=(n // tm,),
        in_specs=[row(a.shape[1]), row(b.shape[1]), row(d)] + [full(c) for c in consts],
        out_specs=row(d),
        out_shape=jax.ShapeDtypeStruct((n, d), x2.dtype),
        compiler_params=pltpu.CompilerParams(dimension_semantics=("arbitrary",),
                                             vmem_limit_bytes=VMEM_LIMIT),
        name="out_projection",
    )(a, b, x2, *consts)


def _layer(h, pos2, w_in, b_in, sinks, w_out, b_out, gain, bias, alpha):
    batch, seq, d = h.shape
    x2 = h.reshape(batch * seq, d)
    aq, ak2, bq, iq, ki2, av2, gates, bvt, iwt = _projection(x2, pos2, w_in, b_in)
    a = _swa_attention(aq, ak2, av2, gates, sinks, batch, seq)
    b = _dsa_attention(iq, bq, iwt, gates, ki2, bvt, batch, seq)
    return _out_projection(a, b, x2, w_out, b_out, gain, bias, alpha).reshape(batch, seq, d)


def kernel(x, positions, w_in, b_in, swa_sinks, w_out, b_out, ln_gain, ln_bias):
    batch, seq, d = x.shape
    depth = w_in.shape[0]
    assert d == 1024 and seq % KEY_CHUNK == 0 and (batch * seq) % PROJ_ROWS == 0
    assert KEY_CHUNK % PROJ_ROWS == 0 and KEY_CHUNK % TIE_CHUNK == 0
    alpha = (2.0 * depth) ** 0.25
    pos2 = positions.reshape(batch * seq, 1)
    h = x
    for layer in range(depth):
        h = _layer(h, pos2, w_in[layer], b_in[layer], swa_sinks[layer], w_out[layer], b_out[layer],
                   ln_gain[layer], ln_bias[layer], alpha)
    return h
```

```python
import functools

import numpy as np
import jax
import jax.numpy as jnp
from jax import lax
from jax.experimental import pallas as pl
from jax.experimental.pallas import tpu as pltpu

HEAD_DIM = 64
SWA_Q_HEADS = 8
SWA_KV_HEADS = 2
DSA_Q_HEADS = 8
IDX_HEADS = 4
IDX_DIM = 64
WINDOW = 128
BLOCK = 128
TOPK_MAX = 256
ROPE_THETA = 10000.0
LN_EPS = 1e-5

LANES = 128
PROJ_ROWS = 512
KEY_CHUNK = 1024
TIE_CHUNK = 512
VMEM_LIMIT = 56 * 1024 * 1024

LOG2E = 1.4426950408889634
NEG_BIG = -1e30
SHIFT_LIMIT = 60.0
F32_LOWEST = -3.4028234663852886e38
INT_MAX = 2 ** 31 - 1
KEY_NEG_INF = 0x807FFFFF - 2 ** 32
KEY_MIN_NORMAL = 0x00800000
SEARCH_GROUP = 3

_F32 = jnp.float32
_BF16 = jnp.bfloat16
_I32 = jnp.int32
_NT = (((1,), (1,)), ((), ()))


def _split_columns(m):
    swa_w = SWA_Q_HEADS * HEAD_DIM
    kv_w = SWA_KV_HEADS * HEAD_DIM
    dsa_w = DSA_Q_HEADS * HEAD_DIM
    sizes = (swa_w, kv_w, kv_w, swa_w, dsa_w, HEAD_DIM, HEAD_DIM, dsa_w,
             IDX_HEADS * IDX_DIM, IDX_DIM, IDX_HEADS)
    o = np.cumsum((0,) + sizes)
    aq, ak, av, ag, bq, bk, bv, bg, iq, ik, iw = [m[:, o[i]:o[i + 1]] for i in range(len(sizes))]

    def dup(c):
        return jnp.concatenate([c[:, j:j + HEAD_DIM] for j in range(0, c.shape[1], HEAD_DIM) for _ in (0, 1)],
                               axis=1)

    def rot(c):
        r = c.reshape(c.shape[0], -1, 2, HEAD_DIM // 2)[:, :, ::-1, :]
        return r.reshape(c.shape[0], -1)

    roped = jnp.concatenate([aq, dup(ak), bq, iq, dup(bk), dup(ik)], axis=1)
    return jnp.concatenate([roped, ag, bg], axis=1), rot(roped), (bv, av[:, :HEAD_DIM], av[:, HEAD_DIM:]), iw


def _proj_kernel(x_ref, pos_ref, inv_ref, sgn_ref, wm_ref, wr_ref, bm_ref, br_ref,
                 wvt_ref, bvt_ref, wiw_ref, biw_ref,
                 aq_ref, ak2_ref, bq_ref, iq_ref, ki2_ref, g_ref, bvt_out_ref, avt_out_ref, iwt_ref):
    xb = x_ref[...].astype(_BF16)
    ang = pos_ref[...].astype(_F32) * inv_ref[...]
    cos = jnp.cos(ang)
    sin = jnp.sin(ang) * sgn_ref[...]

    q_scale = 0.125 * LOG2E
    roped_dst = ((aq_ref, 0, q_scale), (aq_ref, 256, q_scale), (ak2_ref, 0, 1.0), (bq_ref, 0, q_scale),
                 (bq_ref, 256, q_scale), (iq_ref, 0, 1.0), (ki2_ref, 0, 1.0))
    for g, (dst, off, scale) in enumerate(roped_dst):
        c0 = 256 * g
        hm = jnp.dot(xb, wm_ref[:, c0:c0 + 256], preferred_element_type=_F32) + bm_ref[:, c0:c0 + 256]
        hr = jnp.dot(xb, wr_ref[:, c0:c0 + 256], preferred_element_type=_F32) + br_ref[:, c0:c0 + 256]
        for s in range(2):
            sl = slice(LANES * s, LANES * (s + 1))
            o = hm[:, sl] * cos + hr[:, sl] * sin
            if scale != 1.0:
                o = o * scale
            dst[:, off + LANES * s:off + LANES * (s + 1)] = o.astype(dst.dtype)

    n_roped = 256 * len(roped_dst)
    for g in range(4):
        c0 = n_roped + 256 * g
        h = jnp.dot(xb, wm_ref[:, c0:c0 + 256], preferred_element_type=_F32) + bm_ref[:, c0:c0 + 256]
        g_ref[:, 256 * g:256 * (g + 1)] = h * (1.0 / (1.0 + jnp.exp(-h)))

    vt = lax.dot_general(wvt_ref[...], xb, _NT, preferred_element_type=_F32) + bvt_ref[...]
    n_b = bvt_out_ref.shape[1]
    bvt_out_ref[0] = vt[0:n_b].astype(bvt_out_ref.dtype)
    avt_out_ref[0] = vt[n_b:].astype(avt_out_ref.dtype)
    iwt = lax.dot_general(wiw_ref[...], xb, _NT, preferred_element_type=_F32) + biw_ref[...]
    iwt_ref[...] = iwt * (IDX_HEADS ** -0.5 * IDX_DIM ** -0.5)


def _projection(x2, pos2, w_in, b_in):
    n, d = x2.shape
    tm = PROJ_ROWS
    wm, wr, wvs, wiw = _split_columns(w_in)
    bm, br, bvs, biw = _split_columns(b_in[None, :])
    wm, wr = wm.astype(_BF16), wr.astype(_BF16)
    zpad = jnp.zeros((HEAD_DIM, d), w_in.dtype)
    one_row = jnp.zeros((HEAD_DIM,), b_in.dtype).at[0].set(1.0)
    wvt = jnp.concatenate([p for wv in wvs for p in (wv.T, zpad, zpad, wv.T)], axis=0).astype(_BF16)
    bvt = jnp.concatenate([p for bv in bvs for p in (bv[0], one_row, one_row, bv[0])])[:, None]
    wiw_t = jnp.concatenate([wiw.T, jnp.zeros((16 - IDX_HEADS, d), w_in.dtype)], axis=0).astype(_BF16)
    biw_t = jnp.concatenate([biw[0], jnp.zeros((16 - IDX_HEADS,), b_in.dtype)])[:, None]
    half = HEAD_DIM // 2
    inv = ROPE_THETA ** (-jnp.arange(0, HEAD_DIM, 2, dtype=_F32) / HEAD_DIM)
    inv128 = jnp.tile(jnp.concatenate([inv, inv]), LANES // HEAD_DIM)[None, :]
    sgn128 = jnp.tile(jnp.concatenate([-jnp.ones((half,), _F32), jnp.ones((half,), _F32)]),
                      LANES // HEAD_DIM)[None, :]

    row = lambda w: pl.BlockSpec((tm, w), lambda i: (i, 0))
    full = lambda a: pl.BlockSpec(a.shape, lambda i: (0,) * a.ndim)
    out_shape = (
        jax.ShapeDtypeStruct((n, 512), _BF16),
        jax.ShapeDtypeStruct((n, 256), _BF16),
        jax.ShapeDtypeStruct((n, 512), _BF16),
        jax.ShapeDtypeStruct((n, 256), _BF16),
        jax.ShapeDtypeStruct((n, 256), _BF16),
        jax.ShapeDtypeStruct((n, 1024), _F32),
        jax.ShapeDtypeStruct((n // tm, 256, tm), _BF16),
        jax.ShapeDtypeStruct((n // tm, 512, tm), _BF16),
        jax.ShapeDtypeStruct((16, n), _F32),
    )
    out_specs = (row(512), row(256), row(512), row(256), row(256), row(1024),
                 pl.BlockSpec((1, 256, tm), lambda i: (i, 0, 0)),
                 pl.BlockSpec((1, 512, tm), lambda i: (i, 0, 0)),
                 pl.BlockSpec((16, tm), lambda i: (0, i)))
    consts = (inv128, sgn128, wm, wr, bm, br, wvt, bvt, wiw_t, biw_t)
    return pl.pallas_call(
        _proj_kernel,
        grid=(n // tm,),
        in_specs=[row(d), row(1)] + [full(a) for a in consts],
        out_specs=out_specs,
        out_shape=out_shape,
        compiler_params=pltpu.CompilerParams(dimension_semantics=("arbitrary",),
                                             vmem_limit_bytes=VMEM_LIMIT),
        name="in_projection",
    )(x2, pos2, *consts)


def _swa_kernel(q_ref, kc_ref, kp_ref, vc_ref, vp_ref, g_ref, sink_ref, out_ref):
    i = pl.program_id(1)
    tq = BLOCK
    n_sub = q_ref.shape[0] // tq
    group = SWA_Q_HEADS // SWA_KV_HEADS
    lane = lax.broadcasted_iota(_I32, (tq, LANES), 1)
    lo_half = lane < HEAD_DIM
    r = lax.broadcasted_iota(_I32, (2 * tq, LANES), 0)
    c = lax.broadcasted_iota(_I32, (2 * tq, LANES), 1)
    in_window = (c < r) & (r <= c + WINDOW)
    row = lax.broadcasted_iota(_I32, (LANES, LANES), 0)

    kwin = jnp.concatenate([kp_ref[...], kc_ref[...]], axis=0)
    vwin = jnp.concatenate([vp_ref[0], vc_ref[0]], axis=1)
    for g in range(SWA_KV_HEADS):
        k = kwin[:, LANES * g:LANES * (g + 1)]
        for j in range(n_sub):
            valid = in_window if j > 0 else in_window & ((r >= tq) | (i > 0))
            rows = slice(tq * j, tq * (j + 1))
            win = slice(tq * j, tq * (j + 2))
            tiles = []
            for odd in (0, 1):
                qs = []
                for s in range(group // 2):
                    slab = q_ref[rows, LANES * (g * (group // 2) + s):LANES * (g * (group // 2) + s + 1)]
                    qs.append(jnp.where(lo_half if odd == 0 else ~lo_half, slab.astype(_F32), 0.0).astype(_BF16))
                qop = jnp.concatenate(qs, axis=0)
                logits = lax.dot_general(k[win], qop, _NT, preferred_element_type=_F32)
                heads = [group * g + 2 * s + odd for s in range(group // 2)]
                ps, sink_terms = [], []
                for s, h in enumerate(heads):
                    lg = jnp.where(valid, logits[:, LANES * s:LANES * (s + 1)], -jnp.inf)
                    sink = sink_ref[h:h + 1, :]
                    m = jnp.maximum(jnp.max(lg, axis=0, keepdims=True), sink)
                    ps.append(jnp.exp2(lg - m).astype(_BF16))
                    sink_terms.append(jnp.exp2(sink - m))
                v = vwin[LANES * (2 * g + odd):LANES * (2 * g + odd + 1), win]
                acc = jnp.dot(v, jnp.concatenate(ps, axis=1), preferred_element_type=_F32)
                l_row = HEAD_DIM if odd == 0 else 0
                denom = acc[l_row:l_row + 1, :] + jnp.concatenate(sink_terms, axis=1)
                tiles.append(acc / denom)
            for s in range(group // 2):
                sl = slice(LANES * s, LANES * (s + 1))
                tile = jnp.where(row < HEAD_DIM, tiles[0][:, sl], tiles[1][:, sl])
                cols = slice(LANES * (g * (group // 2) + s), LANES * (g * (group // 2) + s + 1))
                out_ref[rows, cols] = (tile.T * g_ref[rows, cols]).astype(out_ref.dtype)


def _swa_attention(aq, ak2, avt, gates, sinks, batch, seq):
    tm = PROJ_ROWS
    ns = seq // tm
    n = batch * seq
    per = tm // BLOCK
    sink_b = jnp.broadcast_to((sinks.astype(_F32) * LOG2E)[:, None], (SWA_Q_HEADS, LANES))
    cur = lambda w: pl.BlockSpec((tm, w), lambda b, i: (b * ns + i, 0))
    return pl.pallas_call(
        _swa_kernel,
        grid=(batch, ns),
        in_specs=[cur(512), cur(256),
                  pl.BlockSpec((BLOCK, 256), lambda b, i: (b * ns * per + jnp.maximum(i * per - 1, 0), 0)),
                  pl.BlockSpec((1, 512, tm), lambda b, i: (b * ns + i, 0, 0)),
                  pl.BlockSpec((1, 512, BLOCK), lambda b, i: (b * ns + jnp.maximum(i - 1, 0), 0, per - 1)),
                  cur(512),
                  pl.BlockSpec((SWA_Q_HEADS, LANES), lambda b, i: (0, 0))],
        out_specs=cur(512),
        out_shape=jax.ShapeDtypeStruct((n, 512), _BF16),
        compiler_params=pltpu.CompilerParams(dimension_semantics=("arbitrary", "arbitrary"),
                                             vmem_limit_bytes=VMEM_LIMIT),
        name="swa_attention",
    )(aq, ak2, ak2, avt, avt, gates, sink_b)


_DSA_HEAD_ORDER = (0, 2, 4, 6, 1, 3, 5, 7)


def _key_to_float(k):
    return lax.bitcast_convert_type(jnp.where(k < 0, k ^ 0x7FFFFFFF, k), _F32)


def _key16_to_float(k):
    return lax.bitcast_convert_type(jnp.where(k <= 0, k ^ 0x7FFF ^ jnp.where(k == 0, -1, 0), k) << 16, _F32)


def _midpoint(a, b):
    return (a >> 1) + (b >> 1) + (a & b & 1)


def _tree_sum(parts, chains=8):
    accs = list(parts[:chains])
    for j in range(chains, len(parts)):
        accs[j % chains] = accs[j % chains] + parts[j]
    while len(accs) > 1:
        accs = [accs[2 * j] + accs[2 * j + 1] for j in range(len(accs) // 2)]
    return accs[0]


def _dsa_kernel(iq_ref, bq_ref, iwt_ref, g_ref, bk2_ref, ik2_ref, bvt_ref, out_ref,
                sc_ref, sc16_ref, iqm_ref, bqm_ref, ka_ref, kb_ref, kmax_ref, acc_e_ref, acc_o_ref, *, n_sel):
    tq, kc = BLOCK, KEY_CHUNK
    nh, half = DSA_Q_HEADS, DSA_Q_HEADS // 2
    i = pl.program_id(1)
    q0 = i * tq
    nch = (q0 + tq + kc - 1) // kc
    lane = lax.broadcasted_iota(_I32, (tq, LANES), 1)
    lo_half = lane < HEAD_DIM

    @pl.when(i == 0)
    def _():
        klane = lax.broadcasted_iota(_I32, (kc, LANES), 1)
        klo = klane < HEAD_DIM

        def body(c, mx):
            k0 = pl.multiple_of(c * kc, kc)
            k = bk2_ref[pl.ds(k0, kc), :].astype(_F32)
            ka_ref[pl.ds(k0, kc), :] = jnp.where(klo, k, jnp.where(klane == HEAD_DIM, 1.0, 0.0)).astype(_BF16)
            kb_ref[pl.ds(k0, kc), :] = jnp.where(klo, jnp.where(klane == 0, 1.0, 0.0), k).astype(_BF16)
            return jnp.maximum(mx, jnp.sum(jnp.where(klo, k * k, 0.0), axis=1, keepdims=True))

        mx = lax.fori_loop(0, sc_ref.shape[0] // kc, body, jnp.zeros((kc, 1), _F32))
        kmax_ref[...] = jnp.broadcast_to(jnp.sqrt(jnp.max(mx, axis=0, keepdims=True)), kmax_ref.shape)

    def masked_half(ref, h):
        slab = ref[:, LANES * (h // 2):LANES * (h // 2 + 1)].astype(_F32)
        return jnp.where(lo_half if h % 2 == 0 else ~lo_half, slab, 0.0)

    for h in range(IDX_HEADS):
        iqm_ref[h * tq:(h + 1) * tq, :] = masked_half(iq_ref, h).astype(_BF16)
    kmax = kmax_ref[0:1, 0:1]
    shift_max = jnp.zeros((tq, 1), _F32)
    for p, h in enumerate(_DSA_HEAD_ORDER):
        qm = masked_half(bq_ref, h)
        shift = jnp.sqrt(jnp.sum(qm * qm, axis=1, keepdims=True)) * kmax
        shift_max = jnp.maximum(shift_max, shift)
        one_lane = HEAD_DIM if h % 2 == 0 else 0
        bqm_ref[p * tq:(p + 1) * tq, :] = jnp.where(lane == one_lane, -shift, qm).astype(_BF16)
    bounded = jnp.max(shift_max) <= SHIFT_LIMIT

    w = [iwt_ref[h:h + 1, :] for h in range(IDX_HEADS)]
    qidx = q0 + lax.broadcasted_iota(_I32, (kc, LANES), 1)
    krow = lax.broadcasted_iota(_I32, (kc, LANES), 0)

    def score_chunk(c, diagonal):
        k0 = pl.multiple_of(c * kc, kc)
        z = lax.dot_general(ik2_ref[pl.ds(k0, kc), :], iqm_ref[...], _NT, preferred_element_type=_F32)
        sc = w[0] * jnp.maximum(z[:, 0:LANES], 0.0)
        for h in range(1, IDX_HEADS):
            sc = sc + w[h] * jnp.maximum(z[:, LANES * h:LANES * (h + 1)], 0.0)
        if diagonal:
            sc = jnp.where(k0 + krow <= qidx, sc, -jnp.inf)
        sc_ref[pl.ds(k0, kc), :] = sc
        top = lax.bitcast_convert_type(sc, _I32) & jnp.int32(-65536)
        sc16_ref[pl.ds(k0, kc), :] = lax.bitcast_convert_type(top, _F32).astype(_BF16)

    def score_body(c, carry):
        score_chunk(c, False)
        return carry

    lax.fori_loop(0, nch - 1, score_body, 0)
    score_chunk(nch - 1, True)

    def count_ge(t):
        def body(c, acc):
            k0 = pl.multiple_of(c * kc, kc)
            for h in range(0, kc, 512):
                ind = jnp.where(sc_ref[pl.ds(k0 + h, 512), :] >= t, 1, 0)
                acc = acc + jnp.sum(ind.reshape(64, 8, LANES), axis=0)
            return acc
        acc = lax.fori_loop(0, nch, body, jnp.zeros((8, LANES), _I32))
        return jnp.sum(acc, axis=0, keepdims=True)

    one16, zero16 = jnp.ones((), _BF16), jnp.zeros((), _BF16)

    def count_ge16(t):
        t = t.astype(_BF16)

        def body(c, acc):
            k0 = pl.multiple_of(c * kc, kc)
            d = pltpu.bitcast(sc16_ref[pl.ds(k0, kc), :] - t, _I32)
            neg = lax.shift_right_logical(d, 15) & 0x00010001
            return acc + jnp.sum(neg.reshape(kc // 16, 8, LANES), axis=0)
        acc = lax.fori_loop(0, nch, body, jnp.zeros((8, LANES), _I32))
        below = jnp.sum((acc & 0xFFFF) + lax.shift_right_logical(acc, 16), axis=0, keepdims=True)
        return nch * kc - below

    def search_pass(count, to_thr, st):
        lo, hi, clo, chi, t, thr = st
        t_up, t_dn = _midpoint(t, hi), _midpoint(lo, t)
        thr_up, thr_dn = to_thr(t_up), to_thr(t_dn)
        cnt = count(thr)
        ok = cnt >= n_sel
        return (jnp.where(ok, t, lo), jnp.where(ok, hi, t), jnp.where(ok, cnt, clo), jnp.where(ok, chi, cnt),
                jnp.where(ok, t_up, t_dn), jnp.where(ok, thr_up, thr_dn))

    vec = lambda v: jnp.full((1, LANES), v, _I32)
    st = (vec(KEY_NEG_INF >> 16), vec(1 << 15), vec(0) + nch * kc, vec(0), vec(0), _key16_to_float(vec(0)))
    st = lax.fori_loop(0, 16, lambda j, s: search_pass(count_ge16, _key16_to_float, s), st)
    hint_lo = jnp.maximum(st[0] << 16, KEY_NEG_INF)
    hint_hi = jnp.where(st[1] >= (1 << 15), INT_MAX, st[1] << 16)
    hint_hi = jnp.where((hint_hi > 0) & (hint_hi < KEY_MIN_NORMAL), KEY_MIN_NORMAL, hint_hi)
    c1 = count_ge(_key_to_float(hint_lo))
    ok1 = c1 >= n_sel
    lo, clo = jnp.where(ok1, hint_lo, KEY_NEG_INF), jnp.where(ok1, c1, nch * kc)
    hi, chi = jnp.where(ok1, INT_MAX, hint_lo), jnp.where(ok1, 0, c1)
    c2 = count_ge(_key_to_float(hint_hi))
    up2 = (c2 >= n_sel) & (hint_hi > lo)
    dn2 = (c2 < n_sel) & (hint_hi < hi)
    lo, clo = jnp.where(up2, hint_hi, lo), jnp.where(up2, c2, clo)
    hi, chi = jnp.where(dn2, hint_hi, hi), jnp.where(dn2, c2, chi)

    def unsettled(lo, hi, clo):
        return ~((clo == n_sel) | (hi - 1 <= lo) | ((lo >= 0) & (hi <= KEY_MIN_NORMAL)))

    def fine_group(c):
        s = c[1:7]
        for _ in range(SEARCH_GROUP):
            s = search_pass(count_ge, _key_to_float, s)
        return (c[0] + 1,) + s + (jnp.max(jnp.where(unsettled(*s[:3]), 1, 0)),)

    t0 = _midpoint(lo, hi)
    init = (jnp.int32(0), lo, hi, clo, chi, t0, _key_to_float(t0),
            jnp.max(jnp.where(unsettled(lo, hi, clo), 1, 0)))
    _, lo, hi, clo, chi, _, _, _ = lax.while_loop(lambda c: (c[7] > 0) & (c[0] < 16), fine_group, init)
    t_lo = _key_to_float(lo)
    t_hi = _key_to_float(hi)

    need = (clo > n_sel) & (lo > KEY_NEG_INF)

    @pl.when(jnp.max(jnp.where(need, 1, 0)) > 0)
    def _():
        tk = TIE_CHUNK
        room = (n_sel - chi).astype(_F32)
        before = (lax.broadcasted_iota(_I32, (tk, tk), 0) > lax.broadcasted_iota(_I32, (tk, tk), 1))
        before = jnp.where(before, 1.0, 0.0).astype(_BF16)

        def body(c, seen):
            for h in range(kc // tk):
                k0 = pl.multiple_of(c * kc + h * tk, tk)
                x = sc_ref[pl.ds(k0, tk), :]
                tie = (x >= t_lo) & ~(x >= t_hi)
                tf = jnp.where(tie, 1.0, 0.0)
                rank = seen + jnp.dot(before, tf.astype(_BF16), preferred_element_type=_F32)
                sc_ref[pl.ds(k0, tk), :] = jnp.where(tie & (rank >= room), -jnp.inf, x)
                seen = seen + jnp.sum(tf, axis=0, keepdims=True)
            return seen

        lax.fori_loop(0, nch, body, jnp.zeros((1, LANES), _F32))

    thr = jnp.maximum(t_lo, F32_LOWEST)

    acc_e_ref[...] = jnp.zeros_like(acc_e_ref)
    acc_o_ref[...] = jnp.zeros_like(acc_o_ref)
    per_vt = kc // bvt_ref.shape[2]

    def masked_logits(c):
        k0 = pl.multiple_of(c * kc, kc)
        sel = sc_ref[pl.ds(k0, kc), :] >= thr
        lge = lax.dot_general(ka_ref[pl.ds(k0, kc), :], bqm_ref[0:half * tq, :], _NT,
                              preferred_element_type=_F32)
        lgo = lax.dot_general(kb_ref[pl.ds(k0, kc), :], bqm_ref[half * tq:nh * tq, :], _NT,
                              preferred_element_type=_F32)
        tiles = lambda lg: [jnp.where(sel, lg[:, LANES * j:LANES * (j + 1)], NEG_BIG) for j in range(half)]
        vt = jnp.concatenate([bvt_ref[c * per_vt + j] for j in range(per_vt)], axis=1)
        return ((acc_e_ref, tiles(lge), vt[0:LANES, :]), (acc_o_ref, tiles(lgo), vt[LANES:2 * LANES, :]))

    @pl.when(bounded)
    def _():
        def body(c, carry):
            for ref, tiles, v in masked_logits(c):
                p = jnp.concatenate([jnp.exp2(t).astype(_BF16) for t in tiles], axis=1)
                ref[...] += jnp.dot(v, p, preferred_element_type=_F32)
            return carry
        lax.fori_loop(0, nch, body, 0)

    @pl.when(jnp.logical_not(bounded))
    def _():
        def body(c, m):
            m_out = []
            for g, (ref, tiles, v) in enumerate(masked_logits(c)):
                ps, alphas = [], []
                for j, t in enumerate(tiles):
                    sl = slice(LANES * (g * half + j), LANES * (g * half + j + 1))
                    m_new = jnp.maximum(m[:, sl], jnp.max(t, axis=0, keepdims=True))
                    ps.append(jnp.exp2(t - m_new).astype(_BF16))
                    alphas.append(jnp.exp2(m[:, sl] - m_new))
                    m_out.append(m_new)
                ref[...] = ref[...] * jnp.concatenate(alphas, axis=1) + jnp.dot(
                    v, jnp.concatenate(ps, axis=1), preferred_element_type=_F32)
            return jnp.concatenate(m_out, axis=1)
        lax.fori_loop(0, nch, body, jnp.full((1, nh * LANES), NEG_BIG, _F32))

    row = lax.broadcasted_iota(_I32, (LANES, LANES), 0)
    for j in range(half):
        sl = slice(LANES * j, LANES * (j + 1))
        even = acc_e_ref[:, sl] / acc_e_ref[HEAD_DIM:HEAD_DIM + 1, sl]
        odd = acc_o_ref[:, sl] / acc_o_ref[0:1, sl]
        tile = jnp.where(row < HEAD_DIM, even, odd)
        out_ref[:, sl] = (tile.T * g_ref[:, sl]).astype(out_ref.dtype)


def _dsa_attention(iq, bq, iwt, gates, ki2, bvt, batch, seq):
    nb = seq // BLOCK
    n = batch * seq
    nvt = seq // PROJ_ROWS
    n_sel = min(TOPK_MAX, seq // 4)
    blk = lambda w, col=0: pl.BlockSpec((BLOCK, w), lambda b, i: (b * nb + i, col))
    return pl.pallas_call(
        functools.partial(_dsa_kernel, n_sel=n_sel),
        grid=(batch, nb),
        in_specs=[blk(256), blk(512),
                  pl.BlockSpec((16, BLOCK), lambda b, i: (0, b * nb + i)),
                  blk(512, 1),
                  pl.BlockSpec((seq, LANES), lambda b, i: (b, 0)),
                  pl.BlockSpec((seq, LANES), lambda b, i: (b, 1)),
                  pl.BlockSpec((nvt, 256, PROJ_ROWS), lambda b, i: (b, 0, 0))],
        out_specs=blk(512),
        out_shape=jax.ShapeDtypeStruct((n, 512), _BF16),
        scratch_shapes=[pltpu.VMEM((seq, LANES), _F32),
                        pltpu.VMEM((seq, LANES), _BF16),
                        pltpu.VMEM((IDX_HEADS * BLOCK, LANES), _BF16),
                        pltpu.VMEM((DSA_Q_HEADS * BLOCK, LANES), _BF16),
                        pltpu.VMEM((seq, LANES), _BF16),
                        pltpu.VMEM((seq, LANES), _BF16),
                        pltpu.VMEM((8, LANES), _F32),
                        pltpu.VMEM((LANES, 512), _F32),
                        pltpu.VMEM((LANES, 512), _F32)],
        compiler_params=pltpu.CompilerParams(dimension_semantics=("arbitrary", "arbitrary"),
                                             vmem_limit_bytes=VMEM_LIMIT),
        name="dsa_attention",
    )(iq, bq, iwt, gates, ki2, ki2, bvt)


def _out_kernel(a_ref, b_ref, x_ref, w_ref, bo_ref, gain_ref, bias_ref, out_ref, *, alpha):
    half = a_ref.shape[1]
    y = jnp.dot(a_ref[...], w_ref[0:half, :], preferred_element_type=_F32)
    y = y + jnp.dot(b_ref[...], w_ref[half:2 * half, :], preferred_element_type=_F32)
    z = alpha * x_ref[...] + (y + bo_ref[...])
    mu = jnp.mean(z, axis=-1, keepdims=True)
    zc = z - mu
    var = jnp.mean(zc * zc, axis=-1, keepdims=True)
    out_ref[...] = zc * lax.rsqrt(var + LN_EPS) * gain_ref[...] + bias_ref[...]


def _out_projection(a, b, x2, w_out, b_out, gain, bias, alpha):
    n, d = x2.shape
    tm = PROJ_ROWS
    row = lambda w: pl.BlockSpec((tm, w), lambda i: (i, 0))
    full = lambda a_: pl.BlockSpec(a_.shape, lambda i: (0,) * a_.ndim)
    consts = (w_out.astype(_BF16), b_out[None, :], gain[None, :], bias[None, :])
    return pl.pallas_call(
        functools.partial(_out_kernel, alpha=alpha),
        grid=(n // tm,),
        in_specs=[row(a.shape[1]), row(b.shape[1]), row(d)] + [full(c) for c in consts],
        out_specs=row(d),
        out_shape=jax.ShapeDtypeStruct((n, d), x2.dtype),
        compiler_params=pltpu.CompilerParams(dimension_semantics=("arbitrary",),
                                             vmem_limit_bytes=VMEM_LIMIT),
        name="out_projection",
    )(a, b, x2, *consts)


def _layer(h, pos2, w_in, b_in, sinks, w_out, b_out, gain, bias, alpha):
    batch, seq, d = h.shape
    x2 = h.reshape(batch * seq, d)
    aq, ak2, bq, iq, ki2, gates, bvt, avt, iwt = _projection(x2, pos2, w_in, b_in)
    a = _swa_attention(aq, ak2, avt, gates, sinks, batch, seq)
    b = _dsa_attention(iq, bq, iwt, gates, ki2, bvt, batch, seq)
    return _out_projection(a, b, x2, w_out, b_out, gain, bias, alpha).reshape(batch, seq, d)


def kernel(x, positions, w_in, b_in, swa_sinks, w_out, b_out, ln_gain, ln_bias):
    batch, seq, d = x.shape
    depth = w_in.shape[0]
    assert d == 1024 and seq % KEY_CHUNK == 0 and (batch * seq) % PROJ_ROWS == 0
    assert KEY_CHUNK % PROJ_ROWS == 0 and KEY_CHUNK % TIE_CHUNK == 0
    alpha = (2.0 * depth) ** 0.25
    pos2 = positions.reshape(batch * seq, 1)
    h = x
    for layer in range(depth):
        h = _layer(h, pos2, w_in[layer], b_in[layer], swa_sinks[layer], w_out[layer], b_out[layer],
                   ln_gain[layer], ln_bias[layer], alpha)
    return h
```

```python
import functools

import numpy as np
import jax
import jax.numpy as jnp
from jax import lax
from jax.experimental import pallas as pl
from jax.experimental.pallas import tpu as pltpu

HEAD_DIM = 64
SWA_Q_HEADS = 8
SWA_KV_HEADS = 2
DSA_Q_HEADS = 8
IDX_HEADS = 4
IDX_DIM = 64
WINDOW = 128
BLOCK = 128
TOPK_MAX = 256
ROPE_THETA = 10000.0
LN_EPS = 1e-5

LANES = 128
PROJ_ROWS = 512
KEY_CHUNK = 1024
TIE_CHUNK = 512
VMEM_LIMIT = 56 * 1024 * 1024

LOG2E = 1.4426950408889634
NEG_BIG = -1e30
SHIFT_LIMIT = 60.0
F32_LOWEST = -3.4028234663852886e38
INT_MAX = 2 ** 31 - 1
KEY_NEG_INF = 0x807FFFFF - 2 ** 32
KEY_MIN_NORMAL = 0x00800000
SEARCH_GROUP = 3
COARSE_SAMPLE = 8
COARSE_WINDOW = 64
COARSE_STEPS = 7

_F32 = jnp.float32
_BF16 = jnp.bfloat16
_I32 = jnp.int32
_NT = (((1,), (1,)), ((), ()))


def _split_columns(m):
    swa_w = SWA_Q_HEADS * HEAD_DIM
    kv_w = SWA_KV_HEADS * HEAD_DIM
    dsa_w = DSA_Q_HEADS * HEAD_DIM
    sizes = (swa_w, kv_w, kv_w, swa_w, dsa_w, HEAD_DIM, HEAD_DIM, dsa_w,
             IDX_HEADS * IDX_DIM, IDX_DIM, IDX_HEADS)
    o = np.cumsum((0,) + sizes)
    aq, ak, av, ag, bq, bk, bv, bg, iq, ik, iw = [m[:, o[i]:o[i + 1]] for i in range(len(sizes))]

    def dup(c):
        return jnp.concatenate([c[:, j:j + HEAD_DIM] for j in range(0, c.shape[1], HEAD_DIM) for _ in (0, 1)],
                               axis=1)

    def rot(c):
        r = c.reshape(c.shape[0], -1, 2, HEAD_DIM // 2)[:, :, ::-1, :]
        return r.reshape(c.shape[0], -1)

    roped = jnp.concatenate([aq, dup(ak), bq, iq, dup(bk), dup(ik)], axis=1)
    return jnp.concatenate([roped, ag, bg], axis=1), rot(roped), (bv, av[:, :HEAD_DIM], av[:, HEAD_DIM:]), iw


def _proj_kernel(x_ref, pos_ref, inv_ref, sgn_ref, wm_ref, wr_ref, bm_ref, br_ref,
                 wvt_ref, bvt_ref, wiw_ref, biw_ref,
                 aq_ref, ak2_ref, bq_ref, iq_ref, ki2_ref, g_ref, bvt_out_ref, avt_out_ref, iwt_ref):
    xb = x_ref[...].astype(_BF16)
    ang = pos_ref[...].astype(_F32) * inv_ref[...]
    cos = jnp.cos(ang)
    sin = jnp.sin(ang) * sgn_ref[...]

    q_scale = 0.125 * LOG2E
    roped_dst = ((aq_ref, 0, q_scale), (aq_ref, 256, q_scale), (ak2_ref, 0, 1.0), (bq_ref, 0, q_scale),
                 (bq_ref, 256, q_scale), (iq_ref, 0, 1.0), (ki2_ref, 0, 1.0))
    for g, (dst, off, scale) in enumerate(roped_dst):
        c0 = 256 * g
        hm = jnp.dot(xb, wm_ref[:, c0:c0 + 256], preferred_element_type=_F32) + bm_ref[:, c0:c0 + 256]
        hr = jnp.dot(xb, wr_ref[:, c0:c0 + 256], preferred_element_type=_F32) + br_ref[:, c0:c0 + 256]
        for s in range(2):
            sl = slice(LANES * s, LANES * (s + 1))
            o = hm[:, sl] * cos + hr[:, sl] * sin
            if scale != 1.0:
                o = o * scale
            dst[:, off + LANES * s:off + LANES * (s + 1)] = o.astype(dst.dtype)

    n_roped = 256 * len(roped_dst)
    for g in range(4):
        c0 = n_roped + 256 * g
        h = jnp.dot(xb, wm_ref[:, c0:c0 + 256], preferred_element_type=_F32) + bm_ref[:, c0:c0 + 256]
        g_ref[:, 256 * g:256 * (g + 1)] = h * (1.0 / (1.0 + jnp.exp(-h)))

    vt = lax.dot_general(wvt_ref[...], xb, _NT, preferred_element_type=_F32) + bvt_ref[...]
    n_b = bvt_out_ref.shape[1]
    bvt_out_ref[0] = vt[0:n_b].astype(bvt_out_ref.dtype)
    avt_out_ref[0] = vt[n_b:].astype(avt_out_ref.dtype)
    iwt = lax.dot_general(wiw_ref[...], xb, _NT, preferred_element_type=_F32) + biw_ref[...]
    iwt_ref[...] = iwt * (IDX_HEADS ** -0.5 * IDX_DIM ** -0.5)


def _projection(x2, pos2, w_in, b_in):
    n, d = x2.shape
    tm = PROJ_ROWS
    wm, wr, wvs, wiw = _split_columns(w_in)
    bm, br, bvs, biw = _split_columns(b_in[None, :])
    wm, wr = wm.astype(_BF16), wr.astype(_BF16)
    zpad = jnp.zeros((HEAD_DIM, d), w_in.dtype)
    one_row = jnp.zeros((HEAD_DIM,), b_in.dtype).at[0].set(1.0)
    wvt = jnp.concatenate([p for wv in wvs for p in (wv.T, zpad, zpad, wv.T)], axis=0).astype(_BF16)
    bvt = jnp.concatenate([p for bv in bvs for p in (bv[0], one_row, one_row, bv[0])])[:, None]
    wiw_t = jnp.concatenate([wiw.T, jnp.zeros((16 - IDX_HEADS, d), w_in.dtype)], axis=0).astype(_BF16)
    biw_t = jnp.concatenate([biw[0], jnp.zeros((16 - IDX_HEADS,), b_in.dtype)])[:, None]
    half = HEAD_DIM // 2
    inv = ROPE_THETA ** (-jnp.arange(0, HEAD_DIM, 2, dtype=_F32) / HEAD_DIM)
    inv128 = jnp.tile(jnp.concatenate([inv, inv]), LANES // HEAD_DIM)[None, :]
    sgn128 = jnp.tile(jnp.concatenate([-jnp.ones((half,), _F32), jnp.ones((half,), _F32)]),
                      LANES // HEAD_DIM)[None, :]

    row = lambda w: pl.BlockSpec((tm, w), lambda i: (i, 0))
    full = lambda a: pl.BlockSpec(a.shape, lambda i: (0,) * a.ndim)
    out_shape = (
        jax.ShapeDtypeStruct((n, 512), _BF16),
        jax.ShapeDtypeStruct((n, 256), _BF16),
        jax.ShapeDtypeStruct((n, 512), _BF16),
        jax.ShapeDtypeStruct((n, 256), _BF16),
        jax.ShapeDtypeStruct((n, 256), _BF16),
        jax.ShapeDtypeStruct((n, 1024), _F32),
        jax.ShapeDtypeStruct((n // tm, 256, tm), _BF16),
        jax.ShapeDtypeStruct((n // tm, 512, tm), _BF16),
        jax.ShapeDtypeStruct((16, n), _F32),
    )
    out_specs = (row(512), row(256), row(512), row(256), row(256), row(1024),
                 pl.BlockSpec((1, 256, tm), lambda i: (i, 0, 0)),
                 pl.BlockSpec((1, 512, tm), lambda i: (i, 0, 0)),
                 pl.BlockSpec((16, tm), lambda i: (0, i)))
    consts = (inv128, sgn128, wm, wr, bm, br, wvt, bvt, wiw_t, biw_t)
    return pl.pallas_call(
        _proj_kernel,
        grid=(n // tm,),
        in_specs=[row(d), row(1)] + [full(a) for a in consts],
        out_specs=out_specs,
        out_shape=out_shape,
        compiler_params=pltpu.CompilerParams(dimension_semantics=("arbitrary",),
                                             vmem_limit_bytes=VMEM_LIMIT),
        name="in_projection",
    )(x2, pos2, *consts)


def _swa_kernel(q_ref, kc_ref, kp_ref, vc_ref, vp_ref, g_ref, sink_ref, out_ref):
    i = pl.program_id(1)
    tq = BLOCK
    n_sub = q_ref.shape[0] // tq
    group = SWA_Q_HEADS // SWA_KV_HEADS
    lane = lax.broadcasted_iota(_I32, (tq, LANES), 1)
    lo_half = lane < HEAD_DIM
    r = lax.broadcasted_iota(_I32, (2 * tq, LANES), 0)
    c = lax.broadcasted_iota(_I32, (2 * tq, LANES), 1)
    in_window = (c < r) & (r <= c + WINDOW)
    row = lax.broadcasted_iota(_I32, (LANES, LANES), 0)

    kwin = jnp.concatenate([kp_ref[...], kc_ref[...]], axis=0)
    vwin = jnp.concatenate([vp_ref[0], vc_ref[0]], axis=1)
    for g in range(SWA_KV_HEADS):
        k = kwin[:, LANES * g:LANES * (g + 1)]
        for j in range(n_sub):
            valid = in_window if j > 0 else in_window & ((r >= tq) | (i > 0))
            rows = slice(tq * j, tq * (j + 1))
            win = slice(tq * j, tq * (j + 2))
            tiles = []
            for odd in (0, 1):
                qs = []
                for s in range(group // 2):
                    slab = q_ref[rows, LANES * (g * (group // 2) + s):LANES * (g * (group // 2) + s + 1)]
                    qs.append(jnp.where(lo_half if odd == 0 else ~lo_half, slab.astype(_F32), 0.0).astype(_BF16))
                qop = jnp.concatenate(qs, axis=0)
                logits = lax.dot_general(k[win], qop, _NT, preferred_element_type=_F32)
                heads = [group * g + 2 * s + odd for s in range(group // 2)]
                ps, sink_terms = [], []
                for s, h in enumerate(heads):
                    lg = jnp.where(valid, logits[:, LANES * s:LANES * (s + 1)], -jnp.inf)
                    sink = sink_ref[h:h + 1, :]
                    m = jnp.maximum(jnp.max(lg, axis=0, keepdims=True), sink)
                    ps.append(jnp.exp2(lg - m).astype(_BF16))
                    sink_terms.append(jnp.exp2(sink - m))
                v = vwin[LANES * (2 * g + odd):LANES * (2 * g + odd + 1), win]
                acc = jnp.dot(v, jnp.concatenate(ps, axis=1), preferred_element_type=_F32)
                l_row = HEAD_DIM if odd == 0 else 0
                denom = acc[l_row:l_row + 1, :] + jnp.concatenate(sink_terms, axis=1)
                tiles.append(acc / denom)
            for s in range(group // 2):
                sl = slice(LANES * s, LANES * (s + 1))
                tile = jnp.where(row < HEAD_DIM, tiles[0][:, sl], tiles[1][:, sl])
                cols = slice(LANES * (g * (group // 2) + s), LANES * (g * (group // 2) + s + 1))
                out_ref[rows, cols] = (tile.T * g_ref[rows, cols]).astype(out_ref.dtype)


def _swa_attention(aq, ak2, avt, gates, sinks, batch, seq):
    tm = PROJ_ROWS
    ns = seq // tm
    n = batch * seq
    per = tm // BLOCK
    sink_b = jnp.broadcast_to((sinks.astype(_F32) * LOG2E)[:, None], (SWA_Q_HEADS, LANES))
    cur = lambda w: pl.BlockSpec((tm, w), lambda b, i: (b * ns + i, 0))
    return pl.pallas_call(
        _swa_kernel,
        grid=(batch, ns),
        in_specs=[cur(512), cur(256),
                  pl.BlockSpec((BLOCK, 256), lambda b, i: (b * ns * per + jnp.maximum(i * per - 1, 0), 0)),
                  pl.BlockSpec((1, 512, tm), lambda b, i: (b * ns + i, 0, 0)),
                  pl.BlockSpec((1, 512, BLOCK), lambda b, i: (b * ns + jnp.maximum(i - 1, 0), 0, per - 1)),
                  cur(512),
                  pl.BlockSpec((SWA_Q_HEADS, LANES), lambda b, i: (0, 0))],
        out_specs=cur(512),
        out_shape=jax.ShapeDtypeStruct((n, 512), _BF16),
        compiler_params=pltpu.CompilerParams(dimension_semantics=("arbitrary", "arbitrary"),
                                             vmem_limit_bytes=VMEM_LIMIT),
        name="swa_attention",
    )(aq, ak2, ak2, avt, avt, gates, sink_b)


_DSA_HEAD_ORDER = (0, 2, 4, 6, 1, 3, 5, 7)


def _key_to_float(k):
    return lax.bitcast_convert_type(jnp.where(k < 0, k ^ 0x7FFFFFFF, k), _F32)


def _key16_to_float(k):
    return lax.bitcast_convert_type(jnp.where(k <= 0, k ^ 0x7FFF ^ jnp.where(k == 0, -1, 0), k) << 16, _F32)


def _midpoint(a, b):
    return (a >> 1) + (b >> 1) + (a & b & 1)


def _tree_sum(parts, chains=8):
    accs = list(parts[:chains])
    for j in range(chains, len(parts)):
        accs[j % chains] = accs[j % chains] + parts[j]
    while len(accs) > 1:
        accs = [accs[2 * j] + accs[2 * j + 1] for j in range(len(accs) // 2)]
    return accs[0]


def _dsa_kernel(iq_ref, bq_ref, iwt_ref, g_ref, bk2_ref, ik2_ref, bvt_ref, out_ref,
                sc_ref, sc16_ref, iqm_ref, bqm_ref, ka_ref, kb_ref, kmax_ref, acc_e_ref, acc_o_ref, *, n_sel):
    tq, kc = BLOCK, KEY_CHUNK
    nh, half = DSA_Q_HEADS, DSA_Q_HEADS // 2
    i = pl.program_id(1)
    q0 = i * tq
    nch = (q0 + tq + kc - 1) // kc
    lane = lax.broadcasted_iota(_I32, (tq, LANES), 1)
    lo_half = lane < HEAD_DIM

    @pl.when(i == 0)
    def _():
        klane = lax.broadcasted_iota(_I32, (kc, LANES), 1)
        klo = klane < HEAD_DIM

        def body(c, mx):
            k0 = pl.multiple_of(c * kc, kc)
            k = bk2_ref[pl.ds(k0, kc), :].astype(_F32)
            ka_ref[pl.ds(k0, kc), :] = jnp.where(klo, k, jnp.where(klane == HEAD_DIM, 1.0, 0.0)).astype(_BF16)
            kb_ref[pl.ds(k0, kc), :] = jnp.where(klo, jnp.where(klane == 0, 1.0, 0.0), k).astype(_BF16)
            return jnp.maximum(mx, jnp.sum(jnp.where(klo, k * k, 0.0), axis=1, keepdims=True))

        mx = lax.fori_loop(0, sc_ref.shape[0] // kc, body, jnp.zeros((kc, 1), _F32))
        kmax_ref[...] = jnp.broadcast_to(jnp.sqrt(jnp.max(mx, axis=0, keepdims=True)), kmax_ref.shape)

    def masked_half(ref, h):
        slab = ref[:, LANES * (h // 2):LANES * (h // 2 + 1)].astype(_F32)
        return jnp.where(lo_half if h % 2 == 0 else ~lo_half, slab, 0.0)

    for h in range(IDX_HEADS):
        iqm_ref[h * tq:(h + 1) * tq, :] = masked_half(iq_ref, h).astype(_BF16)
    kmax = kmax_ref[0:1, 0:1]
    shift_max = jnp.zeros((tq, 1), _F32)
    for p, h in enumerate(_DSA_HEAD_ORDER):
        qm = masked_half(bq_ref, h)
        shift = jnp.sqrt(jnp.sum(qm * qm, axis=1, keepdims=True)) * kmax
        shift_max = jnp.maximum(shift_max, shift)
        one_lane = HEAD_DIM if h % 2 == 0 else 0
        bqm_ref[p * tq:(p + 1) * tq, :] = jnp.where(lane == one_lane, -shift, qm).astype(_BF16)
    bounded = jnp.max(shift_max) <= SHIFT_LIMIT

    w = [iwt_ref[h:h + 1, :] for h in range(IDX_HEADS)]
    qidx = q0 + lax.broadcasted_iota(_I32, (kc, LANES), 1)
    krow = lax.broadcasted_iota(_I32, (kc, LANES), 0)

    def score_chunk(c, diagonal):
        k0 = pl.multiple_of(c * kc, kc)
        z = lax.dot_general(ik2_ref[pl.ds(k0, kc), :], iqm_ref[...], _NT, preferred_element_type=_F32)
        sc = w[0] * jnp.maximum(z[:, 0:LANES], 0.0)
        for h in range(1, IDX_HEADS):
            sc = sc + w[h] * jnp.maximum(z[:, LANES * h:LANES * (h + 1)], 0.0)
        if diagonal:
            sc = jnp.where(k0 + krow <= qidx, sc, -jnp.inf)
        sc_ref[pl.ds(k0, kc), :] = sc
        top = lax.bitcast_convert_type(sc, _I32) & jnp.int32(-65536)
        sc16_ref[pl.ds(k0, kc), :] = lax.bitcast_convert_type(top, _F32).astype(_BF16)

    def score_body(c, carry):
        score_chunk(c, False)
        return carry

    lax.fori_loop(0, nch - 1, score_body, 0)
    score_chunk(nch - 1, True)

    def count_ge(t):
        def body(c, acc):
            k0 = pl.multiple_of(c * kc, kc)
            for h in range(0, kc, 512):
                ind = jnp.where(sc_ref[pl.ds(k0 + h, 512), :] >= t, 1, 0)
                acc = acc + jnp.sum(ind.reshape(64, 8, LANES), axis=0)
            return acc
        acc = lax.fori_loop(0, nch, body, jnp.zeros((8, LANES), _I32))
        return jnp.sum(acc, axis=0, keepdims=True)

    one16, zero16 = jnp.ones((), _BF16), jnp.zeros((), _BF16)

    def count_ge16(t, every=1):
        t = t.astype(_BF16)
        rows = kc // every

        def body(c, acc):
            k0 = pl.multiple_of(c * kc, kc)
            if every == 1:
                x = sc16_ref[pl.ds(k0, kc), :]
            else:
                x = jnp.concatenate([sc16_ref[pl.ds(k0 + 16 * every * j, 16), :] for j in range(rows // 16)], axis=0)
            d = pltpu.bitcast(x - t, _I32)
            neg = lax.shift_right_logical(d, 15) & 0x00010001
            return acc + jnp.sum(neg.reshape(rows // 16, 8, LANES), axis=0)
        acc = lax.fori_loop(0, nch, body, jnp.zeros((8, LANES), _I32))
        below = jnp.sum((acc & 0xFFFF) + lax.shift_right_logical(acc, 16), axis=0, keepdims=True)
        return nch * rows - below

    def search_pass(count, to_thr, st, target=n_sel):
        lo, hi, clo, chi, t, thr = st
        t_up, t_dn = _midpoint(t, hi), _midpoint(lo, t)
        thr_up, thr_dn = to_thr(t_up), to_thr(t_dn)
        cnt = count(thr)
        ok = cnt >= target
        return (jnp.where(ok, t, lo), jnp.where(ok, hi, t), jnp.where(ok, cnt, clo), jnp.where(ok, chi, cnt),
                jnp.where(ok, t_up, t_dn), jnp.where(ok, thr_up, thr_dn))

    vec = lambda v: jnp.full((1, LANES), v, _I32)
    key16_min, key16_end = KEY_NEG_INF >> 16, 1 << 15
    sample = functools.partial(count_ge16, every=COARSE_SAMPLE)
    st = (vec(key16_min), vec(key16_end), vec(0), vec(0), vec(0), _key16_to_float(vec(0)))
    st = lax.fori_loop(0, 16, lambda j, s: search_pass(sample, _key16_to_float, s, n_sel // COARSE_SAMPLE), st)
    w_lo = jnp.maximum(st[0] - COARSE_WINDOW, key16_min)
    w_hi = jnp.minimum(st[0] + COARSE_WINDOW, key16_end)
    c_lo = count_ge16(_key16_to_float(w_lo))
    c_hi = jnp.where(w_hi >= key16_end, 0, count_ge16(_key16_to_float(jnp.minimum(w_hi, key16_end - 1))))
    lo16 = jnp.where(c_lo >= n_sel, w_lo, key16_min)
    hi16 = jnp.where(c_hi < n_sel, w_hi, key16_end)
    t16 = _midpoint(lo16, hi16)
    st = (lo16, hi16, jnp.where(c_lo >= n_sel, c_lo, nch * kc), jnp.where(c_hi < n_sel, c_hi, 0),
          t16, _key16_to_float(t16))
    st = lax.fori_loop(0, COARSE_STEPS, lambda j, s: search_pass(count_ge16, _key16_to_float, s), st)
    hint_lo = jnp.maximum(st[0] << 16, KEY_NEG_INF)
    hint_hi = jnp.where(st[1] >= (1 << 15), INT_MAX, st[1] << 16)
    hint_hi = jnp.where((hint_hi > 0) & (hint_hi < KEY_MIN_NORMAL), KEY_MIN_NORMAL, hint_hi)
    c1 = count_ge(_key_to_float(hint_lo))
    ok1 = c1 >= n_sel
    lo, clo = jnp.where(ok1, hint_lo, KEY_NEG_INF), jnp.where(ok1, c1, nch * kc)
    hi, chi = jnp.where(ok1, INT_MAX, hint_lo), jnp.where(ok1, 0, c1)
    c2 = count_ge(_key_to_float(hint_hi))
    up2 = (c2 >= n_sel) & (hint_hi > lo)
    dn2 = (c2 < n_sel) & (hint_hi < hi)
    lo, clo = jnp.where(up2, hint_hi, lo), jnp.where(up2, c2, clo)
    hi, chi = jnp.where(dn2, hint_hi, hi), jnp.where(dn2, c2, chi)

    def unsettled(lo, hi, clo):
        return ~((clo == n_sel) | (hi - 1 <= lo) | ((lo >= 0) & (hi <= KEY_MIN_NORMAL)))

    def fine_group(c):
        s = c[1:7]
        for _ in range(SEARCH_GROUP):
            s = search_pass(count_ge, _key_to_float, s)
        return (c[0] + 1,) + s + (jnp.max(jnp.where(unsettled(*s[:3]), 1, 0)),)

    t0 = _midpoint(lo, hi)
    init = (jnp.int32(0), lo, hi, clo, chi, t0, _key_to_float(t0),
            jnp.max(jnp.where(unsettled(lo, hi, clo), 1, 0)))
    _, lo, hi, clo, chi, _, _, _ = lax.while_loop(lambda c: (c[7] > 0) & (c[0] < 16), fine_group, init)
    t_lo = _key_to_float(lo)
    t_hi = _key_to_float(hi)

    need = (clo > n_sel) & (lo > KEY_NEG_INF)

    @pl.when(jnp.max(jnp.where(need, 1, 0)) > 0)
    def _():
        tk = TIE_CHUNK
        room = (n_sel - chi).astype(_F32)
        before = (lax.broadcasted_iota(_I32, (tk, tk), 0) > lax.broadcasted_iota(_I32, (tk, tk), 1))
        before = jnp.where(before, 1.0, 0.0).astype(_BF16)

        def body(c, seen):
            for h in range(kc // tk):
                k0 = pl.multiple_of(c * kc + h * tk, tk)
                x = sc_ref[pl.ds(k0, tk), :]
                tie = (x >= t_lo) & ~(x >= t_hi)
                tf = jnp.where(tie, 1.0, 0.0)
                rank = seen + jnp.dot(before, tf.astype(_BF16), preferred_element_type=_F32)
                sc_ref[pl.ds(k0, tk), :] = jnp.where(tie & (rank >= room), -jnp.inf, x)
                seen = seen + jnp.sum(tf, axis=0, keepdims=True)
            return seen

        lax.fori_loop(0, nch, body, jnp.zeros((1, LANES), _F32))

    thr = jnp.maximum(t_lo, F32_LOWEST)

    acc_e_ref[...] = jnp.zeros_like(acc_e_ref)
    acc_o_ref[...] = jnp.zeros_like(acc_o_ref)
    per_vt = kc // bvt_ref.shape[2]

    def masked_logits(c):
        k0 = pl.multiple_of(c * kc, kc)
        sel = sc_ref[pl.ds(k0, kc), :] >= thr
        lge = lax.dot_general(ka_ref[pl.ds(k0, kc), :], bqm_ref[0:half * tq, :], _NT,
                              preferred_element_type=_F32)
        lgo = lax.dot_general(kb_ref[pl.ds(k0, kc), :], bqm_ref[half * tq:nh * tq, :], _NT,
                              preferred_element_type=_F32)
        tiles = lambda lg: [jnp.where(sel, lg[:, LANES * j:LANES * (j + 1)], NEG_BIG) for j in range(half)]
        vt = jnp.concatenate([bvt_ref[c * per_vt + j] for j in range(per_vt)], axis=1)
        return ((acc_e_ref, tiles(lge), vt[0:LANES, :]), (acc_o_ref, tiles(lgo), vt[LANES:2 * LANES, :]))

    @pl.when(bounded)
    def _():
        def body(c, carry):
            for ref, tiles, v in masked_logits(c):
                p = jnp.concatenate([jnp.exp2(t).astype(_BF16) for t in tiles], axis=1)
                ref[...] += jnp.dot(v, p, preferred_element_type=_F32)
            return carry
        lax.fori_loop(0, nch, body, 0)

    @pl.when(jnp.logical_not(bounded))
    def _():
        def body(c, m):
            m_out = []
            for g, (ref, tiles, v) in enumerate(masked_logits(c)):
                ps, alphas = [], []
                for j, t in enumerate(tiles):
                    sl = slice(LANES * (g * half + j), LANES * (g * half + j + 1))
                    m_new = jnp.maximum(m[:, sl], jnp.max(t, axis=0, keepdims=True))
                    ps.append(jnp.exp2(t - m_new).astype(_BF16))
                    alphas.append(jnp.exp2(m[:, sl] - m_new))
                    m_out.append(m_new)
                ref[...] = ref[...] * jnp.concatenate(alphas, axis=1) + jnp.dot(
                    v, jnp.concatenate(ps, axis=1), preferred_element_type=_F32)
            return jnp.concatenate(m_out, axis=1)
        lax.fori_loop(0, nch, body, jnp.full((1, nh * LANES), NEG_BIG, _F32))

    row = lax.broadcasted_iota(_I32, (LANES, LANES), 0)
    for j in range(half):
        sl = slice(LANES * j, LANES * (j + 1))
        even = acc_e_ref[:, sl] / acc_e_ref[HEAD_DIM:HEAD_DIM + 1, sl]
        odd = acc_o_ref[:, sl] / acc_o_ref[0:1, sl]
        tile = jnp.where(row < HEAD_DIM, even, odd)
        out_ref[:, sl] = (tile.T * g_ref[:, sl]).astype(out_ref.dtype)


def _dsa_attention(iq, bq, iwt, gates, ki2, bvt, batch, seq):
    nb = seq // BLOCK
    n = batch * seq
    nvt = seq // PROJ_ROWS
    n_sel = min(TOPK_MAX, seq // 4)
    blk = lambda w, col=0: pl.BlockSpec((BLOCK, w), lambda b, i: (b * nb + i, col))
    return pl.pallas_call(
        functools.partial(_dsa_kernel, n_sel=n_sel),
        grid=(batch, nb),
        in_specs=[blk(256), blk(512),
                  pl.BlockSpec((16, BLOCK), lambda b, i: (0, b * nb + i)),
                  blk(512, 1),
                  pl.BlockSpec((seq, LANES), lambda b, i: (b, 0)),
                  pl.BlockSpec((seq, LANES), lambda b, i: (b, 1)),
                  pl.BlockSpec((nvt, 256, PROJ_ROWS), lambda b, i: (b, 0, 0))],
        out_specs=blk(512),
        out_shape=jax.ShapeDtypeStruct((n, 512), _BF16),
        scratch_shapes=[pltpu.VMEM((seq, LANES), _F32),
                        pltpu.VMEM((seq, LANES), _BF16),
                        pltpu.VMEM((IDX_HEADS * BLOCK, LANES), _BF16),
                        pltpu.VMEM((DSA_Q_HEADS * BLOCK, LANES), _BF16),
                        pltpu.VMEM((seq, LANES), _BF16),
                        pltpu.VMEM((seq, LANES), _BF16),
                        pltpu.VMEM((8, LANES), _F32),
                        pltpu.VMEM((LANES, 512), _F32),
                        pltpu.VMEM((LANES, 512), _F32)],
        compiler_params=pltpu.CompilerParams(dimension_semantics=("arbitrary", "arbitrary"),
                                             vmem_limit_bytes=VMEM_LIMIT),
        name="dsa_attention",
    )(iq, bq, iwt, gates, ki2, ki2, bvt)


def _out_kernel(a_ref, b_ref, x_ref, w_ref, bo_ref, gain_ref, bias_ref, out_ref, *, alpha):
    half = a_ref.shape[1]
    y = jnp.dot(a_ref[...], w_ref[0:half, :], preferred_element_type=_F32)
    y = y + jnp.dot(b_ref[...], w_ref[half:2 * half, :], preferred_element_type=_F32)
    z = alpha * x_ref[...] + (y + bo_ref[...])
    mu = jnp.mean(z, axis=-1, keepdims=True)
    zc = z - mu
    var = jnp.mean(zc * zc, axis=-1, keepdims=True)
    out_ref[...] = zc * lax.rsqrt(var + LN_EPS) * gain_ref[...] + bias_ref[...]


def _out_projection(a, b, x2, w_out, b_out, gain, bias, alpha):
    n, d = x2.shape
    tm = PROJ_ROWS
    row = lambda w: pl.BlockSpec((tm, w), lambda i: (i, 0))
    full = lambda a_: pl.BlockSpec(a_.shape, lambda i: (0,) * a_.ndim)
    consts = (w_out.astype(_BF16), b_out[None, :], gain[None, :], bias[None, :])
    return pl.pallas_call(
        functools.partial(_out_kernel, alpha=alpha),
        grid=(n // tm,),
        in_specs=[row(a.shape[1]), row(b.shape[1]), row(d)] + [full(c) for c in consts],
        out_specs=row(d),
        out_shape=jax.ShapeDtypeStruct((n, d), x2.dtype),
        compiler_params=pltpu.CompilerParams(dimension_semantics=("arbitrary",),
                                             vmem_limit_bytes=VMEM_LIMIT),
        name="out_projection",
    )(a, b, x2, *consts)


def _layer(h, pos2, w_in, b_in, sinks, w_out, b_out, gain, bias, alpha):
    batch, seq, d = h.shape
    x2 = h.reshape(batch * seq, d)
    aq, ak2, bq, iq, ki2, gates, bvt, avt, iwt = _projection(x2, pos2, w_in, b_in)
    a = _swa_attention(aq, ak2, avt, gates, sinks, batch, seq)
    b = _dsa_attention(iq, bq, iwt, gates, ki2, bvt, batch, seq)
    return _out_projection(a, b, x2, w_out, b_out, gain, bias, alpha).reshape(batch, seq, d)


def kernel(x, positions, w_in, b_in, swa_sinks, w_out, b_out, ln_gain, ln_bias):
    batch, seq, d = x.shape
    depth = w_in.shape[0]
    assert d == 1024 and seq % KEY_CHUNK == 0 and (batch * seq) % PROJ_ROWS == 0
    assert KEY_CHUNK % PROJ_ROWS == 0 and KEY_CHUNK % TIE_CHUNK == 0
    alpha = (2.0 * depth) ** 0.25
    pos2 = positions.reshape(batch * seq, 1)
    h = x
    for layer in range(depth):
        h = _layer(h, pos2, w_in[layer], b_in[layer], swa_sinks[layer], w_out[layer], b_out[layer],
                   ln_gain[layer], ln_bias[layer], alpha)
    return h
```

```python
import functools

import numpy as np
import jax
import jax.numpy as jnp
from jax import lax
from jax.experimental import pallas as pl
from jax.experimental.pallas import tpu as pltpu

HEAD_DIM = 64
SWA_Q_HEADS = 8
SWA_KV_HEADS = 2
DSA_Q_HEADS = 8
IDX_HEADS = 4
IDX_DIM = 64
WINDOW = 128
BLOCK = 128
TOPK_MAX = 256
ROPE_THETA = 10000.0
LN_EPS = 1e-5

LANES = 128
PROJ_ROWS = 512
KEY_CHUNK = 1024
TIE_CHUNK = 512
VMEM_LIMIT = 56 * 1024 * 1024

LOG2E = 1.4426950408889634
NEG_BIG = -1e30
SHIFT_LIMIT = 60.0
F32_LOWEST = -3.4028234663852886e38
INT_MAX = 2 ** 31 - 1
KEY_NEG_INF = 0x807FFFFF - 2 ** 32
KEY_MIN_NORMAL = 0x00800000
SEARCH_GROUP = 3

_F32 = jnp.float32
_BF16 = jnp.bfloat16
_I32 = jnp.int32
_NT = (((1,), (1,)), ((), ()))


def _split_columns(m):
    swa_w = SWA_Q_HEADS * HEAD_DIM
    kv_w = SWA_KV_HEADS * HEAD_DIM
    dsa_w = DSA_Q_HEADS * HEAD_DIM
    sizes = (swa_w, kv_w, kv_w, swa_w, dsa_w, HEAD_DIM, HEAD_DIM, dsa_w,
             IDX_HEADS * IDX_DIM, IDX_DIM, IDX_HEADS)
    o = np.cumsum((0,) + sizes)
    aq, ak, av, ag, bq, bk, bv, bg, iq, ik, iw = [m[:, o[i]:o[i + 1]] for i in range(len(sizes))]

    def dup(c):
        return jnp.concatenate([c[:, j:j + HEAD_DIM] for j in range(0, c.shape[1], HEAD_DIM) for _ in (0, 1)],
                               axis=1)

    def rot(c):
        r = c.reshape(c.shape[0], -1, 2, HEAD_DIM // 2)[:, :, ::-1, :]
        return r.reshape(c.shape[0], -1)

    roped = jnp.concatenate([aq, dup(ak), bq, iq, dup(bk), dup(ik)], axis=1)
    return jnp.concatenate([roped, ag, bg], axis=1), rot(roped), (bv, av[:, :HEAD_DIM], av[:, HEAD_DIM:]), iw


def _proj_kernel(x_ref, pos_ref, inv_ref, sgn_ref, wm_ref, wr_ref, bm_ref, br_ref,
                 wvt_ref, bvt_ref, wiw_ref, biw_ref,
                 aq_ref, ak2_ref, bq_ref, iq_ref, ki2_ref, g_ref, bvt_out_ref, avt_out_ref, iwt_ref):
    xb = x_ref[...].astype(_BF16)
    ang = pos_ref[...].astype(_F32) * inv_ref[...]
    cos = jnp.cos(ang)
    sin = jnp.sin(ang) * sgn_ref[...]

    q_scale = 0.125 * LOG2E
    roped_dst = ((aq_ref, 0, q_scale), (aq_ref, 256, q_scale), (ak2_ref, 0, 1.0), (bq_ref, 0, q_scale),
                 (bq_ref, 256, q_scale), (iq_ref, 0, 1.0), (ki2_ref, 0, 1.0))
    for g, (dst, off, scale) in enumerate(roped_dst):
        c0 = 256 * g
        hm = jnp.dot(xb, wm_ref[:, c0:c0 + 256], preferred_element_type=_F32) + bm_ref[:, c0:c0 + 256]
        hr = jnp.dot(xb, wr_ref[:, c0:c0 + 256], preferred_element_type=_F32) + br_ref[:, c0:c0 + 256]
        for s in range(2):
            sl = slice(LANES * s, LANES * (s + 1))
            o = hm[:, sl] * cos + hr[:, sl] * sin
            if scale != 1.0:
                o = o * scale
            dst[:, off + LANES * s:off + LANES * (s + 1)] = o.astype(dst.dtype)

    n_roped = 256 * len(roped_dst)
    for g in range(4):
        c0 = n_roped + 256 * g
        h = jnp.dot(xb, wm_ref[:, c0:c0 + 256], preferred_element_type=_F32) + bm_ref[:, c0:c0 + 256]
        g_ref[:, 256 * g:256 * (g + 1)] = h * (1.0 / (1.0 + jnp.exp(-h)))

    vt = lax.dot_general(wvt_ref[...], xb, _NT, preferred_element_type=_F32) + bvt_ref[...]
    n_b = bvt_out_ref.shape[1]
    bvt_out_ref[0] = vt[0:n_b].astype(bvt_out_ref.dtype)
    avt_out_ref[0] = vt[n_b:].astype(avt_out_ref.dtype)
    iwt = lax.dot_general(wiw_ref[...], xb, _NT, preferred_element_type=_F32) + biw_ref[...]
    iwt_ref[...] = iwt * (IDX_HEADS ** -0.5 * IDX_DIM ** -0.5)


def _projection(x2, pos2, w_in, b_in):
    n, d = x2.shape
    tm = PROJ_ROWS
    wm, wr, wvs, wiw = _split_columns(w_in)
    bm, br, bvs, biw = _split_columns(b_in[None, :])
    wm, wr = wm.astype(_BF16), wr.astype(_BF16)
    zpad = jnp.zeros((HEAD_DIM, d), w_in.dtype)
    one_row = jnp.zeros((HEAD_DIM,), b_in.dtype).at[0].set(1.0)
    wvt = jnp.concatenate([p for wv in wvs for p in (wv.T, zpad, zpad, wv.T)], axis=0).astype(_BF16)
    bvt = jnp.concatenate([p for bv in bvs for p in (bv[0], one_row, one_row, bv[0])])[:, None]
    wiw_t = jnp.concatenate([wiw.T, jnp.zeros((16 - IDX_HEADS, d), w_in.dtype)], axis=0).astype(_BF16)
    biw_t = jnp.concatenate([biw[0], jnp.zeros((16 - IDX_HEADS,), b_in.dtype)])[:, None]
    half = HEAD_DIM // 2
    inv = ROPE_THETA ** (-jnp.arange(0, HEAD_DIM, 2, dtype=_F32) / HEAD_DIM)
    inv128 = jnp.tile(jnp.concatenate([inv, inv]), LANES // HEAD_DIM)[None, :]
    sgn128 = jnp.tile(jnp.concatenate([-jnp.ones((half,), _F32), jnp.ones((half,), _F32)]),
                      LANES // HEAD_DIM)[None, :]

    row = lambda w: pl.BlockSpec((tm, w), lambda i: (i, 0))
    full = lambda a: pl.BlockSpec(a.shape, lambda i: (0,) * a.ndim)
    out_shape = (
        jax.ShapeDtypeStruct((n, 512), _BF16),
        jax.ShapeDtypeStruct((n, 256), _BF16),
        jax.ShapeDtypeStruct((n, 512), _BF16),
        jax.ShapeDtypeStruct((n, 256), _BF16),
        jax.ShapeDtypeStruct((n, 256), _BF16),
        jax.ShapeDtypeStruct((n, 1024), _F32),
        jax.ShapeDtypeStruct((n // tm, 256, tm), _BF16),
        jax.ShapeDtypeStruct((n // tm, 512, tm), _BF16),
        jax.ShapeDtypeStruct((16, n), _F32),
    )
    out_specs = (row(512), row(256), row(512), row(256), row(256), row(1024),
                 pl.BlockSpec((1, 256, tm), lambda i: (i, 0, 0)),
                 pl.BlockSpec((1, 512, tm), lambda i: (i, 0, 0)),
                 pl.BlockSpec((16, tm), lambda i: (0, i)))
    consts = (inv128, sgn128, wm, wr, bm, br, wvt, bvt, wiw_t, biw_t)
    return pl.pallas_call(
        _proj_kernel,
        grid=(n // tm,),
        in_specs=[row(d), row(1)] + [full(a) for a in consts],
        out_specs=out_specs,
        out_shape=out_shape,
        compiler_params=pltpu.CompilerParams(dimension_semantics=("arbitrary",),
                                             vmem_limit_bytes=VMEM_LIMIT),
        name="in_projection",
    )(x2, pos2, *consts)


def _swa_kernel(q_ref, kc_ref, kp_ref, vc_ref, vp_ref, g_ref, sink_ref, out_ref):
    i = pl.program_id(1)
    tq = BLOCK
    n_sub = q_ref.shape[0] // tq
    group = SWA_Q_HEADS // SWA_KV_HEADS
    lane = lax.broadcasted_iota(_I32, (tq, LANES), 1)
    lo_half = lane < HEAD_DIM
    r = lax.broadcasted_iota(_I32, (2 * tq, LANES), 0)
    c = lax.broadcasted_iota(_I32, (2 * tq, LANES), 1)
    in_window = (c < r) & (r <= c + WINDOW)
    row = lax.broadcasted_iota(_I32, (LANES, LANES), 0)

    kwin = jnp.concatenate([kp_ref[...], kc_ref[...]], axis=0)
    vwin = jnp.concatenate([vp_ref[0], vc_ref[0]], axis=1)
    for g in range(SWA_KV_HEADS):
        k = kwin[:, LANES * g:LANES * (g + 1)]
        for j in range(n_sub):
            valid = in_window if j > 0 else in_window & ((r >= tq) | (i > 0))
            rows = slice(tq * j, tq * (j + 1))
            win = slice(tq * j, tq * (j + 2))
            tiles = []
            for odd in (0, 1):
                qs = []
                for s in range(group // 2):
                    slab = q_ref[rows, LANES * (g * (group // 2) + s):LANES * (g * (group // 2) + s + 1)]
                    qs.append(jnp.where(lo_half if odd == 0 else ~lo_half, slab.astype(_F32), 0.0).astype(_BF16))
                qop = jnp.concatenate(qs, axis=0)
                logits = lax.dot_general(k[win], qop, _NT, preferred_element_type=_F32)
                heads = [group * g + 2 * s + odd for s in range(group // 2)]
                ps, sink_terms = [], []
                for s, h in enumerate(heads):
                    lg = jnp.where(valid, logits[:, LANES * s:LANES * (s + 1)], -jnp.inf)
                    sink = sink_ref[h:h + 1, :]
                    m = jnp.maximum(jnp.max(lg, axis=0, keepdims=True), sink)
                    ps.append(jnp.exp2(lg - m).astype(_BF16))
                    sink_terms.append(jnp.exp2(sink - m))
                v = vwin[LANES * (2 * g + odd):LANES * (2 * g + odd + 1), win]
                acc = jnp.dot(v, jnp.concatenate(ps, axis=1), preferred_element_type=_F32)
                l_row = HEAD_DIM if odd == 0 else 0
                denom = acc[l_row:l_row + 1, :] + jnp.concatenate(sink_terms, axis=1)
                tiles.append(acc / denom)
            for s in range(group // 2):
                sl = slice(LANES * s, LANES * (s + 1))
                tile = jnp.where(row < HEAD_DIM, tiles[0][:, sl], tiles[1][:, sl])
                cols = slice(LANES * (g * (group // 2) + s), LANES * (g * (group // 2) + s + 1))
                out_ref[rows, cols] = (tile.T * g_ref[rows, cols]).astype(out_ref.dtype)


def _swa_attention(aq, ak2, avt, gates, sinks, batch, seq):
    tm = PROJ_ROWS
    ns = seq // tm
    n = batch * seq
    per = tm // BLOCK
    sink_b = jnp.broadcast_to((sinks.astype(_F32) * LOG2E)[:, None], (SWA_Q_HEADS, LANES))
    cur = lambda w: pl.BlockSpec((tm, w), lambda b, i: (b * ns + i, 0))
    return pl.pallas_call(
        _swa_kernel,
        grid=(batch, ns),
        in_specs=[cur(512), cur(256),
                  pl.BlockSpec((BLOCK, 256), lambda b, i: (b * ns * per + jnp.maximum(i * per - 1, 0), 0)),
                  pl.BlockSpec((1, 512, tm), lambda b, i: (b * ns + i, 0, 0)),
                  pl.BlockSpec((1, 512, BLOCK), lambda b, i: (b * ns + jnp.maximum(i - 1, 0), 0, per - 1)),
                  cur(512),
                  pl.BlockSpec((SWA_Q_HEADS, LANES), lambda b, i: (0, 0))],
        out_specs=cur(512),
        out_shape=jax.ShapeDtypeStruct((n, 512), _BF16),
        compiler_params=pltpu.CompilerParams(dimension_semantics=("arbitrary", "arbitrary"),
                                             vmem_limit_bytes=VMEM_LIMIT),
        name="swa_attention",
    )(aq, ak2, ak2, avt, avt, gates, sink_b)


_DSA_HEAD_ORDER = (0, 2, 4, 6, 1, 3, 5, 7)


def _key_to_float(k):
    return lax.bitcast_convert_type(jnp.where(k < 0, k ^ 0x7FFFFFFF, k), _F32)


def _key16_to_float(k):
    return lax.bitcast_convert_type(jnp.where(k <= 0, k ^ 0x7FFF ^ jnp.where(k == 0, -1, 0), k) << 16, _F32)


def _midpoint(a, b):
    return (a >> 1) + (b >> 1) + (a & b & 1)


def _tree_sum(parts, chains=8):
    accs = list(parts[:chains])
    for j in range(chains, len(parts)):
        accs[j % chains] = accs[j % chains] + parts[j]
    while len(accs) > 1:
        accs = [accs[2 * j] + accs[2 * j + 1] for j in range(len(accs) // 2)]
    return accs[0]


def _dsa_kernel(iq_ref, bq_ref, iwt_ref, g_ref, bk2_ref, ik2_ref, bvt_ref, out_ref,
                sc_ref, sc16_ref, iqm_ref, bqm_ref, ka_ref, kb_ref, kmax_ref, acc_e_ref, acc_o_ref, *, n_sel):
    tq, kc = BLOCK, KEY_CHUNK
    nh, half = DSA_Q_HEADS, DSA_Q_HEADS // 2
    i = pl.program_id(1)
    q0 = i * tq
    lane = lax.broadcasted_iota(_I32, (tq, LANES), 1)
    lo_half = lane < HEAD_DIM

    kh = kc // 2
    rem = (q0 + tq) % kc
    n_whole = (q0 + tq) // kc + jnp.where(rem > kh, 1, 0)
    has_half = (rem > 0) & (rem <= kh)
    k_half = n_whole * kc
    n_rows = k_half + jnp.where(has_half, kh, 0)

    def over_keys(body, carry):
        carry = lax.fori_loop(0, n_whole, lambda c, cr: body(pl.multiple_of(c * kc, kc), kc, cr), carry)
        return lax.cond(has_half, lambda cr: body(pl.multiple_of(k_half, kh), kh, cr), lambda cr: cr, carry)

    @pl.when(i == 0)
    def _():
        klane = lax.broadcasted_iota(_I32, (kc, LANES), 1)
        klo = klane < HEAD_DIM

        def body(c, mx):
            k0 = pl.multiple_of(c * kc, kc)
            k = bk2_ref[pl.ds(k0, kc), :].astype(_F32)
            ka_ref[pl.ds(k0, kc), :] = jnp.where(klo, k, jnp.where(klane == HEAD_DIM, 1.0, 0.0)).astype(_BF16)
            kb_ref[pl.ds(k0, kc), :] = jnp.where(klo, jnp.where(klane == 0, 1.0, 0.0), k).astype(_BF16)
            return jnp.maximum(mx, jnp.sum(jnp.where(klo, k * k, 0.0), axis=1, keepdims=True))

        mx = lax.fori_loop(0, sc_ref.shape[0] // kc, body, jnp.zeros((kc, 1), _F32))
        kmax_ref[...] = jnp.broadcast_to(jnp.sqrt(jnp.max(mx, axis=0, keepdims=True)), kmax_ref.shape)

    def masked_half(ref, h):
        slab = ref[:, LANES * (h // 2):LANES * (h // 2 + 1)].astype(_F32)
        return jnp.where(lo_half if h % 2 == 0 else ~lo_half, slab, 0.0)

    for h in range(IDX_HEADS):
        iqm_ref[h * tq:(h + 1) * tq, :] = masked_half(iq_ref, h).astype(_BF16)
    kmax = kmax_ref[0:1, 0:1]
    shift_max = jnp.zeros((tq, 1), _F32)
    for p, h in enumerate(_DSA_HEAD_ORDER):
        qm = masked_half(bq_ref, h)
        shift = jnp.sqrt(jnp.sum(qm * qm, axis=1, keepdims=True)) * kmax
        shift_max = jnp.maximum(shift_max, shift)
        one_lane = HEAD_DIM if h % 2 == 0 else 0
        bqm_ref[p * tq:(p + 1) * tq, :] = jnp.where(lane == one_lane, -shift, qm).astype(_BF16)
    bounded = jnp.max(shift_max) <= SHIFT_LIMIT

    w = [iwt_ref[h:h + 1, :] for h in range(IDX_HEADS)]
    def score_piece(k0, rows, diagonal):
        z = lax.dot_general(ik2_ref[pl.ds(k0, rows), :], iqm_ref[...], _NT, preferred_element_type=_F32)
        sc = w[0] * jnp.maximum(z[:, 0:LANES], 0.0)
        for h in range(1, IDX_HEADS):
            sc = sc + w[h] * jnp.maximum(z[:, LANES * h:LANES * (h + 1)], 0.0)
        if diagonal:
            kidx = k0 + lax.broadcasted_iota(_I32, (rows, LANES), 0)
            qidx = q0 + lax.broadcasted_iota(_I32, (rows, LANES), 1)
            sc = jnp.where(kidx <= qidx, sc, -jnp.inf)
        sc_ref[pl.ds(k0, rows), :] = sc
        top = lax.bitcast_convert_type(sc, _I32) & jnp.int32(-65536)
        sc16_ref[pl.ds(k0, rows), :] = lax.bitcast_convert_type(top, _F32).astype(_BF16)

    def score_body(c, carry):
        score_piece(pl.multiple_of(c * kc, kc), kc, False)
        return carry

    lax.fori_loop(0, n_whole - jnp.where(has_half, 0, 1), score_body, 0)

    @pl.when(has_half)
    def _():
        score_piece(pl.multiple_of(k_half, kh), kh, True)

    @pl.when(jnp.logical_not(has_half))
    def _():
        score_piece(pl.multiple_of((n_whole - 1) * kc, kc), kc, True)

    def count_ge(t):
        def body(k0, rows, acc):
            for h in range(0, rows, 512):
                ind = jnp.where(sc_ref[pl.ds(k0 + h, 512), :] >= t, 1, 0)
                acc = acc + jnp.sum(ind.reshape(64, 8, LANES), axis=0)
            return acc
        return jnp.sum(over_keys(body, jnp.zeros((8, LANES), _I32)), axis=0, keepdims=True)

    def count_ge16(t):
        t = t.astype(_BF16)

        def body(k0, rows, acc):
            d = pltpu.bitcast(sc16_ref[pl.ds(k0, rows), :] - t, _I32)
            neg = lax.shift_right_logical(d, 15) & 0x00010001
            return acc + jnp.sum(neg.reshape(rows // 16, 8, LANES), axis=0)
        acc = over_keys(body, jnp.zeros((8, LANES), _I32))
        below = jnp.sum((acc & 0xFFFF) + lax.shift_right_logical(acc, 16), axis=0, keepdims=True)
        return n_rows - below

    def search_pass(count, to_thr, st):
        lo, hi, clo, chi, t, thr = st
        t_up, t_dn = _midpoint(t, hi), _midpoint(lo, t)
        thr_up, thr_dn = to_thr(t_up), to_thr(t_dn)
        cnt = count(thr)
        ok = cnt >= n_sel
        return (jnp.where(ok, t, lo), jnp.where(ok, hi, t), jnp.where(ok, cnt, clo), jnp.where(ok, chi, cnt),
                jnp.where(ok, t_up, t_dn), jnp.where(ok, thr_up, thr_dn))

    vec = lambda v: jnp.full((1, LANES), v, _I32)
    st = (vec(KEY_NEG_INF >> 16), vec(1 << 15), vec(0) + n_rows, vec(0), vec(0), _key16_to_float(vec(0)))
    st = lax.fori_loop(0, 16, lambda j, s: search_pass(count_ge16, _key16_to_float, s), st)
    hint_lo = jnp.maximum(st[0] << 16, KEY_NEG_INF)
    hint_hi = jnp.where(st[1] >= (1 << 15), INT_MAX, st[1] << 16)
    hint_hi = jnp.where((hint_hi > 0) & (hint_hi < KEY_MIN_NORMAL), KEY_MIN_NORMAL, hint_hi)
    c1 = count_ge(_key_to_float(hint_lo))
    ok1 = c1 >= n_sel
    lo, clo = jnp.where(ok1, hint_lo, KEY_NEG_INF), jnp.where(ok1, c1, n_rows)
    hi, chi = jnp.where(ok1, INT_MAX, hint_lo), jnp.where(ok1, 0, c1)
    c2 = count_ge(_key_to_float(hint_hi))
    up2 = (c2 >= n_sel) & (hint_hi > lo)
    dn2 = (c2 < n_sel) & (hint_hi < hi)
    lo, clo = jnp.where(up2, hint_hi, lo), jnp.where(up2, c2, clo)
    hi, chi = jnp.where(dn2, hint_hi, hi), jnp.where(dn2, c2, chi)

    def unsettled(lo, hi, clo):
        return ~((clo == n_sel) | (hi - 1 <= lo) | ((lo >= 0) & (hi <= KEY_MIN_NORMAL)))

    def fine_group(c):
        s = c[1:7]
        for _ in range(SEARCH_GROUP):
            s = search_pass(count_ge, _key_to_float, s)
        return (c[0] + 1,) + s + (jnp.max(jnp.where(unsettled(*s[:3]), 1, 0)),)

    t0 = _midpoint(lo, hi)
    init = (jnp.int32(0), lo, hi, clo, chi, t0, _key_to_float(t0),
            jnp.max(jnp.where(unsettled(lo, hi, clo), 1, 0)))
    _, lo, hi, clo, chi, _, _, _ = lax.while_loop(lambda c: (c[7] > 0) & (c[0] < 16), fine_group, init)
    t_lo = _key_to_float(lo)
    t_hi = _key_to_float(hi)

    need = (clo > n_sel) & (lo > KEY_NEG_INF)

    @pl.when(jnp.max(jnp.where(need, 1, 0)) > 0)
    def _():
        tk = TIE_CHUNK
        room = (n_sel - chi).astype(_F32)
        before = (lax.broadcasted_iota(_I32, (tk, tk), 0) > lax.broadcasted_iota(_I32, (tk, tk), 1))
        before = jnp.where(before, 1.0, 0.0).astype(_BF16)

        def body(k0, rows, seen):
            for h in range(rows // tk):
                x = sc_ref[pl.ds(k0 + h * tk, tk), :]
                tie = (x >= t_lo) & ~(x >= t_hi)
                tf = jnp.where(tie, 1.0, 0.0)
                rank = seen + jnp.dot(before, tf.astype(_BF16), preferred_element_type=_F32)
                sc_ref[pl.ds(k0 + h * tk, tk), :] = jnp.where(tie & (rank >= room), -jnp.inf, x)
                seen = seen + jnp.sum(tf, axis=0, keepdims=True)
            return seen

        over_keys(body, jnp.zeros((1, LANES), _F32))

    thr = jnp.maximum(t_lo, F32_LOWEST)

    acc_e_ref[...] = jnp.zeros_like(acc_e_ref)
    acc_o_ref[...] = jnp.zeros_like(acc_o_ref)
    vt_rows = bvt_ref.shape[2]

    def masked_logits(k0, rows):
        sel = sc_ref[pl.ds(k0, rows), :] >= thr
        lge = lax.dot_general(ka_ref[pl.ds(k0, rows), :], bqm_ref[0:half * tq, :], _NT,
                              preferred_element_type=_F32)
        lgo = lax.dot_general(kb_ref[pl.ds(k0, rows), :], bqm_ref[half * tq:nh * tq, :], _NT,
                              preferred_element_type=_F32)
        tiles = lambda lg: [jnp.where(sel, lg[:, LANES * j:LANES * (j + 1)], NEG_BIG) for j in range(half)]
        vt = jnp.concatenate([bvt_ref[k0 // vt_rows + j] for j in range(rows // vt_rows)], axis=1)
        return ((acc_e_ref, tiles(lge), vt[0:LANES, :]), (acc_o_ref, tiles(lgo), vt[LANES:2 * LANES, :]))

    @pl.when(bounded)
    def _():
        def body(k0, rows, carry):
            for ref, tiles, v in masked_logits(k0, rows):
                p = jnp.concatenate([jnp.exp2(t).astype(_BF16) for t in tiles], axis=1)
                ref[...] += jnp.dot(v, p, preferred_element_type=_F32)
            return carry
        over_keys(body, jnp.int32(0))

    @pl.when(jnp.logical_not(bounded))
    def _():
        def body(k0, rows, m):
            m_out = []
            for g, (ref, tiles, v) in enumerate(masked_logits(k0, rows)):
                ps, alphas = [], []
                for j, t in enumerate(tiles):
                    sl = slice(LANES * (g * half + j), LANES * (g * half + j + 1))
                    m_new = jnp.maximum(m[:, sl], jnp.max(t, axis=0, keepdims=True))
                    ps.append(jnp.exp2(t - m_new).astype(_BF16))
                    alphas.append(jnp.exp2(m[:, sl] - m_new))
                    m_out.append(m_new)
                ref[...] = ref[...] * jnp.concatenate(alphas, axis=1) + jnp.dot(
                    v, jnp.concatenate(ps, axis=1), preferred_element_type=_F32)
            return jnp.concatenate(m_out, axis=1)
        over_keys(body, jnp.full((1, nh * LANES), NEG_BIG, _F32))

    row = lax.broadcasted_iota(_I32, (LANES, LANES), 0)
    for j in range(half):
        sl = slice(LANES * j, LANES * (j + 1))
        even = acc_e_ref[:, sl] / acc_e_ref[HEAD_DIM:HEAD_DIM + 1, sl]
        odd = acc_o_ref[:, sl] / acc_o_ref[0:1, sl]
        tile = jnp.where(row < HEAD_DIM, even, odd)
        out_ref[:, sl] = (tile.T * g_ref[:, sl]).astype(out_ref.dtype)


def _dsa_attention(iq, bq, iwt, gates, ki2, bvt, batch, seq):
    nb = seq // BLOCK
    n = batch * seq
    nvt = seq // PROJ_ROWS
    n_sel = min(TOPK_MAX, seq // 4)
    blk = lambda w, col=0: pl.BlockSpec((BLOCK, w), lambda b, i: (b * nb + i, col))
    return pl.pallas_call(
        functools.partial(_dsa_kernel, n_sel=n_sel),
        grid=(batch, nb),
        in_specs=[blk(256), blk(512),
                  pl.BlockSpec((16, BLOCK), lambda b, i: (0, b * nb + i)),
                  blk(512, 1),
                  pl.BlockSpec((seq, LANES), lambda b, i: (b, 0)),
                  pl.BlockSpec((seq, LANES), lambda b, i: (b, 1)),
                  pl.BlockSpec((nvt, 256, PROJ_ROWS), lambda b, i: (b, 0, 0))],
        out_specs=blk(512),
        out_shape=jax.ShapeDtypeStruct((n, 512), _BF16),
        scratch_shapes=[pltpu.VMEM((seq, LANES), _F32),
                        pltpu.VMEM((seq, LANES), _BF16),
                        pltpu.VMEM((IDX_HEADS * BLOCK, LANES), _BF16),
                        pltpu.VMEM((DSA_Q_HEADS * BLOCK, LANES), _BF16),
                        pltpu.VMEM((seq, LANES), _BF16),
                        pltpu.VMEM((seq, LANES), _BF16),
                        pltpu.VMEM((8, LANES), _F32),
                        pltpu.VMEM((LANES, 512), _F32),
                        pltpu.VMEM((LANES, 512), _F32)],
        compiler_params=pltpu.CompilerParams(dimension_semantics=("arbitrary", "arbitrary"),
                                             vmem_limit_bytes=VMEM_LIMIT),
        name="dsa_attention",
    )(iq, bq, iwt, gates, ki2, ki2, bvt)


def _out_kernel(a_ref, b_ref, x_ref, w_ref, bo_ref, gain_ref, bias_ref, out_ref, *, alpha):
    half = a_ref.shape[1]
    y = jnp.dot(a_ref[...], w_ref[0:half, :], preferred_element_type=_F32)
    y = y + jnp.dot(b_ref[...], w_ref[half:2 * half, :], preferred_element_type=_F32)
    z = alpha * x_ref[...] + (y + bo_ref[...])
    mu = jnp.mean(z, axis=-1, keepdims=True)
    zc = z - mu
    var = jnp.mean(zc * zc, axis=-1, keepdims=True)
    out_ref[...] = zc * lax.rsqrt(var + LN_EPS) * gain_ref[...] + bias_ref[...]


def _out_projection(a, b, x2, w_out, b_out, gain, bias, alpha):
    n, d = x2.shape
    tm = PROJ_ROWS
    row = lambda w: pl.BlockSpec((tm, w), lambda i: (i, 0))
    full = lambda a_: pl.BlockSpec(a_.shape, lambda i: (0,) * a_.ndim)
    consts = (w_out.astype(_BF16), b_out[None, :], gain[None, :], bias[None, :])
    return pl.pallas_call(
        functools.partial(_out_kernel, alpha=alpha),
        grid=(n // tm,),
        in_specs=[row(a.shape[1]), row(b.shape[1]), row(d)] + [full(c) for c in consts],
        out_specs=row(d),
        out_shape=jax.ShapeDtypeStruct((n, d), x2.dtype),
        compiler_params=pltpu.CompilerParams(dimension_semantics=("arbitrary",),
                                             vmem_limit_bytes=VMEM_LIMIT),
        name="out_projection",
    )(a, b, x2, *consts)


def _layer(h, pos2, w_in, b_in, sinks, w_out, b_out, gain, bias, alpha):
    batch, seq, d = h.shape
    x2 = h.reshape(batch * seq, d)
    aq, ak2, bq, iq, ki2, gates, bvt, avt, iwt = _projection(x2, pos2, w_in, b_in)
    a = _swa_attention(aq, ak2, avt, gates, sinks, batch, seq)
    b = _dsa_attention(iq, bq, iwt, gates, ki2, bvt, batch, seq)
    return _out_projection(a, b, x2, w_out, b_out, gain, bias, alpha).reshape(batch, seq, d)


def kernel(x, positions, w_in, b_in, swa_sinks, w_out, b_out, ln_gain, ln_bias):
    batch, seq, d = x.shape
    depth = w_in.shape[0]
    assert d == 1024 and seq % KEY_CHUNK == 0 and (batch * seq) % PROJ_ROWS == 0
    assert KEY_CHUNK % PROJ_ROWS == 0 and KEY_CHUNK % TIE_CHUNK == 0
    alpha = (2.0 * depth) ** 0.25
    pos2 = positions.reshape(batch * seq, 1)
    h = x
    for layer in range(depth):
        h = _layer(h, pos2, w_in[layer], b_in[layer], swa_sinks[layer], w_out[layer], b_out[layer],
                   ln_gain[layer], ln_bias[layer], alpha)
    return h
```

```python
import functools

import numpy as np
import jax
import jax.numpy as jnp
from jax import lax
from jax.experimental import pallas as pl
from jax.experimental.pallas import tpu as pltpu

HEAD_DIM = 64
SWA_Q_HEADS = 8
SWA_KV_HEADS = 2
DSA_Q_HEADS = 8
IDX_HEADS = 4
IDX_DIM = 64
WINDOW = 128
BLOCK = 128
TOPK_MAX = 256
ROPE_THETA = 10000.0
LN_EPS = 1e-5

LANES = 128
PROJ_ROWS = 512
KEY_CHUNK = 1024
TIE_CHUNK = 512
VMEM_LIMIT = 56 * 1024 * 1024

LOG2E = 1.4426950408889634
NEG_BIG = -1e30
SHIFT_LIMIT = 60.0
F32_LOWEST = -3.4028234663852886e38
INT_MAX = 2 ** 31 - 1
KEY_NEG_INF = 0x807FFFFF - 2 ** 32
KEY_MIN_NORMAL = 0x00800000
SEARCH_GROUP = 3

_F32 = jnp.float32
_BF16 = jnp.bfloat16
_I32 = jnp.int32
_NT = (((1,), (1,)), ((), ()))


def _split_columns(m):
    swa_w = SWA_Q_HEADS * HEAD_DIM
    kv_w = SWA_KV_HEADS * HEAD_DIM
    dsa_w = DSA_Q_HEADS * HEAD_DIM
    sizes = (swa_w, kv_w, kv_w, swa_w, dsa_w, HEAD_DIM, HEAD_DIM, dsa_w,
             IDX_HEADS * IDX_DIM, IDX_DIM, IDX_HEADS)
    o = np.cumsum((0,) + sizes)
    aq, ak, av, ag, bq, bk, bv, bg, iq, ik, iw = [m[:, o[i]:o[i + 1]] for i in range(len(sizes))]

    def dup(c):
        return jnp.concatenate([c[:, j:j + HEAD_DIM] for j in range(0, c.shape[1], HEAD_DIM) for _ in (0, 1)],
                               axis=1)

    def pairs(c):
        r = c.reshape(c.shape[0], -1, 2, 2, HEAD_DIM // 2)
        return jnp.swapaxes(r, 2, 3).reshape(c.shape[0], -1)

    roped = pairs(jnp.concatenate([aq, dup(ak), bq, iq, dup(bk), dup(ik)], axis=1))
    return jnp.concatenate([roped, ag, bg], axis=1), (bv, av[:, :HEAD_DIM], av[:, HEAD_DIM:]), iw


def _proj_kernel(x_ref, pos_ref, inv_ref, sgn_ref, wm_ref, bm_ref,
                 wvt_ref, bvt_ref, wiw_ref, biw_ref,
                 aq_ref, ak2_ref, bq_ref, iq_ref, ki2_ref, g_ref, bvt_out_ref, avt_out_ref, iwt_ref):
    xb = x_ref[...].astype(_BF16)
    ang = pos_ref[...].astype(_F32) * inv_ref[...]
    cos = jnp.cos(ang)
    sin = jnp.sin(ang) * sgn_ref[...]

    q_scale = 0.125 * LOG2E
    roped_dst = ((aq_ref, 0, q_scale), (aq_ref, 256, q_scale), (ak2_ref, 0, 1.0), (bq_ref, 0, q_scale),
                 (bq_ref, 256, q_scale), (iq_ref, 0, 1.0), (ki2_ref, 0, 1.0))
    for g, (dst, off, scale) in enumerate(roped_dst):
        c0 = 256 * g
        hm = jnp.dot(xb, wm_ref[:, c0:c0 + 256], preferred_element_type=_F32) + bm_ref[:, c0:c0 + 256]
        for s in range(2):
            sl = slice(LANES * s, LANES * (s + 1))
            o = hm[:, sl] * cos + pltpu.roll(hm[:, sl], LANES // 2, axis=1) * sin
            if scale != 1.0:
                o = o * scale
            dst[:, off + LANES * s:off + LANES * (s + 1)] = o.astype(dst.dtype)

    n_roped = 256 * len(roped_dst)
    for g in range(4):
        c0 = n_roped + 256 * g
        h = jnp.dot(xb, wm_ref[:, c0:c0 + 256], preferred_element_type=_F32) + bm_ref[:, c0:c0 + 256]
        g_ref[:, 256 * g:256 * (g + 1)] = h * (1.0 / (1.0 + jnp.exp(-h)))

    vt = lax.dot_general(wvt_ref[...], xb, _NT, preferred_element_type=_F32) + bvt_ref[...]
    n_b = bvt_out_ref.shape[1]
    bvt_out_ref[0] = vt[0:n_b].astype(bvt_out_ref.dtype)
    avt_out_ref[0] = vt[n_b:].astype(avt_out_ref.dtype)
    iwt = lax.dot_general(wiw_ref[...], xb, _NT, preferred_element_type=_F32) + biw_ref[...]
    iwt_ref[...] = iwt * (IDX_HEADS ** -0.5 * IDX_DIM ** -0.5)


def _projection(x2, pos2, w_in, b_in):
    n, d = x2.shape
    tm = PROJ_ROWS
    wm, wvs, wiw = _split_columns(w_in)
    bm, bvs, biw = _split_columns(b_in[None, :])
    wm = wm.astype(_BF16)
    zpad = jnp.zeros((HEAD_DIM, d), w_in.dtype)
    one_row = jnp.zeros((HEAD_DIM,), b_in.dtype).at[0].set(1.0)
    wvt = jnp.concatenate([p for wv in wvs for p in (wv.T, zpad, zpad, wv.T)], axis=0).astype(_BF16)
    bvt = jnp.concatenate([p for bv in bvs for p in (bv[0], one_row, one_row, bv[0])])[:, None]
    wiw_t = jnp.concatenate([wiw.T, jnp.zeros((16 - IDX_HEADS, d), w_in.dtype)], axis=0).astype(_BF16)
    biw_t = jnp.concatenate([biw[0], jnp.zeros((16 - IDX_HEADS,), b_in.dtype)])[:, None]
    half = HEAD_DIM // 2
    inv = ROPE_THETA ** (-jnp.arange(0, HEAD_DIM, 2, dtype=_F32) / HEAD_DIM)
    inv128 = jnp.tile(inv, LANES // half)[None, :]
    sgn128 = jnp.concatenate([-jnp.ones((LANES // 2,), _F32), jnp.ones((LANES // 2,), _F32)])[None, :]

    row = lambda w: pl.BlockSpec((tm, w), lambda i: (i, 0))
    full = lambda a: pl.BlockSpec(a.shape, lambda i: (0,) * a.ndim)
    out_shape = (
        jax.ShapeDtypeStruct((n, 512), _BF16),
        jax.ShapeDtypeStruct((n, 256), _BF16),
        jax.ShapeDtypeStruct((n, 512), _BF16),
        jax.ShapeDtypeStruct((n, 256), _BF16),
        jax.ShapeDtypeStruct((n, 256), _BF16),
        jax.ShapeDtypeStruct((n, 1024), _F32),
        jax.ShapeDtypeStruct((n // tm, 256, tm), _BF16),
        jax.ShapeDtypeStruct((n // tm, 512, tm), _BF16),
        jax.ShapeDtypeStruct((16, n), _F32),
    )
    out_specs = (row(512), row(256), row(512), row(256), row(256), row(1024),
                 pl.BlockSpec((1, 256, tm), lambda i: (i, 0, 0)),
                 pl.BlockSpec((1, 512, tm), lambda i: (i, 0, 0)),
                 pl.BlockSpec((16, tm), lambda i: (0, i)))
    consts = (inv128, sgn128, wm, bm, wvt, bvt, wiw_t, biw_t)
    return pl.pallas_call(
        _proj_kernel,
        grid=(n // tm,),
        in_specs=[row(d), row(1)] + [full(a) for a in consts],
        out_specs=out_specs,
        out_shape=out_shape,
        compiler_params=pltpu.CompilerParams(dimension_semantics=("arbitrary",),
                                             vmem_limit_bytes=VMEM_LIMIT),
        name="in_projection",
    )(x2, pos2, *consts)


def _swa_kernel(q_ref, kc_ref, kp_ref, vc_ref, vp_ref, g_ref, sink_ref, out_ref):
    i = pl.program_id(1)
    tq = BLOCK
    n_sub = q_ref.shape[0] // tq
    group = SWA_Q_HEADS // SWA_KV_HEADS
    lane = lax.broadcasted_iota(_I32, (tq, LANES), 1)
    lo_half = (lane & (HEAD_DIM // 2)) == 0
    r = lax.broadcasted_iota(_I32, (2 * tq, LANES), 0)
    c = lax.broadcasted_iota(_I32, (2 * tq, LANES), 1)
    in_window = (c < r) & (r <= c + WINDOW)
    row = lax.broadcasted_iota(_I32, (LANES, LANES), 0)

    kwin = jnp.concatenate([kp_ref[...], kc_ref[...]], axis=0)
    vwin = jnp.concatenate([vp_ref[0], vc_ref[0]], axis=1)
    for g in range(SWA_KV_HEADS):
        k = kwin[:, LANES * g:LANES * (g + 1)]
        for j in range(n_sub):
            valid = in_window if j > 0 else in_window & ((r >= tq) | (i > 0))
            rows = slice(tq * j, tq * (j + 1))
            win = slice(tq * j, tq * (j + 2))
            tiles = []
            for odd in (0, 1):
                qs = []
                for s in range(group // 2):
                    slab = q_ref[rows, LANES * (g * (group // 2) + s):LANES * (g * (group // 2) + s + 1)]
                    qs.append(jnp.where(lo_half if odd == 0 else ~lo_half, slab.astype(_F32), 0.0).astype(_BF16))
                qop = jnp.concatenate(qs, axis=0)
                logits = lax.dot_general(k[win], qop, _NT, preferred_element_type=_F32)
                heads = [group * g + 2 * s + odd for s in range(group // 2)]
                ps, sink_terms = [], []
                for s, h in enumerate(heads):
                    lg = jnp.where(valid, logits[:, LANES * s:LANES * (s + 1)], -jnp.inf)
                    sink = sink_ref[h:h + 1, :]
                    m = jnp.maximum(jnp.max(lg, axis=0, keepdims=True), sink)
                    ps.append(jnp.exp2(lg - m).astype(_BF16))
                    sink_terms.append(jnp.exp2(sink - m))
                v = vwin[LANES * (2 * g + odd):LANES * (2 * g + odd + 1), win]
                acc = jnp.dot(v, jnp.concatenate(ps, axis=1), preferred_element_type=_F32)
                l_row = HEAD_DIM if odd == 0 else 0
                denom = acc[l_row:l_row + 1, :] + jnp.concatenate(sink_terms, axis=1)
                tiles.append(acc / denom)
            for s in range(group // 2):
                sl = slice(LANES * s, LANES * (s + 1))
                tile = jnp.where(row < HEAD_DIM, tiles[0][:, sl], tiles[1][:, sl])
                cols = slice(LANES * (g * (group // 2) + s), LANES * (g * (group // 2) + s + 1))
                out_ref[rows, cols] = (tile.T * g_ref[rows, cols]).astype(out_ref.dtype)


def _swa_attention(aq, ak2, avt, gates, sinks, batch, seq):
    tm = PROJ_ROWS
    ns = seq // tm
    n = batch * seq
    per = tm // BLOCK
    sink_b = jnp.broadcast_to((sinks.astype(_F32) * LOG2E)[:, None], (SWA_Q_HEADS, LANES))
    cur = lambda w: pl.BlockSpec((tm, w), lambda b, i: (b * ns + i, 0))
    return pl.pallas_call(
        _swa_kernel,
        grid=(batch, ns),
        in_specs=[cur(512), cur(256),
                  pl.BlockSpec((BLOCK, 256), lambda b, i: (b * ns * per + jnp.maximum(i * per - 1, 0), 0)),
                  pl.BlockSpec((1, 512, tm), lambda b, i: (b * ns + i, 0, 0)),
                  pl.BlockSpec((1, 512, BLOCK), lambda b, i: (b * ns + jnp.maximum(i - 1, 0), 0, per - 1)),
                  cur(512),
                  pl.BlockSpec((SWA_Q_HEADS, LANES), lambda b, i: (0, 0))],
        out_specs=cur(512),
        out_shape=jax.ShapeDtypeStruct((n, 512), _BF16),
        compiler_params=pltpu.CompilerParams(dimension_semantics=("arbitrary", "arbitrary"),
                                             vmem_limit_bytes=VMEM_LIMIT),
        name="swa_attention",
    )(aq, ak2, ak2, avt, avt, gates, sink_b)


_DSA_HEAD_ORDER = (0, 2, 4, 6, 1, 3, 5, 7)


def _key_to_float(k):
    return lax.bitcast_convert_type(jnp.where(k < 0, k ^ 0x7FFFFFFF, k), _F32)


def _key16_to_float(k):
    return lax.bitcast_convert_type(jnp.where(k <= 0, k ^ 0x7FFF ^ jnp.where(k == 0, -1, 0), k) << 16, _F32)


def _midpoint(a, b):
    return (a >> 1) + (b >> 1) + (a & b & 1)


def _tree_sum(parts, chains=8):
    accs = list(parts[:chains])
    for j in range(chains, len(parts)):
        accs[j % chains] = accs[j % chains] + parts[j]
    while len(accs) > 1:
        accs = [accs[2 * j] + accs[2 * j + 1] for j in range(len(accs) // 2)]
    return accs[0]


def _dsa_kernel(iq_ref, bq_ref, iwt_ref, g_ref, bk2_ref, ik2_ref, bvt_ref, out_ref,
                sc_ref, sc16_ref, iqm_ref, bqm_ref, ka_ref, kb_ref, kmax_ref, acc_e_ref, acc_o_ref, *, n_sel):
    tq, kc = BLOCK, KEY_CHUNK
    nh, half = DSA_Q_HEADS, DSA_Q_HEADS // 2
    i = pl.program_id(1)
    q0 = i * tq
    lane = lax.broadcasted_iota(_I32, (tq, LANES), 1)
    lo_half = (lane & (HEAD_DIM // 2)) == 0
    one_even, one_odd = HEAD_DIM // 2, 0

    kh = kc // 2
    rem = (q0 + tq) % kc
    n_whole = (q0 + tq) // kc + jnp.where(rem > kh, 1, 0)
    has_half = (rem > 0) & (rem <= kh)
    k_half = n_whole * kc
    n_rows = k_half + jnp.where(has_half, kh, 0)

    def over_keys(body, carry):
        carry = lax.fori_loop(0, n_whole, lambda c, cr: body(pl.multiple_of(c * kc, kc), kc, cr), carry)
        return lax.cond(has_half, lambda cr: body(pl.multiple_of(k_half, kh), kh, cr), lambda cr: cr, carry)

    @pl.when(i == 0)
    def _():
        klane = lax.broadcasted_iota(_I32, (kc, LANES), 1)
        klo = (klane & (HEAD_DIM // 2)) == 0

        def body(c, mx):
            k0 = pl.multiple_of(c * kc, kc)
            k = bk2_ref[pl.ds(k0, kc), :].astype(_F32)
            ka_ref[pl.ds(k0, kc), :] = jnp.where(klane == one_even, 1.0, k).astype(_BF16)
            kb_ref[pl.ds(k0, kc), :] = jnp.where(klane == one_odd, 1.0, k).astype(_BF16)
            return jnp.maximum(mx, jnp.sum(jnp.where(klo, k * k, 0.0), axis=1, keepdims=True))

        mx = lax.fori_loop(0, sc_ref.shape[0] // kc, body, jnp.zeros((kc, 1), _F32))
        kmax_ref[...] = jnp.broadcast_to(jnp.sqrt(jnp.max(mx, axis=0, keepdims=True)), kmax_ref.shape)

    def masked_half(ref, h):
        slab = ref[:, LANES * (h // 2):LANES * (h // 2 + 1)].astype(_F32)
        return jnp.where(lo_half if h % 2 == 0 else ~lo_half, slab, 0.0)

    for h in range(IDX_HEADS):
        iqm_ref[h * tq:(h + 1) * tq, :] = masked_half(iq_ref, h).astype(_BF16)
    kmax = kmax_ref[0:1, 0:1]
    shift_max = jnp.zeros((tq, 1), _F32)
    for p, h in enumerate(_DSA_HEAD_ORDER):
        qm = masked_half(bq_ref, h)
        shift = jnp.sqrt(jnp.sum(qm * qm, axis=1, keepdims=True)) * kmax
        shift_max = jnp.maximum(shift_max, shift)
        one_lane = one_even if h % 2 == 0 else one_odd
        bqm_ref[p * tq:(p + 1) * tq, :] = jnp.where(lane == one_lane, -shift, qm).astype(_BF16)
    bounded = jnp.max(shift_max) <= SHIFT_LIMIT

    w = [iwt_ref[h:h + 1, :] for h in range(IDX_HEADS)]
    def score_piece(k0, rows, diagonal):
        z = lax.dot_general(ik2_ref[pl.ds(k0, rows), :], iqm_ref[...], _NT, preferred_element_type=_F32)
        sc = w[0] * jnp.maximum(z[:, 0:LANES], 0.0)
        for h in range(1, IDX_HEADS):
            sc = sc + w[h] * jnp.maximum(z[:, LANES * h:LANES * (h + 1)], 0.0)
        if diagonal:
            kidx = k0 + lax.broadcasted_iota(_I32, (rows, LANES), 0)
            qidx = q0 + lax.broadcasted_iota(_I32, (rows, LANES), 1)
            sc = jnp.where(kidx <= qidx, sc, -jnp.inf)
        sc_ref[pl.ds(k0, rows), :] = sc
        top = lax.bitcast_convert_type(sc, _I32) & jnp.int32(-65536)
        sc16_ref[pl.ds(k0, rows), :] = lax.bitcast_convert_type(top, _F32).astype(_BF16)

    def score_body(c, carry):
        score_piece(pl.multiple_of(c * kc, kc), kc, False)
        return carry

    lax.fori_loop(0, n_whole - jnp.where(has_half, 0, 1), score_body, 0)

    @pl.when(has_half)
    def _():
        score_piece(pl.multiple_of(k_half, kh), kh, True)

    @pl.when(jnp.logical_not(has_half))
    def _():
        score_piece(pl.multiple_of((n_whole - 1) * kc, kc), kc, True)

    def count_ge(t):
        def body(k0, rows, acc):
            for h in range(0, rows, 512):
                ind = jnp.where(sc_ref[pl.ds(k0 + h, 512), :] >= t, 1, 0)
                acc = acc + jnp.sum(ind.reshape(64, 8, LANES), axis=0)
            return acc
        return jnp.sum(over_keys(body, jnp.zeros((8, LANES), _I32)), axis=0, keepdims=True)

    def count_ge16(t):
        t = t.astype(_BF16)

        def body(k0, rows, acc):
            d = pltpu.bitcast(sc16_ref[pl.ds(k0, rows), :] - t, _I32)
            neg = lax.shift_right_logical(d, 15) & 0x00010001
            return acc + jnp.sum(neg.reshape(rows // 16, 8, LANES), axis=0)
        acc = over_keys(body, jnp.zeros((8, LANES), _I32))
        below = jnp.sum((acc & 0xFFFF) + lax.shift_right_logical(acc, 16), axis=0, keepdims=True)
        return n_rows - below

    def search_pass(count, to_thr, st):
        lo, hi, clo, chi, t, thr = st
        t_up, t_dn = _midpoint(t, hi), _midpoint(lo, t)
        thr_up, thr_dn = to_thr(t_up), to_thr(t_dn)
        cnt = count(thr)
        ok = cnt >= n_sel
        return (jnp.where(ok, t, lo), jnp.where(ok, hi, t), jnp.where(ok, cnt, clo), jnp.where(ok, chi, cnt),
                jnp.where(ok, t_up, t_dn), jnp.where(ok, thr_up, thr_dn))

    vec = lambda v: jnp.full((1, LANES), v, _I32)
    st = (vec(KEY_NEG_INF >> 16), vec(1 << 15), vec(0) + n_rows, vec(0), vec(0), _key16_to_float(vec(0)))
    st = lax.fori_loop(0, 16, lambda j, s: search_pass(count_ge16, _key16_to_float, s), st)
    hint_lo = jnp.maximum(st[0] << 16, KEY_NEG_INF)
    hint_hi = jnp.where(st[1] >= (1 << 15), INT_MAX, st[1] << 16)
    hint_hi = jnp.where((hint_hi > 0) & (hint_hi < KEY_MIN_NORMAL), KEY_MIN_NORMAL, hint_hi)
    c1 = count_ge(_key_to_float(hint_lo))
    ok1 = c1 >= n_sel
    lo, clo = jnp.where(ok1, hint_lo, KEY_NEG_INF), jnp.where(ok1, c1, n_rows)
    hi, chi = jnp.where(ok1, INT_MAX, hint_lo), jnp.where(ok1, 0, c1)
    c2 = count_ge(_key_to_float(hint_hi))
    up2 = (c2 >= n_sel) & (hint_hi > lo)
    dn2 = (c2 < n_sel) & (hint_hi < hi)
    lo, clo = jnp.where(up2, hint_hi, lo), jnp.where(up2, c2, clo)
    hi, chi = jnp.where(dn2, hint_hi, hi), jnp.where(dn2, c2, chi)

    def unsettled(lo, hi, clo):
        return ~((clo == n_sel) | (hi - 1 <= lo) | ((lo >= 0) & (hi <= KEY_MIN_NORMAL)))

    def fine_group(c):
        s = c[1:7]
        for _ in range(SEARCH_GROUP):
            s = search_pass(count_ge, _key_to_float, s)
        return (c[0] + 1,) + s + (jnp.max(jnp.where(unsettled(*s[:3]), 1, 0)),)

    t0 = _midpoint(lo, hi)
    init = (jnp.int32(0), lo, hi, clo, chi, t0, _key_to_float(t0),
            jnp.max(jnp.where(unsettled(lo, hi, clo), 1, 0)))
    _, lo, hi, clo, chi, _, _, _ = lax.while_loop(lambda c: (c[7] > 0) & (c[0] < 16), fine_group, init)
    t_lo = _key_to_float(lo)
    t_hi = _key_to_float(hi)

    need = (clo > n_sel) & (lo > KEY_NEG_INF)

    @pl.when(jnp.max(jnp.where(need, 1, 0)) > 0)
    def _():
        tk = TIE_CHUNK
        room = (n_sel - chi).astype(_F32)
        before = (lax.broadcasted_iota(_I32, (tk, tk), 0) > lax.broadcasted_iota(_I32, (tk, tk), 1))
        before = jnp.where(before, 1.0, 0.0).astype(_BF16)

        def body(k0, rows, seen):
            for h in range(rows // tk):
                x = sc_ref[pl.ds(k0 + h * tk, tk), :]
                tie = (x >= t_lo) & ~(x >= t_hi)
                tf = jnp.where(tie, 1.0, 0.0)
                rank = seen + jnp.dot(before, tf.astype(_BF16), preferred_element_type=_F32)
                sc_ref[pl.ds(k0 + h * tk, tk), :] = jnp.where(tie & (rank >= room), -jnp.inf, x)
                seen = seen + jnp.sum(tf, axis=0, keepdims=True)
            return seen

        over_keys(body, jnp.zeros((1, LANES), _F32))

    thr = jnp.maximum(t_lo, F32_LOWEST)

    acc_e_ref[...] = jnp.zeros_like(acc_e_ref)
    acc_o_ref[...] = jnp.zeros_like(acc_o_ref)
    vt_rows = bvt_ref.shape[2]

    def masked_logits(k0, rows):
        sel = sc_ref[pl.ds(k0, rows), :] >= thr
        lge = lax.dot_general(ka_ref[pl.ds(k0, rows), :], bqm_ref[0:half * tq, :], _NT,
                              preferred_element_type=_F32)
        lgo = lax.dot_general(kb_ref[pl.ds(k0, rows), :], bqm_ref[half * tq:nh * tq, :], _NT,
                              preferred_element_type=_F32)
        tiles = lambda lg: [jnp.where(sel, lg[:, LANES * j:LANES * (j + 1)], NEG_BIG) for j in range(half)]
        vt = jnp.concatenate([bvt_ref[k0 // vt_rows + j] for j in range(rows // vt_rows)], axis=1)
        return ((acc_e_ref, tiles(lge), vt[0:LANES, :]), (acc_o_ref, tiles(lgo), vt[LANES:2 * LANES, :]))

    @pl.when(bounded)
    def _():
        def body(k0, rows, carry):
            for ref, tiles, v in masked_logits(k0, rows):
                p = jnp.concatenate([jnp.exp2(t).astype(_BF16) for t in tiles], axis=1)
                ref[...] += jnp.dot(v, p, preferred_element_type=_F32)
            return carry
        over_keys(body, jnp.int32(0))

    @pl.when(jnp.logical_not(bounded))
    def _():
        def body(k0, rows, m):
            m_out = []
            for g, (ref, tiles, v) in enumerate(masked_logits(k0, rows)):
                ps, alphas = [], []
                for j, t in enumerate(tiles):
                    sl = slice(LANES * (g * half + j), LANES * (g * half + j + 1))
                    m_new = jnp.maximum(m[:, sl], jnp.max(t, axis=0, keepdims=True))
                    ps.append(jnp.exp2(t - m_new).astype(_BF16))
                    alphas.append(jnp.exp2(m[:, sl] - m_new))
                    m_out.append(m_new)
                ref[...] = ref[...] * jnp.concatenate(alphas, axis=1) + jnp.dot(
                    v, jnp.concatenate(ps, axis=1), preferred_element_type=_F32)
            return jnp.concatenate(m_out, axis=1)
        over_keys(body, jnp.full((1, nh * LANES), NEG_BIG, _F32))

    row = lax.broadcasted_iota(_I32, (LANES, LANES), 0)
    for j in range(half):
        sl = slice(LANES * j, LANES * (j + 1))
        even = acc_e_ref[:, sl] / acc_e_ref[HEAD_DIM:HEAD_DIM + 1, sl]
        odd = acc_o_ref[:, sl] / acc_o_ref[0:1, sl]
        tile = jnp.where(row < HEAD_DIM, even, odd)
        out_ref[:, sl] = (tile.T * g_ref[:, sl]).astype(out_ref.dtype)


def _dsa_attention(iq, bq, iwt, gates, ki2, bvt, batch, seq):
    nb = seq // BLOCK
    n = batch * seq
    nvt = seq // PROJ_ROWS
    n_sel = min(TOPK_MAX, seq // 4)
    blk = lambda w, col=0: pl.BlockSpec((BLOCK, w), lambda b, i: (b * nb + i, col))
    return pl.pallas_call(
        functools.partial(_dsa_kernel, n_sel=n_sel),
        grid=(batch, nb),
        in_specs=[blk(256), blk(512),
                  pl.BlockSpec((16, BLOCK), lambda b, i: (0, b * nb + i)),
                  blk(512, 1),
                  pl.BlockSpec((seq, LANES), lambda b, i: (b, 0)),
                  pl.BlockSpec((seq, LANES), lambda b, i: (b, 1)),
                  pl.BlockSpec((nvt, 256, PROJ_ROWS), lambda b, i: (b, 0, 0))],
        out_specs=blk(512),
        out_shape=jax.ShapeDtypeStruct((n, 512), _BF16),
        scratch_shapes=[pltpu.VMEM((seq, LANES), _F32),
                        pltpu.VMEM((seq, LANES), _BF16),
                        pltpu.VMEM((IDX_HEADS * BLOCK, LANES), _BF16),
                        pltpu.VMEM((DSA_Q_HEADS * BLOCK, LANES), _BF16),
                        pltpu.VMEM((seq, LANES), _BF16),
                        pltpu.VMEM((seq, LANES), _BF16),
                        pltpu.VMEM((8, LANES), _F32),
                        pltpu.VMEM((LANES, 512), _F32),
                        pltpu.VMEM((LANES, 512), _F32)],
        compiler_params=pltpu.CompilerParams(dimension_semantics=("arbitrary", "arbitrary"),
                                             vmem_limit_bytes=VMEM_LIMIT),
        name="dsa_attention",
    )(iq, bq, iwt, gates, ki2, ki2, bvt)


def _out_kernel(a_ref, b_ref, x_ref, w_ref, bo_ref, gain_ref, bias_ref, out_ref, *, alpha):
    half = a_ref.shape[1]
    y = jnp.dot(a_ref[...], w_ref[0:half, :], preferred_element_type=_F32)
    y = y + jnp.dot(b_ref[...], w_ref[half:2 * half, :], preferred_element_type=_F32)
    z = alpha * x_ref[...] + (y + bo_ref[...])
    mu = jnp.mean(z, axis=-1, keepdims=True)
    zc = z - mu
    var = jnp.mean(zc * zc, axis=-1, keepdims=True)
    out_ref[...] = zc * lax.rsqrt(var + LN_EPS) * gain_ref[...] + bias_ref[...]


def _out_projection(a, b, x2, w_out, b_out, gain, bias, alpha):
    n, d = x2.shape
    tm = PROJ_ROWS
    row = lambda w: pl.BlockSpec((tm, w), lambda i: (i, 0))
    full = lambda a_: pl.BlockSpec(a_.shape, lambda i: (0,) * a_.ndim)
    consts = (w_out.astype(_BF16), b_out[None, :], gain[None, :], bias[None, :])
    return pl.pallas_call(
        functools.partial(_out_kernel, alpha=alpha),
        grid=(n // tm,),
        in_specs=[row(a.shape[1]), row(b.shape[1]), row(d)] + [full(c) for c in consts],
        out_specs=row(d),
        out_shape=jax.ShapeDtypeStruct((n, d), x2.dtype),
        compiler_params=pltpu.CompilerParams(dimension_semantics=("arbitrary",),
                                             vmem_limit_bytes=VMEM_LIMIT),
        name="out_projection",
    )(a, b, x2, *consts)


def _layer(h, pos2, w_in, b_in, sinks, w_out, b_out, gain, bias, alpha):
    batch, seq, d = h.shape
    x2 = h.reshape(batch * seq, d)
    aq, ak2, bq, iq, ki2, gates, bvt, avt, iwt = _projection(x2, pos2, w_in, b_in)
    a = _swa_attention(aq, ak2, avt, gates, sinks, batch, seq)
    b = _dsa_attention(iq, bq, iwt, gates, ki2, bvt, batch, seq)
    return _out_projection(a, b, x2, w_out, b_out, gain, bias, alpha).reshape(batch, seq, d)


def kernel(x, positions, w_in, b_in, swa_sinks, w_out, b_out, ln_gain, ln_bias):
    batch, seq, d = x.shape
    depth = w_in.shape[0]
    assert d == 1024 and seq % KEY_CHUNK == 0 and (batch * seq) % PROJ_ROWS == 0
    assert KEY_CHUNK % PROJ_ROWS == 0 and KEY_CHUNK % TIE_CHUNK == 0
    alpha = (2.0 * depth) ** 0.25
    pos2 = positions.reshape(batch * seq, 1)
    h = x
    for layer in range(depth):
        h = _layer(h, pos2, w_in[layer], b_in[layer], swa_sinks[layer], w_out[layer], b_out[layer],
                   ln_gain[layer], ln_bias[layer], alpha)
    return h
```

```python
import functools

import numpy as np
import jax
import jax.numpy as jnp
from jax import lax
from jax.experimental import pallas as pl
from jax.experimental.pallas import tpu as pltpu

HEAD_DIM = 64
SWA_Q_HEADS = 8
SWA_KV_HEADS = 2
DSA_Q_HEADS = 8
IDX_HEADS = 4
IDX_DIM = 64
WINDOW = 128
BLOCK = 128
TOPK_MAX = 256
ROPE_THETA = 10000.0
LN_EPS = 1e-5

LANES = 128
PROJ_ROWS = 512
KEY_CHUNK = 1024
TIE_CHUNK = 512
VMEM_LIMIT = 56 * 1024 * 1024

LOG2E = 1.4426950408889634
NEG_BIG = -1e30
SHIFT_LIMIT = 60.0
F32_LOWEST = -3.4028234663852886e38
INT_MAX = 2 ** 31 - 1
KEY_NEG_INF = 0x807FFFFF - 2 ** 32
KEY_MIN_NORMAL = 0x00800000
SEARCH_GROUP = 3

_F32 = jnp.float32
_BF16 = jnp.bfloat16
_I32 = jnp.int32
_NT = (((1,), (1,)), ((), ()))


def _split_columns(m):
    swa_w = SWA_Q_HEADS * HEAD_DIM
    kv_w = SWA_KV_HEADS * HEAD_DIM
    dsa_w = DSA_Q_HEADS * HEAD_DIM
    sizes = (swa_w, kv_w, kv_w, swa_w, dsa_w, HEAD_DIM, HEAD_DIM, dsa_w,
             IDX_HEADS * IDX_DIM, IDX_DIM, IDX_HEADS)
    o = np.cumsum((0,) + sizes)
    aq, ak, av, ag, bq, bk, bv, bg, iq, ik, iw = [m[:, o[i]:o[i + 1]] for i in range(len(sizes))]

    def dup(c):
        return jnp.concatenate([c[:, j:j + HEAD_DIM] for j in range(0, c.shape[1], HEAD_DIM) for _ in (0, 1)],
                               axis=1)

    def pairs(c):
        r = c.reshape(c.shape[0], -1, 2, 2, HEAD_DIM // 2)
        return jnp.swapaxes(r, 2, 3).reshape(c.shape[0], -1)

    roped = pairs(jnp.concatenate([aq, dup(ak), bq, iq, dup(bk), dup(ik)], axis=1))
    return jnp.concatenate([roped, ag, bg], axis=1), (bv, av[:, :HEAD_DIM], av[:, HEAD_DIM:]), iw


def _proj_kernel(x_ref, pos_ref, inv_ref, sgn_ref, wm_ref, bm_ref,
                 wvt_ref, bvt_ref, wiw_ref, biw_ref,
                 aq_ref, ak2_ref, bq_ref, iq_ref, ki2_ref, g_ref, bvt_out_ref, avt_out_ref, iwt_ref):
    xb = x_ref[...].astype(_BF16)
    ang = pos_ref[...].astype(_F32) * inv_ref[...]
    cos = jnp.cos(ang)
    sin = jnp.sin(ang) * sgn_ref[...]

    q_scale = 0.125 * LOG2E
    roped_dst = ((aq_ref, 0, q_scale), (aq_ref, 256, q_scale), (ak2_ref, 0, 1.0), (bq_ref, 0, q_scale),
                 (bq_ref, 256, q_scale), (iq_ref, 0, 1.0), (ki2_ref, 0, 1.0))
    for g, (dst, off, scale) in enumerate(roped_dst):
        c0 = 256 * g
        hm = jnp.dot(xb, wm_ref[:, c0:c0 + 256], preferred_element_type=_F32) + bm_ref[:, c0:c0 + 256]
        for s in range(2):
            sl = slice(LANES * s, LANES * (s + 1))
            o = hm[:, sl] * cos + pltpu.roll(hm[:, sl], LANES // 2, axis=1) * sin
            if scale != 1.0:
                o = o * scale
            dst[:, off + LANES * s:off + LANES * (s + 1)] = o.astype(dst.dtype)

    n_roped = 256 * len(roped_dst)
    for g in range(4):
        c0 = n_roped + 256 * g
        h = jnp.dot(xb, wm_ref[:, c0:c0 + 256], preferred_element_type=_F32) + bm_ref[:, c0:c0 + 256]
        g_ref[:, 256 * g:256 * (g + 1)] = h * (1.0 / (1.0 + jnp.exp(-h)))

    vt = lax.dot_general(wvt_ref[...], xb, _NT, preferred_element_type=_F32) + bvt_ref[...]
    n_b = bvt_out_ref.shape[1]
    bvt_out_ref[0] = vt[0:n_b].astype(bvt_out_ref.dtype)
    avt_out_ref[0] = vt[n_b:].astype(avt_out_ref.dtype)
    iwt = lax.dot_general(wiw_ref[...], xb, _NT, preferred_element_type=_F32) + biw_ref[...]
    iwt_ref[...] = iwt * (IDX_HEADS ** -0.5 * IDX_DIM ** -0.5)


def _projection(x2, pos2, w_in, b_in):
    n, d = x2.shape
    tm = PROJ_ROWS
    wm, wvs, wiw = _split_columns(w_in)
    bm, bvs, biw = _split_columns(b_in[None, :])
    wm = wm.astype(_BF16)
    zpad = jnp.zeros((HEAD_DIM, d), w_in.dtype)
    one_row = jnp.zeros((HEAD_DIM,), b_in.dtype).at[0].set(1.0)
    wvt = jnp.concatenate([p for wv in wvs for p in (wv.T, zpad, zpad, wv.T)], axis=0).astype(_BF16)
    bvt = jnp.concatenate([p for bv in bvs for p in (bv[0], one_row, one_row, bv[0])])[:, None]
    wiw_t = jnp.concatenate([wiw.T, jnp.zeros((16 - IDX_HEADS, d), w_in.dtype)], axis=0).astype(_BF16)
    biw_t = jnp.concatenate([biw[0], jnp.zeros((16 - IDX_HEADS,), b_in.dtype)])[:, None]
    half = HEAD_DIM // 2
    inv = ROPE_THETA ** (-jnp.arange(0, HEAD_DIM, 2, dtype=_F32) / HEAD_DIM)
    inv128 = jnp.tile(inv, LANES // half)[None, :]
    sgn128 = jnp.concatenate([-jnp.ones((LANES // 2,), _F32), jnp.ones((LANES // 2,), _F32)])[None, :]

    row = lambda w: pl.BlockSpec((tm, w), lambda i: (i, 0))
    full = lambda a: pl.BlockSpec(a.shape, lambda i: (0,) * a.ndim)
    out_shape = (
        jax.ShapeDtypeStruct((n, 512), _BF16),
        jax.ShapeDtypeStruct((n, 256), _BF16),
        jax.ShapeDtypeStruct((n, 512), _BF16),
        jax.ShapeDtypeStruct((n, 256), _BF16),
        jax.ShapeDtypeStruct((n, 256), _BF16),
        jax.ShapeDtypeStruct((n, 1024), _F32),
        jax.ShapeDtypeStruct((n // tm, 256, tm), _BF16),
        jax.ShapeDtypeStruct((n // tm, 512, tm), _BF16),
        jax.ShapeDtypeStruct((16, n), _F32),
    )
    out_specs = (row(512), row(256), row(512), row(256), row(256), row(1024),
                 pl.BlockSpec((1, 256, tm), lambda i: (i, 0, 0)),
                 pl.BlockSpec((1, 512, tm), lambda i: (i, 0, 0)),
                 pl.BlockSpec((16, tm), lambda i: (0, i)))
    consts = (inv128, sgn128, wm, bm, wvt, bvt, wiw_t, biw_t)
    return pl.pallas_call(
        _proj_kernel,
        grid=(n // tm,),
        in_specs=[row(d), row(1)] + [full(a) for a in consts],
        out_specs=out_specs,
        out_shape=out_shape,
        compiler_params=pltpu.CompilerParams(dimension_semantics=("arbitrary",),
                                             vmem_limit_bytes=VMEM_LIMIT),
        name="in_projection",
    )(x2, pos2, *consts)


def _swa_kernel(q_ref, kc_ref, kp_ref, vc_ref, vp_ref, g_ref, sink_ref, out_ref):
    i = pl.program_id(1)
    tq = BLOCK
    n_sub = q_ref.shape[0] // tq
    group = SWA_Q_HEADS // SWA_KV_HEADS
    lane = lax.broadcasted_iota(_I32, (tq, LANES), 1)
    lo_half = (lane & (HEAD_DIM // 2)) == 0
    r = lax.broadcasted_iota(_I32, (2 * tq, LANES), 0)
    c = lax.broadcasted_iota(_I32, (2 * tq, LANES), 1)
    in_window = (c < r) & (r <= c + WINDOW)
    row = lax.broadcasted_iota(_I32, (LANES, LANES), 0)

    kwin = jnp.concatenate([kp_ref[...], kc_ref[...]], axis=0)
    vwin = jnp.concatenate([vp_ref[0], vc_ref[0]], axis=1)
    for g in range(SWA_KV_HEADS):
        k = kwin[:, LANES * g:LANES * (g + 1)]
        for j in range(n_sub):
            valid = in_window if j > 0 else in_window & ((r >= tq) | (i > 0))
            rows = slice(tq * j, tq * (j + 1))
            win = slice(tq * j, tq * (j + 2))
            tiles = []
            for odd in (0, 1):
                qs = []
                for s in range(group // 2):
                    slab = q_ref[rows, LANES * (g * (group // 2) + s):LANES * (g * (group // 2) + s + 1)]
                    qs.append(jnp.where(lo_half if odd == 0 else ~lo_half, slab.astype(_F32), 0.0).astype(_BF16))
                qop = jnp.concatenate(qs, axis=0)
                logits = lax.dot_general(k[win], qop, _NT, preferred_element_type=_F32)
                heads = [group * g + 2 * s + odd for s in range(group // 2)]
                ps, sink_terms = [], []
                for s, h in enumerate(heads):
                    lg = jnp.where(valid, logits[:, LANES * s:LANES * (s + 1)], -jnp.inf)
                    sink = sink_ref[h:h + 1, :]
                    m = jnp.maximum(jnp.max(lg, axis=0, keepdims=True), sink)
                    ps.append(jnp.exp2(lg - m).astype(_BF16))
                    sink_terms.append(jnp.exp2(sink - m))
                v = vwin[LANES * (2 * g + odd):LANES * (2 * g + odd + 1), win]
                acc = jnp.dot(v, jnp.concatenate(ps, axis=1), preferred_element_type=_F32)
                l_row = HEAD_DIM if odd == 0 else 0
                denom = acc[l_row:l_row + 1, :] + jnp.concatenate(sink_terms, axis=1)
                tiles.append(acc / denom)
            for s in range(group // 2):
                sl = slice(LANES * s, LANES * (s + 1))
                tile = jnp.where(row < HEAD_DIM, tiles[0][:, sl], tiles[1][:, sl])
                cols = slice(LANES * (g * (group // 2) + s), LANES * (g * (group // 2) + s + 1))
                out_ref[rows, cols] = (tile.T * g_ref[rows, cols]).astype(out_ref.dtype)


def _swa_attention(aq, ak2, avt, gates, sinks, batch, seq):
    tm = PROJ_ROWS
    ns = seq // tm
    n = batch * seq
    per = tm // BLOCK
    sink_b = jnp.broadcast_to((sinks.astype(_F32) * LOG2E)[:, None], (SWA_Q_HEADS, LANES))
    cur = lambda w: pl.BlockSpec((tm, w), lambda b, i: (b * ns + i, 0))
    return pl.pallas_call(
        _swa_kernel,
        grid=(batch, ns),
        in_specs=[cur(512), cur(256),
                  pl.BlockSpec((BLOCK, 256), lambda b, i: (b * ns * per + jnp.maximum(i * per - 1, 0), 0)),
                  pl.BlockSpec((1, 512, tm), lambda b, i: (b * ns + i, 0, 0)),
                  pl.BlockSpec((1, 512, BLOCK), lambda b, i: (b * ns + jnp.maximum(i - 1, 0), 0, per - 1)),
                  cur(512),
                  pl.BlockSpec((SWA_Q_HEADS, LANES), lambda b, i: (0, 0))],
        out_specs=cur(512),
        out_shape=jax.ShapeDtypeStruct((n, 512), _BF16),
        compiler_params=pltpu.CompilerParams(dimension_semantics=("arbitrary", "arbitrary"),
                                             vmem_limit_bytes=VMEM_LIMIT),
        name="swa_attention",
    )(aq, ak2, ak2, avt, avt, gates, sink_b)


_DSA_HEAD_ORDER = (0, 2, 4, 6, 1, 3, 5, 7)


def _key_to_float(k):
    return lax.bitcast_convert_type(jnp.where(k < 0, k ^ 0x7FFFFFFF, k), _F32)


def _key16_to_float(k):
    return lax.bitcast_convert_type(jnp.where(k <= 0, k ^ 0x7FFF ^ jnp.where(k == 0, -1, 0), k) << 16, _F32)


def _midpoint(a, b):
    return (a >> 1) + (b >> 1) + (a & b & 1)


def _tree_sum(parts, chains=8):
    accs = list(parts[:chains])
    for j in range(chains, len(parts)):
        accs[j % chains] = accs[j % chains] + parts[j]
    while len(accs) > 1:
        accs = [accs[2 * j] + accs[2 * j + 1] for j in range(len(accs) // 2)]
    return accs[0]


def _dsa_kernel(iq_ref, bq_ref, iwt_ref, g_ref, bk2_ref, ik2_ref, bvt_ref, out_ref,
                sc_ref, sc16_ref, iqm_ref, bqm_ref, ka_ref, kb_ref, kmax_ref, acc_e_ref, acc_o_ref, *, n_sel):
    tq, kc = BLOCK, KEY_CHUNK
    nh, half = DSA_Q_HEADS, DSA_Q_HEADS // 2
    i = pl.program_id(1)
    q0 = i * tq
    lane = lax.broadcasted_iota(_I32, (tq, LANES), 1)
    lo_half = (lane & (HEAD_DIM // 2)) == 0
    one_even, one_odd = HEAD_DIM // 2, 0

    kh = kc // 2
    rem = (q0 + tq) % kc
    n_whole = (q0 + tq) // kc + jnp.where(rem > kh, 1, 0)
    has_half = (rem > 0) & (rem <= kh)
    k_half = n_whole * kc
    n_rows = k_half + jnp.where(has_half, kh, 0)

    def over_keys(body, carry):
        def two(c, cr):
            k0 = pl.multiple_of(c * 2 * kc, 2 * kc)
            return body(k0 + kc, kc, body(k0, kc, cr))

        carry = lax.fori_loop(0, n_whole // 2, two, carry)
        carry = lax.cond(n_whole % 2 == 1, lambda cr: body(pl.multiple_of((n_whole - 1) * kc, kc), kc, cr),
                         lambda cr: cr, carry)
        return lax.cond(has_half, lambda cr: body(pl.multiple_of(k_half, kh), kh, cr), lambda cr: cr, carry)

    @pl.when(i == 0)
    def _():
        klane = lax.broadcasted_iota(_I32, (kc, LANES), 1)
        klo = (klane & (HEAD_DIM // 2)) == 0

        def body(c, mx):
            k0 = pl.multiple_of(c * kc, kc)
            k = bk2_ref[pl.ds(k0, kc), :].astype(_F32)
            ka_ref[pl.ds(k0, kc), :] = jnp.where(klane == one_even, 1.0, k).astype(_BF16)
            kb_ref[pl.ds(k0, kc), :] = jnp.where(klane == one_odd, 1.0, k).astype(_BF16)
            return jnp.maximum(mx, jnp.sum(jnp.where(klo, k * k, 0.0), axis=1, keepdims=True))

        mx = lax.fori_loop(0, sc_ref.shape[0] // kc, body, jnp.zeros((kc, 1), _F32))
        kmax_ref[...] = jnp.broadcast_to(jnp.sqrt(jnp.max(mx, axis=0, keepdims=True)), kmax_ref.shape)

    def masked_half(ref, h):
        slab = ref[:, LANES * (h // 2):LANES * (h // 2 + 1)].astype(_F32)
        return jnp.where(lo_half if h % 2 == 0 else ~lo_half, slab, 0.0)

    for h in range(IDX_HEADS):
        iqm_ref[h * tq:(h + 1) * tq, :] = masked_half(iq_ref, h).astype(_BF16)
    kmax = kmax_ref[0:1, 0:1]
    shift_max = jnp.zeros((tq, 1), _F32)
    for p, h in enumerate(_DSA_HEAD_ORDER):
        qm = masked_half(bq_ref, h)
        shift = jnp.sqrt(jnp.sum(qm * qm, axis=1, keepdims=True)) * kmax
        shift_max = jnp.maximum(shift_max, shift)
        one_lane = one_even if h % 2 == 0 else one_odd
        bqm_ref[p * tq:(p + 1) * tq, :] = jnp.where(lane == one_lane, -shift, qm).astype(_BF16)
    bounded = jnp.max(shift_max) <= SHIFT_LIMIT

    w = [iwt_ref[h:h + 1, :] for h in range(IDX_HEADS)]
    def score_piece(k0, rows, diagonal):
        z = lax.dot_general(ik2_ref[pl.ds(k0, rows), :], iqm_ref[...], _NT, preferred_element_type=_F32)
        sc = w[0] * jnp.maximum(z[:, 0:LANES], 0.0)
        for h in range(1, IDX_HEADS):
            sc = sc + w[h] * jnp.maximum(z[:, LANES * h:LANES * (h + 1)], 0.0)
        if diagonal:
            kidx = k0 + lax.broadcasted_iota(_I32, (rows, LANES), 0)
            qidx = q0 + lax.broadcasted_iota(_I32, (rows, LANES), 1)
            sc = jnp.where(kidx <= qidx, sc, -jnp.inf)
        sc_ref[pl.ds(k0, rows), :] = sc
        top = lax.bitcast_convert_type(sc, _I32) & jnp.int32(-65536)
        sc16_ref[pl.ds(k0, rows), :] = lax.bitcast_convert_type(top, _F32).astype(_BF16)

    def score_body(c, carry):
        score_piece(pl.multiple_of(c * kc, kc), kc, False)
        return carry

    lax.fori_loop(0, n_whole - jnp.where(has_half, 0, 1), score_body, 0)

    @pl.when(has_half)
    def _():
        score_piece(pl.multiple_of(k_half, kh), kh, True)

    @pl.when(jnp.logical_not(has_half))
    def _():
        score_piece(pl.multiple_of((n_whole - 1) * kc, kc), kc, True)

    def count_ge(t):
        def body(k0, rows, acc):
            for h in range(0, rows, 512):
                ind = jnp.where(sc_ref[pl.ds(k0 + h, 512), :] >= t, 1, 0)
                acc = acc + jnp.sum(ind.reshape(64, 8, LANES), axis=0)
            return acc
        return jnp.sum(over_keys(body, jnp.zeros((8, LANES), _I32)), axis=0, keepdims=True)

    def count_ge16(t):
        t = t.astype(_BF16)

        def body(k0, rows, acc):
            d = pltpu.bitcast(sc16_ref[pl.ds(k0, rows), :] - t, _I32)
            neg = lax.shift_right_logical(d, 15) & 0x00010001
            return acc + jnp.sum(neg.reshape(rows // 16, 8, LANES), axis=0)
        acc = over_keys(body, jnp.zeros((8, LANES), _I32))
        below = jnp.sum((acc & 0xFFFF) + lax.shift_right_logical(acc, 16), axis=0, keepdims=True)
        return n_rows - below

    def search_pass(count, to_thr, st):
        lo, hi, clo, chi, t, thr = st
        t_up, t_dn = _midpoint(t, hi), _midpoint(lo, t)
        thr_up, thr_dn = to_thr(t_up), to_thr(t_dn)
        cnt = count(thr)
        ok = cnt >= n_sel
        return (jnp.where(ok, t, lo), jnp.where(ok, hi, t), jnp.where(ok, cnt, clo), jnp.where(ok, chi, cnt),
                jnp.where(ok, t_up, t_dn), jnp.where(ok, thr_up, thr_dn))

    vec = lambda v: jnp.full((1, LANES), v, _I32)
    st = (vec(KEY_NEG_INF >> 16), vec(1 << 15), vec(0) + n_rows, vec(0), vec(0), _key16_to_float(vec(0)))
    st = lax.fori_loop(0, 16, lambda j, s: search_pass(count_ge16, _key16_to_float, s), st)
    hint_lo = jnp.maximum(st[0] << 16, KEY_NEG_INF)
    hint_hi = jnp.where(st[1] >= (1 << 15), INT_MAX, st[1] << 16)
    hint_hi = jnp.where((hint_hi > 0) & (hint_hi < KEY_MIN_NORMAL), KEY_MIN_NORMAL, hint_hi)
    c1 = count_ge(_key_to_float(hint_lo))
    ok1 = c1 >= n_sel
    lo, clo = jnp.where(ok1, hint_lo, KEY_NEG_INF), jnp.where(ok1, c1, n_rows)
    hi, chi = jnp.where(ok1, INT_MAX, hint_lo), jnp.where(ok1, 0, c1)
    c2 = count_ge(_key_to_float(hint_hi))
    up2 = (c2 >= n_sel) & (hint_hi > lo)
    dn2 = (c2 < n_sel) & (hint_hi < hi)
    lo, clo = jnp.where(up2, hint_hi, lo), jnp.where(up2, c2, clo)
    hi, chi = jnp.where(dn2, hint_hi, hi), jnp.where(dn2, c2, chi)

    def unsettled(lo, hi, clo):
        return ~((clo == n_sel) | (hi - 1 <= lo) | ((lo >= 0) & (hi <= KEY_MIN_NORMAL)))

    def fine_group(c):
        s = c[1:7]
        for _ in range(SEARCH_GROUP):
            s = search_pass(count_ge, _key_to_float, s)
        return (c[0] + 1,) + s + (jnp.max(jnp.where(unsettled(*s[:3]), 1, 0)),)

    t0 = _midpoint(lo, hi)
    init = (jnp.int32(0), lo, hi, clo, chi, t0, _key_to_float(t0),
            jnp.max(jnp.where(unsettled(lo, hi, clo), 1, 0)))
    _, lo, hi, clo, chi, _, _, _ = lax.while_loop(lambda c: (c[7] > 0) & (c[0] < 16), fine_group, init)
    t_lo = _key_to_float(lo)
    t_hi = _key_to_float(hi)

    need = (clo > n_sel) & (lo > KEY_NEG_INF)

    @pl.when(jnp.max(jnp.where(need, 1, 0)) > 0)
    def _():
        tk = TIE_CHUNK
        room = (n_sel - chi).astype(_F32)
        before = (lax.broadcasted_iota(_I32, (tk, tk), 0) > lax.broadcasted_iota(_I32, (tk, tk), 1))
        before = jnp.where(before, 1.0, 0.0).astype(_BF16)

        def body(k0, rows, seen):
            for h in range(rows // tk):
                x = sc_ref[pl.ds(k0 + h * tk, tk), :]
                tie = (x >= t_lo) & ~(x >= t_hi)
                tf = jnp.where(tie, 1.0, 0.0)
                rank = seen + jnp.dot(before, tf.astype(_BF16), preferred_element_type=_F32)
                sc_ref[pl.ds(k0 + h * tk, tk), :] = jnp.where(tie & (rank >= room), -jnp.inf, x)
                seen = seen + jnp.sum(tf, axis=0, keepdims=True)
            return seen

        over_keys(body, jnp.zeros((1, LANES), _F32))

    thr = jnp.maximum(t_lo, F32_LOWEST)

    acc_e_ref[...] = jnp.zeros_like(acc_e_ref)
    acc_o_ref[...] = jnp.zeros_like(acc_o_ref)
    vt_rows = bvt_ref.shape[2]

    def masked_logits(k0, rows):
        sel = sc_ref[pl.ds(k0, rows), :] >= thr
        lge = lax.dot_general(ka_ref[pl.ds(k0, rows), :], bqm_ref[0:half * tq, :], _NT,
                              preferred_element_type=_F32)
        lgo = lax.dot_general(kb_ref[pl.ds(k0, rows), :], bqm_ref[half * tq:nh * tq, :], _NT,
                              preferred_element_type=_F32)
        tiles = lambda lg: [jnp.where(sel, lg[:, LANES * j:LANES * (j + 1)], NEG_BIG) for j in range(half)]
        vt = jnp.concatenate([bvt_ref[k0 // vt_rows + j] for j in range(rows // vt_rows)], axis=1)
        return ((acc_e_ref, tiles(lge), vt[0:LANES, :]), (acc_o_ref, tiles(lgo), vt[LANES:2 * LANES, :]))

    @pl.when(bounded)
    def _():
        def body(k0, rows, carry):
            for ref, tiles, v in masked_logits(k0, rows):
                p = jnp.concatenate([jnp.exp2(t).astype(_BF16) for t in tiles], axis=1)
                ref[...] += jnp.dot(v, p, preferred_element_type=_F32)
            return carry
        over_keys(body, jnp.int32(0))

    @pl.when(jnp.logical_not(bounded))
    def _():
        def body(k0, rows, m):
            m_out = []
            for g, (ref, tiles, v) in enumerate(masked_logits(k0, rows)):
                ps, alphas = [], []
                for j, t in enumerate(tiles):
                    sl = slice(LANES * (g * half + j), LANES * (g * half + j + 1))
                    m_new = jnp.maximum(m[:, sl], jnp.max(t, axis=0, keepdims=True))
                    ps.append(jnp.exp2(t - m_new).astype(_BF16))
                    alphas.append(jnp.exp2(m[:, sl] - m_new))
                    m_out.append(m_new)
                ref[...] = ref[...] * jnp.concatenate(alphas, axis=1) + jnp.dot(
                    v, jnp.concatenate(ps, axis=1), preferred_element_type=_F32)
            return jnp.concatenate(m_out, axis=1)
        over_keys(body, jnp.full((1, nh * LANES), NEG_BIG, _F32))

    row = lax.broadcasted_iota(_I32, (LANES, LANES), 0)
    for j in range(half):
        sl = slice(LANES * j, LANES * (j + 1))
        even = acc_e_ref[:, sl] / acc_e_ref[HEAD_DIM:HEAD_DIM + 1, sl]
        odd = acc_o_ref[:, sl] / acc_o_ref[0:1, sl]
        tile = jnp.where(row < HEAD_DIM, even, odd)
        out_ref[:, sl] = (tile.T * g_ref[:, sl]).astype(out_ref.dtype)


def _dsa_attention(iq, bq, iwt, gates, ki2, bvt, batch, seq):
    nb = seq // BLOCK
    n = batch * seq
    nvt = seq // PROJ_ROWS
    n_sel = min(TOPK_MAX, seq // 4)
    blk = lambda w, col=0: pl.BlockSpec((BLOCK, w), lambda b, i: (b * nb + i, col))
    return pl.pallas_call(
        functools.partial(_dsa_kernel, n_sel=n_sel),
        grid=(batch, nb),
        in_specs=[blk(256), blk(512),
                  pl.BlockSpec((16, BLOCK), lambda b, i: (0, b * nb + i)),
                  blk(512, 1),
                  pl.BlockSpec((seq, LANES), lambda b, i: (b, 0)),
                  pl.BlockSpec((seq, LANES), lambda b, i: (b, 1)),
                  pl.BlockSpec((nvt, 256, PROJ_ROWS), lambda b, i: (b, 0, 0))],
        out_specs=blk(512),
        out_shape=jax.ShapeDtypeStruct((n, 512), _BF16),
        scratch_shapes=[pltpu.VMEM((seq, LANES), _F32),
                        pltpu.VMEM((seq, LANES), _BF16),
                        pltpu.VMEM((IDX_HEADS * BLOCK, LANES), _BF16),
                        pltpu.VMEM((DSA_Q_HEADS * BLOCK, LANES), _BF16),
                        pltpu.VMEM((seq, LANES), _BF16),
                        pltpu.VMEM((seq, LANES), _BF16),
                        pltpu.VMEM((8, LANES), _F32),
                        pltpu.VMEM((LANES, 512), _F32),
                        pltpu.VMEM((LANES, 512), _F32)],
        compiler_params=pltpu.CompilerParams(dimension_semantics=("arbitrary", "arbitrary"),
                                             vmem_limit_bytes=VMEM_LIMIT),
        name="dsa_attention",
    )(iq, bq, iwt, gates, ki2, ki2, bvt)


def _out_kernel(a_ref, b_ref, x_ref, w_ref, bo_ref, gain_ref, bias_ref, out_ref, *, alpha):
    half = a_ref.shape[1]
    y = jnp.dot(a_ref[...], w_ref[0:half, :], preferred_element_type=_F32)
    y = y + jnp.dot(b_ref[...], w_ref[half:2 * half, :], preferred_element_type=_F32)
    z = alpha * x_ref[...] + (y + bo_ref[...])
    mu = jnp.mean(z, axis=-1, keepdims=True)
    zc = z - mu
    var = jnp.mean(zc * zc, axis=-1, keepdims=True)
    out_ref[...] = zc * lax.rsqrt(var + LN_EPS) * gain_ref[...] + bias_ref[...]


def _out_projection(a, b, x2, w_out, b_out, gain, bias, alpha):
    n, d = x2.shape
    tm = PROJ_ROWS
    row = lambda w: pl.BlockSpec((tm, w), lambda i: (i, 0))
    full = lambda a_: pl.BlockSpec(a_.shape, lambda i: (0,) * a_.ndim)
    consts = (w_out.astype(_BF16), b_out[None, :], gain[None, :], bias[None, :])
    return pl.pallas_call(
        functools.partial(_out_kernel, alpha=alpha),
        grid=(n // tm,),
        in_specs=[row(a.shape[1]), row(b.shape[1]), row(d)] + [full(c) for c in consts],
        out_specs=row(d),
        out_shape=jax.ShapeDtypeStruct((n, d), x2.dtype),
        compiler_params=pltpu.CompilerParams(dimension_semantics=("arbitrary",),
                                             vmem_limit_bytes=VMEM_LIMIT),
        name="out_projection",
    )(a, b, x2, *consts)


def _layer(h, pos2, w_in, b_in, sinks, w_out, b_out, gain, bias, alpha):
    batch, seq, d = h.shape
    x2 = h.reshape(batch * seq, d)
    aq, ak2, bq, iq, ki2, gates, bvt, avt, iwt = _projection(x2, pos2, w_in, b_in)
    a = _swa_attention(aq, ak2, avt, gates, sinks, batch, seq)
    b = _dsa_attention(iq, bq, iwt, gates, ki2, bvt, batch, seq)
    return _out_projection(a, b, x2, w_out, b_out, gain, bias, alpha).reshape(batch, seq, d)


def kernel(x, positions, w_in, b_in, swa_sinks, w_out, b_out, ln_gain, ln_bias):
    batch, seq, d = x.shape
    depth = w_in.shape[0]
    assert d == 1024 and seq % KEY_CHUNK == 0 and (batch * seq) % PROJ_ROWS == 0
    assert KEY_CHUNK % PROJ_ROWS == 0 and KEY_CHUNK % TIE_CHUNK == 0
    alpha = (2.0 * depth) ** 0.25
    pos2 = positions.reshape(batch * seq, 1)
    h = x
    for layer in range(depth):
        h = _layer(h, pos2, w_in[layer], b_in[layer], swa_sinks[layer], w_out[layer], b_out[layer],
                   ln_gain[layer], ln_bias[layer], alpha)
    return h
```

```python
import functools

import numpy as np
import jax
import jax.numpy as jnp
from jax import lax
from jax.experimental import pallas as pl
from jax.experimental.pallas import tpu as pltpu

HEAD_DIM = 64
SWA_Q_HEADS = 8
SWA_KV_HEADS = 2
DSA_Q_HEADS = 8
IDX_HEADS = 4
IDX_DIM = 64
WINDOW = 128
BLOCK = 128
TOPK_MAX = 256
ROPE_THETA = 10000.0
LN_EPS = 1e-5

LANES = 128
PROJ_ROWS = 512
KEY_CHUNK = 1024
TIE_CHUNK = 512
VMEM_LIMIT = 56 * 1024 * 1024

LOG2E = 1.4426950408889634
NEG_BIG = -1e30
SHIFT_LIMIT = 60.0
F32_LOWEST = -3.4028234663852886e38
INT_MAX = 2 ** 31 - 1
KEY_NEG_INF = 0x807FFFFF - 2 ** 32
KEY_MIN_NORMAL = 0x00800000
SEARCH_GROUP = 3

_F32 = jnp.float32
_BF16 = jnp.bfloat16
_I32 = jnp.int32
_NT = (((1,), (1,)), ((), ()))


def _split_columns(m):
    swa_w = SWA_Q_HEADS * HEAD_DIM
    kv_w = SWA_KV_HEADS * HEAD_DIM
    dsa_w = DSA_Q_HEADS * HEAD_DIM
    sizes = (swa_w, kv_w, kv_w, swa_w, dsa_w, HEAD_DIM, HEAD_DIM, dsa_w,
             IDX_HEADS * IDX_DIM, IDX_DIM, IDX_HEADS)
    o = np.cumsum((0,) + sizes)
    aq, ak, av, ag, bq, bk, bv, bg, iq, ik, iw = [m[:, o[i]:o[i + 1]] for i in range(len(sizes))]

    def dup(c):
        return jnp.concatenate([c[:, j:j + HEAD_DIM] for j in range(0, c.shape[1], HEAD_DIM) for _ in (0, 1)],
                               axis=1)

    def pairs(c):
        r = c.reshape(c.shape[0], -1, 2, 2, HEAD_DIM // 2)
        return jnp.swapaxes(r, 2, 3).reshape(c.shape[0], -1)

    roped = pairs(jnp.concatenate([aq, dup(ak), bq, iq, dup(bk), dup(ik)], axis=1))
    return jnp.concatenate([roped, ag, bg], axis=1), (bv, av[:, :HEAD_DIM], av[:, HEAD_DIM:]), iw


def _proj_kernel(x_ref, pos_ref, inv_ref, sgn_ref, wm_ref, bm_ref,
                 wvt_ref, bvt_ref, wiw_ref, biw_ref,
                 aq_ref, ak2_ref, bq_ref, iq_ref, ki2_ref, g_ref, bvt_out_ref, avt_out_ref, iwt_ref):
    xb = x_ref[...].astype(_BF16)
    ang = pos_ref[...].astype(_F32) * inv_ref[...]
    cos = jnp.cos(ang)
    sin = jnp.sin(ang) * sgn_ref[...]

    q_scale = 0.125 * LOG2E
    roped_dst = ((aq_ref, 0, q_scale), (aq_ref, 256, q_scale), (ak2_ref, 0, 1.0), (bq_ref, 0, q_scale),
                 (bq_ref, 256, q_scale), (iq_ref, 0, 1.0), (ki2_ref, 0, 1.0))
    for g, (dst, off, scale) in enumerate(roped_dst):
        c0 = 256 * g
        hm = jnp.dot(xb, wm_ref[:, c0:c0 + 256], preferred_element_type=_F32) + bm_ref[:, c0:c0 + 256]
        for s in range(2):
            sl = slice(LANES * s, LANES * (s + 1))
            o = hm[:, sl] * cos + pltpu.roll(hm[:, sl], LANES // 2, axis=1) * sin
            if scale != 1.0:
                o = o * scale
            dst[:, off + LANES * s:off + LANES * (s + 1)] = o.astype(dst.dtype)

    n_roped = 256 * len(roped_dst)
    for g in range(4):
        c0 = n_roped + 256 * g
        h = jnp.dot(xb, wm_ref[:, c0:c0 + 256], preferred_element_type=_F32) + bm_ref[:, c0:c0 + 256]
        g_ref[:, 256 * g:256 * (g + 1)] = h * (1.0 / (1.0 + jnp.exp(-h)))

    vt = lax.dot_general(wvt_ref[...], xb, _NT, preferred_element_type=_F32) + bvt_ref[...]
    n_b = bvt_out_ref.shape[1]
    bvt_out_ref[0] = vt[0:n_b].astype(bvt_out_ref.dtype)
    avt_out_ref[0] = vt[n_b:].astype(avt_out_ref.dtype)
    iwt = lax.dot_general(wiw_ref[...], xb, _NT, preferred_element_type=_F32) + biw_ref[...]
    iwt_ref[...] = iwt * (IDX_HEADS ** -0.5 * IDX_DIM ** -0.5)


def _projection(x2, pos2, w_in, b_in):
    n, d = x2.shape
    tm = PROJ_ROWS
    wm, wvs, wiw = _split_columns(w_in)
    bm, bvs, biw = _split_columns(b_in[None, :])
    wm = wm.astype(_BF16)
    zpad = jnp.zeros((HEAD_DIM, d), w_in.dtype)
    one_row = jnp.zeros((HEAD_DIM,), b_in.dtype).at[0].set(1.0)
    wvt = jnp.concatenate([p for wv in wvs for p in (wv.T, zpad, zpad, wv.T)], axis=0).astype(_BF16)
    bvt = jnp.concatenate([p for bv in bvs for p in (bv[0], one_row, one_row, bv[0])])[:, None]
    wiw_t = jnp.concatenate([wiw.T, jnp.zeros((16 - IDX_HEADS, d), w_in.dtype)], axis=0).astype(_BF16)
    biw_t = jnp.concatenate([biw[0], jnp.zeros((16 - IDX_HEADS,), b_in.dtype)])[:, None]
    half = HEAD_DIM // 2
    inv = ROPE_THETA ** (-jnp.arange(0, HEAD_DIM, 2, dtype=_F32) / HEAD_DIM)
    inv128 = jnp.tile(inv, LANES // half)[None, :]
    sgn128 = jnp.concatenate([-jnp.ones((LANES // 2,), _F32), jnp.ones((LANES // 2,), _F32)])[None, :]

    row = lambda w: pl.BlockSpec((tm, w), lambda i: (i, 0))
    full = lambda a: pl.BlockSpec(a.shape, lambda i: (0,) * a.ndim)
    out_shape = (
        jax.ShapeDtypeStruct((n, 512), _BF16),
        jax.ShapeDtypeStruct((n, 256), _BF16),
        jax.ShapeDtypeStruct((n, 512), _BF16),
        jax.ShapeDtypeStruct((n, 256), _BF16),
        jax.ShapeDtypeStruct((n, 256), _BF16),
        jax.ShapeDtypeStruct((n, 1024), _F32),
        jax.ShapeDtypeStruct((n // tm, 256, tm), _BF16),
        jax.ShapeDtypeStruct((n // tm, 512, tm), _BF16),
        jax.ShapeDtypeStruct((16, n), _F32),
    )
    out_specs = (row(512), row(256), row(512), row(256), row(256), row(1024),
                 pl.BlockSpec((1, 256, tm), lambda i: (i, 0, 0)),
                 pl.BlockSpec((1, 512, tm), lambda i: (i, 0, 0)),
                 pl.BlockSpec((16, tm), lambda i: (0, i)))
    consts = (inv128, sgn128, wm, bm, wvt, bvt, wiw_t, biw_t)
    return pl.pallas_call(
        _proj_kernel,
        grid=(n // tm,),
        in_specs=[row(d), row(1)] + [full(a) for a in consts],
        out_specs=out_specs,
        out_shape=out_shape,
        compiler_params=pltpu.CompilerParams(dimension_semantics=("arbitrary",),
                                             vmem_limit_bytes=VMEM_LIMIT),
        name="in_projection",
    )(x2, pos2, *consts)


def _swa_kernel(q_ref, kc_ref, kp_ref, vc_ref, vp_ref, g_ref, sink_ref, out_ref):
    i = pl.program_id(1)
    tq = BLOCK
    n_sub = q_ref.shape[0] // tq
    group = SWA_Q_HEADS // SWA_KV_HEADS
    lane = lax.broadcasted_iota(_I32, (tq, LANES), 1)
    lo_half = (lane & (HEAD_DIM // 2)) == 0
    r = lax.broadcasted_iota(_I32, (2 * tq, LANES), 0)
    c = lax.broadcasted_iota(_I32, (2 * tq, LANES), 1)
    in_window = (c < r) & (r <= c + WINDOW)
    row = lax.broadcasted_iota(_I32, (LANES, LANES), 0)

    kwin = jnp.concatenate([kp_ref[...], kc_ref[...]], axis=0)
    vwin = jnp.concatenate([vp_ref[0], vc_ref[0]], axis=1)
    for g in range(SWA_KV_HEADS):
        k = kwin[:, LANES * g:LANES * (g + 1)]
        for j in range(n_sub):
            valid = in_window if j > 0 else in_window & ((r >= tq) | (i > 0))
            rows = slice(tq * j, tq * (j + 1))
            win = slice(tq * j, tq * (j + 2))
            tiles = []
            for odd in (0, 1):
                qs = []
                for s in range(group // 2):
                    slab = q_ref[rows, LANES * (g * (group // 2) + s):LANES * (g * (group // 2) + s + 1)]
                    qs.append(jnp.where(lo_half if odd == 0 else ~lo_half, slab.astype(_F32), 0.0).astype(_BF16))
                qop = jnp.concatenate(qs, axis=0)
                logits = lax.dot_general(k[win], qop, _NT, preferred_element_type=_F32)
                heads = [group * g + 2 * s + odd for s in range(group // 2)]
                ps, sink_terms = [], []
                for s, h in enumerate(heads):
                    lg = jnp.where(valid, logits[:, LANES * s:LANES * (s + 1)], -jnp.inf)
                    sink = sink_ref[h:h + 1, :]
                    m = jnp.maximum(jnp.max(lg, axis=0, keepdims=True), sink)
                    ps.append(jnp.exp2(lg - m).astype(_BF16))
                    sink_terms.append(jnp.exp2(sink - m))
                v = vwin[LANES * (2 * g + odd):LANES * (2 * g + odd + 1), win]
                acc = jnp.dot(v, jnp.concatenate(ps, axis=1), preferred_element_type=_F32)
                l_row = HEAD_DIM if odd == 0 else 0
                denom = acc[l_row:l_row + 1, :] + jnp.concatenate(sink_terms, axis=1)
                tiles.append(acc / denom)
            for s in range(group // 2):
                sl = slice(LANES * s, LANES * (s + 1))
                tile = jnp.where(row < HEAD_DIM, tiles[0][:, sl], tiles[1][:, sl])
                cols = slice(LANES * (g * (group // 2) + s), LANES * (g * (group // 2) + s + 1))
                out_ref[rows, cols] = (tile.T * g_ref[rows, cols]).astype(out_ref.dtype)


def _swa_attention(aq, ak2, avt, gates, sinks, batch, seq):
    tm = PROJ_ROWS
    ns = seq // tm
    n = batch * seq
    per = tm // BLOCK
    sink_b = jnp.broadcast_to((sinks.astype(_F32) * LOG2E)[:, None], (SWA_Q_HEADS, LANES))
    cur = lambda w: pl.BlockSpec((tm, w), lambda b, i: (b * ns + i, 0))
    return pl.pallas_call(
        _swa_kernel,
        grid=(batch, ns),
        in_specs=[cur(512), cur(256),
                  pl.BlockSpec((BLOCK, 256), lambda b, i: (b * ns * per + jnp.maximum(i * per - 1, 0), 0)),
                  pl.BlockSpec((1, 512, tm), lambda b, i: (b * ns + i, 0, 0)),
                  pl.BlockSpec((1, 512, BLOCK), lambda b, i: (b * ns + jnp.maximum(i - 1, 0), 0, per - 1)),
                  cur(512),
                  pl.BlockSpec((SWA_Q_HEADS, LANES), lambda b, i: (0, 0))],
        out_specs=cur(512),
        out_shape=jax.ShapeDtypeStruct((n, 512), _BF16),
        compiler_params=pltpu.CompilerParams(dimension_semantics=("arbitrary", "arbitrary"),
                                             vmem_limit_bytes=VMEM_LIMIT),
        name="swa_attention",
    )(aq, ak2, ak2, avt, avt, gates, sink_b)


_DSA_HEAD_ORDER = (0, 2, 4, 6, 1, 3, 5, 7)


def _key_to_float(k):
    return lax.bitcast_convert_type(jnp.where(k < 0, k ^ 0x7FFFFFFF, k), _F32)


def _key16_to_float(k):
    return lax.bitcast_convert_type(jnp.where(k <= 0, k ^ 0x7FFF ^ jnp.where(k == 0, -1, 0), k) << 16, _F32)


def _midpoint(a, b):
    return (a >> 1) + (b >> 1) + (a & b & 1)


def _tree_sum(parts, chains=8):
    accs = list(parts[:chains])
    for j in range(chains, len(parts)):
        accs[j % chains] = accs[j % chains] + parts[j]
    while len(accs) > 1:
        accs = [accs[2 * j] + accs[2 * j + 1] for j in range(len(accs) // 2)]
    return accs[0]


def _dsa_kernel(iq_ref, bq_ref, iwt_ref, g_ref, bk2_ref, ik2_ref, bvt_ref, out_ref,
                sc_ref, sc16_ref, iqm_ref, bqm_ref, ka_ref, kb_ref, kmax_ref, acc_e_ref, acc_o_ref, *, n_sel):
    tq, kc = BLOCK, KEY_CHUNK
    nh, half = DSA_Q_HEADS, DSA_Q_HEADS // 2
    i = pl.program_id(1)
    q0 = i * tq
    lane = lax.broadcasted_iota(_I32, (tq, LANES), 1)
    lo_half = (lane & (HEAD_DIM // 2)) == 0
    one_even, one_odd = HEAD_DIM // 2, 0

    kh = kc // 2
    rem = (q0 + tq) % kc
    n_whole = (q0 + tq) // kc + jnp.where(rem > kh, 1, 0)
    has_half = (rem > 0) & (rem <= kh)
    k_half = n_whole * kc
    n_rows = k_half + jnp.where(has_half, kh, 0)

    def over_keys(body, carry):
        def two(c, cr):
            k0 = pl.multiple_of(c * 2 * kc, 2 * kc)
            return body(k0 + kc, kc, body(k0, kc, cr))

        carry = lax.fori_loop(0, n_whole // 2, two, carry)
        carry = lax.cond(n_whole % 2 == 1, lambda cr: body(pl.multiple_of((n_whole - 1) * kc, kc), kc, cr),
                         lambda cr: cr, carry)
        return lax.cond(has_half, lambda cr: body(pl.multiple_of(k_half, kh), kh, cr), lambda cr: cr, carry)

    @pl.when(i == 0)
    def _():
        klane = lax.broadcasted_iota(_I32, (kc, LANES), 1)
        klo = (klane & (HEAD_DIM // 2)) == 0

        def body(c, mx):
            k0 = pl.multiple_of(c * kc, kc)
            k = bk2_ref[pl.ds(k0, kc), :].astype(_F32)
            ka_ref[pl.ds(k0, kc), :] = jnp.where(klane == one_even, 1.0, k).astype(_BF16)
            kb_ref[pl.ds(k0, kc), :] = jnp.where(klane == one_odd, 1.0, k).astype(_BF16)
            return jnp.maximum(mx, jnp.sum(jnp.where(klo, k * k, 0.0), axis=1, keepdims=True))

        mx = lax.fori_loop(0, sc_ref.shape[0] // kc, body, jnp.zeros((kc, 1), _F32))
        kmax_ref[...] = jnp.broadcast_to(jnp.sqrt(jnp.max(mx, axis=0, keepdims=True)), kmax_ref.shape)

    def masked_half(ref, h):
        slab = ref[:, LANES * (h // 2):LANES * (h // 2 + 1)].astype(_F32)
        return jnp.where(lo_half if h % 2 == 0 else ~lo_half, slab, 0.0)

    for h in range(IDX_HEADS):
        iqm_ref[h * tq:(h + 1) * tq, :] = masked_half(iq_ref, h).astype(_BF16)
    kmax = kmax_ref[0:1, 0:1]
    shift_max = jnp.zeros((tq, 1), _F32)
    for p, h in enumerate(_DSA_HEAD_ORDER):
        qm = masked_half(bq_ref, h)
        shift = jnp.sqrt(jnp.sum(qm * qm, axis=1, keepdims=True)) * kmax
        shift_max = jnp.maximum(shift_max, shift)
        one_lane = one_even if h % 2 == 0 else one_odd
        bqm_ref[p * tq:(p + 1) * tq, :] = jnp.where(lane == one_lane, -shift, qm).astype(_BF16)
    bounded = jnp.max(shift_max) <= SHIFT_LIMIT

    w = [iwt_ref[h:h + 1, :] for h in range(IDX_HEADS)]
    def score_piece(k0, rows, diagonal):
        z = lax.dot_general(ik2_ref[pl.ds(k0, rows), :], iqm_ref[...], _NT, preferred_element_type=_F32)
        sc = w[0] * jnp.maximum(z[:, 0:LANES], 0.0)
        for h in range(1, IDX_HEADS):
            sc = sc + w[h] * jnp.maximum(z[:, LANES * h:LANES * (h + 1)], 0.0)
        if diagonal:
            kidx = k0 + lax.broadcasted_iota(_I32, (rows, LANES), 0)
            qidx = q0 + lax.broadcasted_iota(_I32, (rows, LANES), 1)
            sc = jnp.where(kidx <= qidx, sc, -jnp.inf)
        sc_ref[pl.ds(k0, rows), :] = sc
        top = lax.bitcast_convert_type(sc, _I32) & jnp.int32(-65536)
        sc16_ref[pl.ds(k0, rows), :] = lax.bitcast_convert_type(top, _F32).astype(_BF16)

    def score_body(c, carry):
        score_piece(pl.multiple_of(c * kc, kc), kc, False)
        return carry

    lax.fori_loop(0, n_whole - jnp.where(has_half, 0, 1), score_body, 0)

    @pl.when(has_half)
    def _():
        score_piece(pl.multiple_of(k_half, kh), kh, True)

    @pl.when(jnp.logical_not(has_half))
    def _():
        score_piece(pl.multiple_of((n_whole - 1) * kc, kc), kc, True)

    def count_ge(t):
        def body(k0, rows, acc):
            for h in range(0, rows, 512):
                ind = jnp.where(sc_ref[pl.ds(k0 + h, 512), :] >= t, 1, 0)
                acc = acc + jnp.sum(ind.reshape(64, 8, LANES), axis=0)
            return acc
        return jnp.sum(over_keys(body, jnp.zeros((8, LANES), _I32)), axis=0, keepdims=True)

    def count_ge16(t):
        t = t.astype(_BF16)

        def body(k0, rows, acc):
            d = pltpu.bitcast(sc16_ref[pl.ds(k0, rows), :] - t, _I32)
            neg = lax.shift_right_logical(d, 15) & 0x00010001
            return acc + jnp.sum(neg.reshape(rows // 16, 8, LANES), axis=0)
        acc = over_keys(body, jnp.zeros((8, LANES), _I32))
        below = jnp.sum((acc & 0xFFFF) + lax.shift_right_logical(acc, 16), axis=0, keepdims=True)
        return n_rows - below

    def search_pass(count, to_thr, st):
        lo, hi, clo, chi, t, thr = st
        t_up, t_dn = _midpoint(t, hi), _midpoint(lo, t)
        thr_up, thr_dn = to_thr(t_up), to_thr(t_dn)
        cnt = count(thr)
        ok = cnt >= n_sel
        return (jnp.where(ok, t, lo), jnp.where(ok, hi, t), jnp.where(ok, cnt, clo), jnp.where(ok, chi, cnt),
                jnp.where(ok, t_up, t_dn), jnp.where(ok, thr_up, thr_dn))

    vec = lambda v: jnp.full((1, LANES), v, _I32)
    st = (vec(KEY_NEG_INF >> 16), vec(1 << 15), vec(0) + n_rows, vec(0), vec(0), _key16_to_float(vec(0)))
    st = lax.fori_loop(0, 16, lambda j, s: search_pass(count_ge16, _key16_to_float, s), st)
    def unsettled(lo, hi, clo):
        return ~((clo == n_sel) | (hi - 1 <= lo) | ((lo >= 0) & (hi <= KEY_MIN_NORMAL)))

    def search_loop(count, lo, hi, clo, chi, check_first):
        def group(c):
            s = c[1:7]
            for _ in range(SEARCH_GROUP):
                s = search_pass(count, _key_to_float, s)
            return (c[0] + 1,) + s + (jnp.max(jnp.where(unsettled(*s[:3]), 1, 0)),)

        t0 = _midpoint(lo, hi)
        go = jnp.max(jnp.where(unsettled(lo, hi, clo), 1, 0)) if check_first else jnp.int32(1)
        init = (jnp.int32(0), lo, hi, clo, chi, t0, _key_to_float(t0), go)
        return lax.while_loop(lambda c: (c[7] > 0) & (c[0] < 16), group, init)[1:5]

    def count_ge_sign(t):
        def body(k0, rows, acc):
            for h in range(0, rows, 512):
                d = lax.bitcast_convert_type(sc_ref[pl.ds(k0 + h, 512), :] - t, _I32)
                acc = acc + jnp.sum(lax.shift_right_logical(d, 31).reshape(64, 8, LANES), axis=0)
            return acc
        return n_rows - jnp.sum(over_keys(body, jnp.zeros((8, LANES), _I32)), axis=0, keepdims=True)

    hint_lo = jnp.maximum(st[0] << 16, KEY_NEG_INF)
    hint_hi = jnp.where(st[1] >= (1 << 15), INT_MAX, st[1] << 16)
    above_zero = lambda k: jnp.where((k > 0) & (k < KEY_MIN_NORMAL), KEY_MIN_NORMAL, k)
    hint_lo, hint_hi, _, _ = search_loop(count_ge_sign, hint_lo, above_zero(hint_hi), st[2], st[3], False)
    hint_hi = above_zero(hint_hi)
    c1 = count_ge(_key_to_float(hint_lo))
    ok1 = c1 >= n_sel
    lo, clo = jnp.where(ok1, hint_lo, KEY_NEG_INF), jnp.where(ok1, c1, n_rows)
    hi, chi = jnp.where(ok1, INT_MAX, hint_lo), jnp.where(ok1, 0, c1)
    c2 = count_ge(_key_to_float(hint_hi))
    up2 = (c2 >= n_sel) & (hint_hi > lo)
    dn2 = (c2 < n_sel) & (hint_hi < hi)
    lo, clo = jnp.where(up2, hint_hi, lo), jnp.where(up2, c2, clo)
    hi, chi = jnp.where(dn2, hint_hi, hi), jnp.where(dn2, c2, chi)

    lo, hi, clo, chi = search_loop(count_ge, lo, hi, clo, chi, True)
    t_lo = _key_to_float(lo)
    t_hi = _key_to_float(hi)

    need = (clo > n_sel) & (lo > KEY_NEG_INF)

    @pl.when(jnp.max(jnp.where(need, 1, 0)) > 0)
    def _():
        tk = TIE_CHUNK
        room = (n_sel - chi).astype(_F32)
        before = (lax.broadcasted_iota(_I32, (tk, tk), 0) > lax.broadcasted_iota(_I32, (tk, tk), 1))
        before = jnp.where(before, 1.0, 0.0).astype(_BF16)

        def body(k0, rows, seen):
            for h in range(rows // tk):
                x = sc_ref[pl.ds(k0 + h * tk, tk), :]
                tie = (x >= t_lo) & ~(x >= t_hi)
                tf = jnp.where(tie, 1.0, 0.0)
                rank = seen + jnp.dot(before, tf.astype(_BF16), preferred_element_type=_F32)
                sc_ref[pl.ds(k0 + h * tk, tk), :] = jnp.where(tie & (rank >= room), -jnp.inf, x)
                seen = seen + jnp.sum(tf, axis=0, keepdims=True)
            return seen

        over_keys(body, jnp.zeros((1, LANES), _F32))

    thr = jnp.maximum(t_lo, F32_LOWEST)

    acc_e_ref[...] = jnp.zeros_like(acc_e_ref)
    acc_o_ref[...] = jnp.zeros_like(acc_o_ref)
    vt_rows = bvt_ref.shape[2]

    def masked_logits(k0, rows):
        sel = sc_ref[pl.ds(k0, rows), :] >= thr
        lge = lax.dot_general(ka_ref[pl.ds(k0, rows), :], bqm_ref[0:half * tq, :], _NT,
                              preferred_element_type=_F32)
        lgo = lax.dot_general(kb_ref[pl.ds(k0, rows), :], bqm_ref[half * tq:nh * tq, :], _NT,
                              preferred_element_type=_F32)
        tiles = lambda lg: [jnp.where(sel, lg[:, LANES * j:LANES * (j + 1)], NEG_BIG) for j in range(half)]
        vt = jnp.concatenate([bvt_ref[k0 // vt_rows + j] for j in range(rows // vt_rows)], axis=1)
        return ((acc_e_ref, tiles(lge), vt[0:LANES, :]), (acc_o_ref, tiles(lgo), vt[LANES:2 * LANES, :]))

    @pl.when(bounded)
    def _():
        def body(k0, rows, carry):
            for ref, tiles, v in masked_logits(k0, rows):
                p = jnp.concatenate([jnp.exp2(t).astype(_BF16) for t in tiles], axis=1)
                ref[...] += jnp.dot(v, p, preferred_element_type=_F32)
            return carry
        over_keys(body, jnp.int32(0))

    @pl.when(jnp.logical_not(bounded))
    def _():
        def body(k0, rows, m):
            m_out = []
            for g, (ref, tiles, v) in enumerate(masked_logits(k0, rows)):
                ps, alphas = [], []
                for j, t in enumerate(tiles):
                    sl = slice(LANES * (g * half + j), LANES * (g * half + j + 1))
                    m_new = jnp.maximum(m[:, sl], jnp.max(t, axis=0, keepdims=True))
                    ps.append(jnp.exp2(t - m_new).astype(_BF16))
                    alphas.append(jnp.exp2(m[:, sl] - m_new))
                    m_out.append(m_new)
                ref[...] = ref[...] * jnp.concatenate(alphas, axis=1) + jnp.dot(
                    v, jnp.concatenate(ps, axis=1), preferred_element_type=_F32)
            return jnp.concatenate(m_out, axis=1)
        over_keys(body, jnp.full((1, nh * LANES), NEG_BIG, _F32))

    row = lax.broadcasted_iota(_I32, (LANES, LANES), 0)
    for j in range(half):
        sl = slice(LANES * j, LANES * (j + 1))
        even = acc_e_ref[:, sl] / acc_e_ref[HEAD_DIM:HEAD_DIM + 1, sl]
        odd = acc_o_ref[:, sl] / acc_o_ref[0:1, sl]
        tile = jnp.where(row < HEAD_DIM, even, odd)
        out_ref[:, sl] = (tile.T * g_ref[:, sl]).astype(out_ref.dtype)


def _dsa_attention(iq, bq, iwt, gates, ki2, bvt, batch, seq):
    nb = seq // BLOCK
    n = batch * seq
    nvt = seq // PROJ_ROWS
    n_sel = min(TOPK_MAX, seq // 4)
    blk = lambda w, col=0: pl.BlockSpec((BLOCK, w), lambda b, i: (b * nb + i, col))
    return pl.pallas_call(
        functools.partial(_dsa_kernel, n_sel=n_sel),
        grid=(batch, nb),
        in_specs=[blk(256), blk(512),
                  pl.BlockSpec((16, BLOCK), lambda b, i: (0, b * nb + i)),
                  blk(512, 1),
                  pl.BlockSpec((seq, LANES), lambda b, i: (b, 0)),
                  pl.BlockSpec((seq, LANES), lambda b, i: (b, 1)),
                  pl.BlockSpec((nvt, 256, PROJ_ROWS), lambda b, i: (b, 0, 0))],
        out_specs=blk(512),
        out_shape=jax.ShapeDtypeStruct((n, 512), _BF16),
        scratch_shapes=[pltpu.VMEM((seq, LANES), _F32),
                        pltpu.VMEM((seq, LANES), _BF16),
                        pltpu.VMEM((IDX_HEADS * BLOCK, LANES), _BF16),
                        pltpu.VMEM((DSA_Q_HEADS * BLOCK, LANES), _BF16),
                        pltpu.VMEM((seq, LANES), _BF16),
                        pltpu.VMEM((seq, LANES), _BF16),
                        pltpu.VMEM((8, LANES), _F32),
                        pltpu.VMEM((LANES, 512), _F32),
                        pltpu.VMEM((LANES, 512), _F32)],
        compiler_params=pltpu.CompilerParams(dimension_semantics=("arbitrary", "arbitrary"),
                                             vmem_limit_bytes=VMEM_LIMIT),
        name="dsa_attention",
    )(iq, bq, iwt, gates, ki2, ki2, bvt)


def _out_kernel(a_ref, b_ref, x_ref, w_ref, bo_ref, gain_ref, bias_ref, out_ref, *, alpha):
    half = a_ref.shape[1]
    y = jnp.dot(a_ref[...], w_ref[0:half, :], preferred_element_type=_F32)
    y = y + jnp.dot(b_ref[...], w_ref[half:2 * half, :], preferred_element_type=_F32)
    z = alpha * x_ref[...] + (y + bo_ref[...])
    mu = jnp.mean(z, axis=-1, keepdims=True)
    zc = z - mu
    var = jnp.mean(zc * zc, axis=-1, keepdims=True)
    out_ref[...] = zc * lax.rsqrt(var + LN_EPS) * gain_ref[...] + bias_ref[...]


def _out_projection(a, b, x2, w_out, b_out, gain, bias, alpha):
    n, d = x2.shape
    tm = PROJ_ROWS
    row = lambda w: pl.BlockSpec((tm, w), lambda i: (i, 0))
    full = lambda a_: pl.BlockSpec(a_.shape, lambda i: (0,) * a_.ndim)
    consts = (w_out.astype(_BF16), b_out[None, :], gain[None, :], bias[None, :])
    return pl.pallas_call(
        functools.partial(_out_kernel, alpha=alpha),
        grid=(n // tm,),
        in_specs=[row(a.shape[1]), row(b.shape[1]), row(d)] + [full(c) for c in consts],
        out_specs=row(d),
        out_shape=jax.ShapeDtypeStruct((n, d), x2.dtype),
        compiler_params=pltpu.CompilerParams(dimension_semantics=("arbitrary",),
                                             vmem_limit_bytes=VMEM_LIMIT),
        name="out_projection",
    )(a, b, x2, *consts)


def _layer(h, pos2, w_in, b_in, sinks, w_out, b_out, gain, bias, alpha):
    batch, seq, d = h.shape
    x2 = h.reshape(batch * seq, d)
    aq, ak2, bq, iq, ki2, gates, bvt, avt, iwt = _projection(x2, pos2, w_in, b_in)
    a = _swa_attention(aq, ak2, avt, gates, sinks, batch, seq)
    b = _dsa_attention(iq, bq, iwt, gates, ki2, bvt, batch, seq)
    return _out_projection(a, b, x2, w_out, b_out, gain, bias, alpha).reshape(batch, seq, d)


def kernel(x, positions, w_in, b_in, swa_sinks, w_out, b_out, ln_gain, ln_bias):
    batch, seq, d = x.shape
    depth = w_in.shape[0]
    assert d == 1024 and seq % KEY_CHUNK == 0 and (batch * seq) % PROJ_ROWS == 0
    assert KEY_CHUNK % PROJ_ROWS == 0 and KEY_CHUNK % TIE_CHUNK == 0
    alpha = (2.0 * depth) ** 0.25
    pos2 = positions.reshape(batch * seq, 1)
    h = x
    for layer in range(depth):
        h = _layer(h, pos2, w_in[layer], b_in[layer], swa_sinks[layer], w_out[layer], b_out[layer],
                   ln_gain[layer], ln_bias[layer], alpha)
    return h
```

```python
import functools

import numpy as np
import jax
import jax.numpy as jnp
from jax import lax
from jax.experimental import pallas as pl
from jax.experimental.pallas import tpu as pltpu

HEAD_DIM = 64
SWA_Q_HEADS = 8
SWA_KV_HEADS = 2
DSA_Q_HEADS = 8
IDX_HEADS = 4
IDX_DIM = 64
WINDOW = 128
BLOCK = 128
TOPK_MAX = 256
ROPE_THETA = 10000.0
LN_EPS = 1e-5

LANES = 128
PROJ_ROWS = 512
KEY_CHUNK = 1024
TIE_CHUNK = 512
VT_ROWS = 80
VMEM_LIMIT = 56 * 1024 * 1024

LOG2E = 1.4426950408889634
NEG_BIG = -1e30
SHIFT_LIMIT = 60.0
F32_LOWEST = -3.4028234663852886e38
INT_MAX = 2 ** 31 - 1
KEY_NEG_INF = 0x807FFFFF - 2 ** 32
KEY_MIN_NORMAL = 0x00800000
SEARCH_GROUP = 3

_F32 = jnp.float32
_BF16 = jnp.bfloat16
_I32 = jnp.int32
_NT = (((1,), (1,)), ((), ()))


def _split_columns(m):
    swa_w = SWA_Q_HEADS * HEAD_DIM
    kv_w = SWA_KV_HEADS * HEAD_DIM
    dsa_w = DSA_Q_HEADS * HEAD_DIM
    sizes = (swa_w, kv_w, kv_w, swa_w, dsa_w, HEAD_DIM, HEAD_DIM, dsa_w,
             IDX_HEADS * IDX_DIM, IDX_DIM, IDX_HEADS)
    o = np.cumsum((0,) + sizes)
    aq, ak, av, ag, bq, bk, bv, bg, iq, ik, iw = [m[:, o[i]:o[i + 1]] for i in range(len(sizes))]

    def dup(c):
        return jnp.concatenate([c[:, j:j + HEAD_DIM] for j in range(0, c.shape[1], HEAD_DIM) for _ in (0, 1)],
                               axis=1)

    def pairs(c):
        r = c.reshape(c.shape[0], -1, 2, 2, HEAD_DIM // 2)
        return jnp.swapaxes(r, 2, 3).reshape(c.shape[0], -1)

    roped = pairs(jnp.concatenate([aq, dup(ak), bq, iq, dup(bk), dup(ik)], axis=1))
    return jnp.concatenate([roped, ag, bg], axis=1), (bv, av[:, :HEAD_DIM], av[:, HEAD_DIM:]), iw


def _proj_kernel(x_ref, pos_ref, inv_ref, sgn_ref, wm_ref, bm_ref,
                 wvt_ref, bvt_ref, wiw_ref, biw_ref,
                 aq_ref, ak2_ref, bq_ref, iq_ref, ki2_ref, g_ref, bvt_out_ref, avt_out_ref, iwt_ref):
    xb = x_ref[...].astype(_BF16)
    ang = pos_ref[...].astype(_F32) * inv_ref[...]
    cos = jnp.cos(ang)
    sin = jnp.sin(ang) * sgn_ref[...]

    q_scale = 0.125 * LOG2E
    roped_dst = ((aq_ref, 0, q_scale), (aq_ref, 256, q_scale), (ak2_ref, 0, 1.0), (bq_ref, 0, q_scale),
                 (bq_ref, 256, q_scale), (iq_ref, 0, 1.0), (ki2_ref, 0, 1.0))
    for g, (dst, off, scale) in enumerate(roped_dst):
        c0 = 256 * g
        hm = jnp.dot(xb, wm_ref[:, c0:c0 + 256], preferred_element_type=_F32) + bm_ref[:, c0:c0 + 256]
        for s in range(2):
            sl = slice(LANES * s, LANES * (s + 1))
            o = hm[:, sl] * cos + pltpu.roll(hm[:, sl], LANES // 2, axis=1) * sin
            if scale != 1.0:
                o = o * scale
            dst[:, off + LANES * s:off + LANES * (s + 1)] = o.astype(dst.dtype)

    n_roped = 256 * len(roped_dst)
    for g in range(4):
        c0 = n_roped + 256 * g
        h = jnp.dot(xb, wm_ref[:, c0:c0 + 256], preferred_element_type=_F32) + bm_ref[:, c0:c0 + 256]
        g_ref[:, 256 * g:256 * (g + 1)] = h * (1.0 / (1.0 + jnp.exp(-h)))

    vt = lax.dot_general(wvt_ref[...], xb, _NT, preferred_element_type=_F32) + bvt_ref[...]
    n_b = bvt_out_ref.shape[1]
    bvt_out_ref[0] = vt[0:n_b].astype(bvt_out_ref.dtype)
    avt_out_ref[0] = vt[n_b:].astype(avt_out_ref.dtype)
    iwt = lax.dot_general(wiw_ref[...], xb, _NT, preferred_element_type=_F32) + biw_ref[...]
    iwt_ref[...] = iwt * (IDX_HEADS ** -0.5 * IDX_DIM ** -0.5)


def _projection(x2, pos2, w_in, b_in):
    n, d = x2.shape
    tm = PROJ_ROWS
    wm, wvs, wiw = _split_columns(w_in)
    bm, bvs, biw = _split_columns(b_in[None, :])
    wm = wm.astype(_BF16)
    zpad = jnp.zeros((HEAD_DIM, d), w_in.dtype)
    one_row = jnp.zeros((HEAD_DIM,), b_in.dtype).at[0].set(1.0)
    extra = VT_ROWS - HEAD_DIM
    wvt = jnp.concatenate([wvs[0].T, zpad[:extra]] + [p for wv in wvs[1:] for p in (wv.T, zpad, zpad, wv.T)],
                          axis=0).astype(_BF16)
    bvt = jnp.concatenate([bvs[0][0], one_row[:extra]]
                          + [p for bv in bvs[1:] for p in (bv[0], one_row, one_row, bv[0])])[:, None]
    wiw_t = jnp.concatenate([wiw.T, jnp.zeros((16 - IDX_HEADS, d), w_in.dtype)], axis=0).astype(_BF16)
    biw_t = jnp.concatenate([biw[0], jnp.zeros((16 - IDX_HEADS,), b_in.dtype)])[:, None]
    half = HEAD_DIM // 2
    inv = ROPE_THETA ** (-jnp.arange(0, HEAD_DIM, 2, dtype=_F32) / HEAD_DIM)
    inv128 = jnp.tile(inv, LANES // half)[None, :]
    sgn128 = jnp.concatenate([-jnp.ones((LANES // 2,), _F32), jnp.ones((LANES // 2,), _F32)])[None, :]

    row = lambda w: pl.BlockSpec((tm, w), lambda i: (i, 0))
    full = lambda a: pl.BlockSpec(a.shape, lambda i: (0,) * a.ndim)
    out_shape = (
        jax.ShapeDtypeStruct((n, 512), _BF16),
        jax.ShapeDtypeStruct((n, 256), _BF16),
        jax.ShapeDtypeStruct((n, 512), _BF16),
        jax.ShapeDtypeStruct((n, 256), _BF16),
        jax.ShapeDtypeStruct((n, 256), _BF16),
        jax.ShapeDtypeStruct((n, 1024), _F32),
        jax.ShapeDtypeStruct((n // tm, VT_ROWS, tm), _BF16),
        jax.ShapeDtypeStruct((n // tm, 512, tm), _BF16),
        jax.ShapeDtypeStruct((16, n), _F32),
    )
    out_specs = (row(512), row(256), row(512), row(256), row(256), row(1024),
                 pl.BlockSpec((1, VT_ROWS, tm), lambda i: (i, 0, 0)),
                 pl.BlockSpec((1, 512, tm), lambda i: (i, 0, 0)),
                 pl.BlockSpec((16, tm), lambda i: (0, i)))
    consts = (inv128, sgn128, wm, bm, wvt, bvt, wiw_t, biw_t)
    return pl.pallas_call(
        _proj_kernel,
        grid=(n // tm,),
        in_specs=[row(d), row(1)] + [full(a) for a in consts],
        out_specs=out_specs,
        out_shape=out_shape,
        compiler_params=pltpu.CompilerParams(dimension_semantics=("arbitrary",),
                                             vmem_limit_bytes=VMEM_LIMIT),
        name="in_projection",
    )(x2, pos2, *consts)


def _swa_kernel(q_ref, kc_ref, kp_ref, vc_ref, vp_ref, g_ref, sink_ref, out_ref):
    i = pl.program_id(1)
    tq = BLOCK
    n_sub = q_ref.shape[0] // tq
    group = SWA_Q_HEADS // SWA_KV_HEADS
    lane = lax.broadcasted_iota(_I32, (tq, LANES), 1)
    lo_half = (lane & (HEAD_DIM // 2)) == 0
    r = lax.broadcasted_iota(_I32, (2 * tq, LANES), 0)
    c = lax.broadcasted_iota(_I32, (2 * tq, LANES), 1)
    in_window = (c < r) & (r <= c + WINDOW)
    row = lax.broadcasted_iota(_I32, (LANES, LANES), 0)

    kwin = jnp.concatenate([kp_ref[...], kc_ref[...]], axis=0)
    vwin = jnp.concatenate([vp_ref[0], vc_ref[0]], axis=1)
    for g in range(SWA_KV_HEADS):
        k = kwin[:, LANES * g:LANES * (g + 1)]
        for j in range(n_sub):
            valid = in_window if j > 0 else in_window & ((r >= tq) | (i > 0))
            rows = slice(tq * j, tq * (j + 1))
            win = slice(tq * j, tq * (j + 2))
            tiles = []
            for odd in (0, 1):
                qs = []
                for s in range(group // 2):
                    slab = q_ref[rows, LANES * (g * (group // 2) + s):LANES * (g * (group // 2) + s + 1)]
                    qs.append(jnp.where(lo_half if odd == 0 else ~lo_half, slab.astype(_F32), 0.0).astype(_BF16))
                qop = jnp.concatenate(qs, axis=0)
                logits = lax.dot_general(k[win], qop, _NT, preferred_element_type=_F32)
                heads = [group * g + 2 * s + odd for s in range(group // 2)]
                ps, sink_terms = [], []
                for s, h in enumerate(heads):
                    lg = jnp.where(valid, logits[:, LANES * s:LANES * (s + 1)], -jnp.inf)
                    sink = sink_ref[h:h + 1, :]
                    m = jnp.maximum(jnp.max(lg, axis=0, keepdims=True), sink)
                    ps.append(jnp.exp2(lg - m).astype(_BF16))
                    sink_terms.append(jnp.exp2(sink - m))
                v = vwin[LANES * (2 * g + odd):LANES * (2 * g + odd + 1), win]
                acc = jnp.dot(v, jnp.concatenate(ps, axis=1), preferred_element_type=_F32)
                l_row = HEAD_DIM if odd == 0 else 0
                denom = acc[l_row:l_row + 1, :] + jnp.concatenate(sink_terms, axis=1)
                tiles.append(acc / denom)
            for s in range(group // 2):
                sl = slice(LANES * s, LANES * (s + 1))
                tile = jnp.where(row < HEAD_DIM, tiles[0][:, sl], tiles[1][:, sl])
                cols = slice(LANES * (g * (group // 2) + s), LANES * (g * (group // 2) + s + 1))
                out_ref[rows, cols] = (tile.T * g_ref[rows, cols]).astype(out_ref.dtype)


def _swa_attention(aq, ak2, avt, gates, sinks, batch, seq):
    tm = PROJ_ROWS
    ns = seq // tm
    n = batch * seq
    per = tm // BLOCK
    sink_b = jnp.broadcast_to((sinks.astype(_F32) * LOG2E)[:, None], (SWA_Q_HEADS, LANES))
    cur = lambda w: pl.BlockSpec((tm, w), lambda b, i: (b * ns + i, 0))
    return pl.pallas_call(
        _swa_kernel,
        grid=(batch, ns),
        in_specs=[cur(512), cur(256),
                  pl.BlockSpec((BLOCK, 256), lambda b, i: (b * ns * per + jnp.maximum(i * per - 1, 0), 0)),
                  pl.BlockSpec((1, 512, tm), lambda b, i: (b * ns + i, 0, 0)),
                  pl.BlockSpec((1, 512, BLOCK), lambda b, i: (b * ns + jnp.maximum(i - 1, 0), 0, per - 1)),
                  cur(512),
                  pl.BlockSpec((SWA_Q_HEADS, LANES), lambda b, i: (0, 0))],
        out_specs=cur(512),
        out_shape=jax.ShapeDtypeStruct((n, 512), _BF16),
        compiler_params=pltpu.CompilerParams(dimension_semantics=("arbitrary", "arbitrary"),
                                             vmem_limit_bytes=VMEM_LIMIT),
        name="swa_attention",
    )(aq, ak2, ak2, avt, avt, gates, sink_b)


_DSA_HEAD_ORDER = (0, 2, 4, 6, 1, 3, 5, 7)


def _key_to_float(k):
    return lax.bitcast_convert_type(jnp.where(k < 0, k ^ 0x7FFFFFFF, k), _F32)


def _key16_to_float(k):
    return lax.bitcast_convert_type(jnp.where(k <= 0, k ^ 0x7FFF ^ jnp.where(k == 0, -1, 0), k) << 16, _F32)


def _midpoint(a, b):
    return (a >> 1) + (b >> 1) + (a & b & 1)


def _tree_sum(parts, chains=8):
    accs = list(parts[:chains])
    for j in range(chains, len(parts)):
        accs[j % chains] = accs[j % chains] + parts[j]
    while len(accs) > 1:
        accs = [accs[2 * j] + accs[2 * j + 1] for j in range(len(accs) // 2)]
    return accs[0]


def _dsa_kernel(iq_ref, bq_ref, iwt_ref, g_ref, bk2_ref, ik2_ref, bvt_ref, out_ref,
                sc_ref, sc16_ref, iqm_ref, bqm_ref, ka_ref, kb_ref, kmax_ref, acc_e_ref, acc_o_ref, *, n_sel):
    tq, kc = BLOCK, KEY_CHUNK
    nh, half = DSA_Q_HEADS, DSA_Q_HEADS // 2
    i = pl.program_id(1)
    q0 = i * tq
    lane = lax.broadcasted_iota(_I32, (tq, LANES), 1)
    lo_half = (lane & (HEAD_DIM // 2)) == 0
    one_even, one_odd = HEAD_DIM // 2, 0

    kh = kc // 2
    rem = (q0 + tq) % kc
    n_whole = (q0 + tq) // kc + jnp.where(rem > kh, 1, 0)
    has_half = (rem > 0) & (rem <= kh)
    k_half = n_whole * kc
    n_rows = k_half + jnp.where(has_half, kh, 0)

    def over_keys(body, carry):
        def two(c, cr):
            k0 = pl.multiple_of(c * 2 * kc, 2 * kc)
            return body(k0 + kc, kc, body(k0, kc, cr))

        carry = lax.fori_loop(0, n_whole // 2, two, carry)
        carry = lax.cond(n_whole % 2 == 1, lambda cr: body(pl.multiple_of((n_whole - 1) * kc, kc), kc, cr),
                         lambda cr: cr, carry)
        return lax.cond(has_half, lambda cr: body(pl.multiple_of(k_half, kh), kh, cr), lambda cr: cr, carry)

    @pl.when(i == 0)
    def _():
        klane = lax.broadcasted_iota(_I32, (kc, LANES), 1)
        klo = (klane & (HEAD_DIM // 2)) == 0

        def body(c, mx):
            k0 = pl.multiple_of(c * kc, kc)
            k = bk2_ref[pl.ds(k0, kc), :].astype(_F32)
            ka_ref[pl.ds(k0, kc), :] = jnp.where(klane == one_even, 1.0, k).astype(_BF16)
            kb_ref[pl.ds(k0, kc), :] = jnp.where(klane == one_odd, 1.0, k).astype(_BF16)
            return jnp.maximum(mx, jnp.sum(jnp.where(klo, k * k, 0.0), axis=1, keepdims=True))

        mx = lax.fori_loop(0, sc_ref.shape[0] // kc, body, jnp.zeros((kc, 1), _F32))
        kmax_ref[...] = jnp.broadcast_to(jnp.sqrt(jnp.max(mx, axis=0, keepdims=True)), kmax_ref.shape)

    def masked_half(ref, h):
        slab = ref[:, LANES * (h // 2):LANES * (h // 2 + 1)].astype(_F32)
        return jnp.where(lo_half if h % 2 == 0 else ~lo_half, slab, 0.0)

    for h in range(IDX_HEADS):
        iqm_ref[h * tq:(h + 1) * tq, :] = masked_half(iq_ref, h).astype(_BF16)
    kmax = kmax_ref[0:1, 0:1]
    shift_max = jnp.zeros((tq, 1), _F32)
    for p, h in enumerate(_DSA_HEAD_ORDER):
        qm = masked_half(bq_ref, h)
        shift = jnp.sqrt(jnp.sum(qm * qm, axis=1, keepdims=True)) * kmax
        shift_max = jnp.maximum(shift_max, shift)
        one_lane = one_even if h % 2 == 0 else one_odd
        bqm_ref[p * tq:(p + 1) * tq, :] = jnp.where(lane == one_lane, -shift, qm).astype(_BF16)
    bounded = jnp.max(shift_max) <= SHIFT_LIMIT

    w = [iwt_ref[h:h + 1, :] for h in range(IDX_HEADS)]
    def score_piece(k0, rows, diagonal):
        z = lax.dot_general(ik2_ref[pl.ds(k0, rows), :], iqm_ref[...], _NT, preferred_element_type=_F32)
        sc = w[0] * jnp.maximum(z[:, 0:LANES], 0.0)
        for h in range(1, IDX_HEADS):
            sc = sc + w[h] * jnp.maximum(z[:, LANES * h:LANES * (h + 1)], 0.0)
        if diagonal:
            kidx = k0 + lax.broadcasted_iota(_I32, (rows, LANES), 0)
            qidx = q0 + lax.broadcasted_iota(_I32, (rows, LANES), 1)
            sc = jnp.where(kidx <= qidx, sc, -jnp.inf)
        sc_ref[pl.ds(k0, rows), :] = sc
        top = lax.bitcast_convert_type(sc, _I32) & jnp.int32(-65536)
        sc16_ref[pl.ds(k0, rows), :] = lax.bitcast_convert_type(top, _F32).astype(_BF16)

    def score_body(c, carry):
        score_piece(pl.multiple_of(c * kc, kc), kc, False)
        return carry

    lax.fori_loop(0, n_whole - jnp.where(has_half, 0, 1), score_body, 0)

    @pl.when(has_half)
    def _():
        score_piece(pl.multiple_of(k_half, kh), kh, True)

    @pl.when(jnp.logical_not(has_half))
    def _():
        score_piece(pl.multiple_of((n_whole - 1) * kc, kc), kc, True)

    def count_ge(t):
        def body(k0, rows, acc):
            for h in range(0, rows, 512):
                ind = jnp.where(sc_ref[pl.ds(k0 + h, 512), :] >= t, 1, 0)
                acc = acc + jnp.sum(ind.reshape(64, 8, LANES), axis=0)
            return acc
        return jnp.sum(over_keys(body, jnp.zeros((8, LANES), _I32)), axis=0, keepdims=True)

    def count_ge16(t):
        t = t.astype(_BF16)

        def body(k0, rows, acc):
            d = pltpu.bitcast(sc16_ref[pl.ds(k0, rows), :] - t, _I32)
            neg = lax.shift_right_logical(d, 15) & 0x00010001
            return acc + jnp.sum(neg.reshape(rows // 16, 8, LANES), axis=0)
        acc = over_keys(body, jnp.zeros((8, LANES), _I32))
        below = jnp.sum((acc & 0xFFFF) + lax.shift_right_logical(acc, 16), axis=0, keepdims=True)
        return n_rows - below

    def search_pass(count, to_thr, st):
        lo, hi, clo, chi, t, thr = st
        t_up, t_dn = _midpoint(t, hi), _midpoint(lo, t)
        thr_up, thr_dn = to_thr(t_up), to_thr(t_dn)
        cnt = count(thr)
        ok = cnt >= n_sel
        return (jnp.where(ok, t, lo), jnp.where(ok, hi, t), jnp.where(ok, cnt, clo), jnp.where(ok, chi, cnt),
                jnp.where(ok, t_up, t_dn), jnp.where(ok, thr_up, thr_dn))

    vec = lambda v: jnp.full((1, LANES), v, _I32)
    st = (vec(KEY_NEG_INF >> 16), vec(1 << 15), vec(0) + n_rows, vec(0), vec(0), _key16_to_float(vec(0)))
    st = lax.fori_loop(0, 16, lambda j, s: search_pass(count_ge16, _key16_to_float, s), st)
    hint_lo = jnp.maximum(st[0] << 16, KEY_NEG_INF)
    hint_hi = jnp.where(st[1] >= (1 << 15), INT_MAX, st[1] << 16)
    hint_hi = jnp.where((hint_hi > 0) & (hint_hi < KEY_MIN_NORMAL), KEY_MIN_NORMAL, hint_hi)
    c1 = count_ge(_key_to_float(hint_lo))
    ok1 = c1 >= n_sel
    lo, clo = jnp.where(ok1, hint_lo, KEY_NEG_INF), jnp.where(ok1, c1, n_rows)
    hi, chi = jnp.where(ok1, INT_MAX, hint_lo), jnp.where(ok1, 0, c1)
    c2 = count_ge(_key_to_float(hint_hi))
    up2 = (c2 >= n_sel) & (hint_hi > lo)
    dn2 = (c2 < n_sel) & (hint_hi < hi)
    lo, clo = jnp.where(up2, hint_hi, lo), jnp.where(up2, c2, clo)
    hi, chi = jnp.where(dn2, hint_hi, hi), jnp.where(dn2, c2, chi)

    def unsettled(lo, hi, clo):
        return ~((clo == n_sel) | (hi - 1 <= lo) | ((lo >= 0) & (hi <= KEY_MIN_NORMAL)))

    def fine_group(c):
        s = c[1:7]
        for _ in range(SEARCH_GROUP):
            s = search_pass(count_ge, _key_to_float, s)
        return (c[0] + 1,) + s + (jnp.max(jnp.where(unsettled(*s[:3]), 1, 0)),)

    t0 = _midpoint(lo, hi)
    init = (jnp.int32(0), lo, hi, clo, chi, t0, _key_to_float(t0),
            jnp.max(jnp.where(unsettled(lo, hi, clo), 1, 0)))
    _, lo, hi, clo, chi, _, _, _ = lax.while_loop(lambda c: (c[7] > 0) & (c[0] < 16), fine_group, init)
    t_lo = _key_to_float(lo)
    t_hi = _key_to_float(hi)

    need = (clo > n_sel) & (lo > KEY_NEG_INF)

    @pl.when(jnp.max(jnp.where(need, 1, 0)) > 0)
    def _():
        tk = TIE_CHUNK
        room = (n_sel - chi).astype(_F32)
        before = (lax.broadcasted_iota(_I32, (tk, tk), 0) > lax.broadcasted_iota(_I32, (tk, tk), 1))
        before = jnp.where(before, 1.0, 0.0).astype(_BF16)

        def body(k0, rows, seen):
            for h in range(rows // tk):
                x = sc_ref[pl.ds(k0 + h * tk, tk), :]
                tie = (x >= t_lo) & ~(x >= t_hi)
                tf = jnp.where(tie, 1.0, 0.0)
                rank = seen + jnp.dot(before, tf.astype(_BF16), preferred_element_type=_F32)
                sc_ref[pl.ds(k0 + h * tk, tk), :] = jnp.where(tie & (rank >= room), -jnp.inf, x)
                seen = seen + jnp.sum(tf, axis=0, keepdims=True)
            return seen

        over_keys(body, jnp.zeros((1, LANES), _F32))

    thr = jnp.maximum(t_lo, F32_LOWEST)

    acc_e_ref[...] = jnp.zeros_like(acc_e_ref)
    acc_o_ref[...] = jnp.zeros_like(acc_o_ref)
    vt_rows = bvt_ref.shape[2]

    def masked_logits(k0, rows):
        sel = sc_ref[pl.ds(k0, rows), :] >= thr
        lge = lax.dot_general(ka_ref[pl.ds(k0, rows), :], bqm_ref[0:half * tq, :], _NT,
                              preferred_element_type=_F32)
        lgo = lax.dot_general(kb_ref[pl.ds(k0, rows), :], bqm_ref[half * tq:nh * tq, :], _NT,
                              preferred_element_type=_F32)
        tiles = lambda lg: [jnp.where(sel, lg[:, LANES * j:LANES * (j + 1)], NEG_BIG) for j in range(half)]
        vt = jnp.concatenate([bvt_ref[k0 // vt_rows + j] for j in range(rows // vt_rows)], axis=1)
        return ((acc_e_ref, tiles(lge), vt), (acc_o_ref, tiles(lgo), vt))

    @pl.when(bounded)
    def _():
        def body(k0, rows, carry):
            for ref, tiles, v in masked_logits(k0, rows):
                p = jnp.concatenate([jnp.exp2(t).astype(_BF16) for t in tiles], axis=1)
                ref[...] += jnp.dot(v, p, preferred_element_type=_F32)
            return carry
        over_keys(body, jnp.int32(0))

    @pl.when(jnp.logical_not(bounded))
    def _():
        def body(k0, rows, m):
            m_out = []
            for g, (ref, tiles, v) in enumerate(masked_logits(k0, rows)):
                ps, alphas = [], []
                for j, t in enumerate(tiles):
                    sl = slice(LANES * (g * half + j), LANES * (g * half + j + 1))
                    m_new = jnp.maximum(m[:, sl], jnp.max(t, axis=0, keepdims=True))
                    ps.append(jnp.exp2(t - m_new).astype(_BF16))
                    alphas.append(jnp.exp2(m[:, sl] - m_new))
                    m_out.append(m_new)
                ref[...] = ref[...] * jnp.concatenate(alphas, axis=1) + jnp.dot(
                    v, jnp.concatenate(ps, axis=1), preferred_element_type=_F32)
            return jnp.concatenate(m_out, axis=1)
        over_keys(body, jnp.full((1, nh * LANES), NEG_BIG, _F32))

    for j in range(half):
        sl = slice(LANES * j, LANES * (j + 1))
        even = acc_e_ref[0:HEAD_DIM, sl] / acc_e_ref[HEAD_DIM:HEAD_DIM + 1, sl]
        odd = acc_o_ref[0:HEAD_DIM, sl] / acc_o_ref[HEAD_DIM:HEAD_DIM + 1, sl]
        tile = jnp.concatenate([even, odd], axis=0)
        out_ref[:, sl] = (tile.T * g_ref[:, sl]).astype(out_ref.dtype)


def _dsa_attention(iq, bq, iwt, gates, ki2, bvt, batch, seq):
    nb = seq // BLOCK
    n = batch * seq
    nvt = seq // PROJ_ROWS
    n_sel = min(TOPK_MAX, seq // 4)
    blk = lambda w, col=0: pl.BlockSpec((BLOCK, w), lambda b, i: (b * nb + i, col))
    return pl.pallas_call(
        functools.partial(_dsa_kernel, n_sel=n_sel),
        grid=(batch, nb),
        in_specs=[blk(256), blk(512),
                  pl.BlockSpec((16, BLOCK), lambda b, i: (0, b * nb + i)),
                  blk(512, 1),
                  pl.BlockSpec((seq, LANES), lambda b, i: (b, 0)),
                  pl.BlockSpec((seq, LANES), lambda b, i: (b, 1)),
                  pl.BlockSpec((nvt, VT_ROWS, PROJ_ROWS), lambda b, i: (b, 0, 0))],
        out_specs=blk(512),
        out_shape=jax.ShapeDtypeStruct((n, 512), _BF16),
        scratch_shapes=[pltpu.VMEM((seq, LANES), _F32),
                        pltpu.VMEM((seq, LANES), _BF16),
                        pltpu.VMEM((IDX_HEADS * BLOCK, LANES), _BF16),
                        pltpu.VMEM((DSA_Q_HEADS * BLOCK, LANES), _BF16),
                        pltpu.VMEM((seq, LANES), _BF16),
                        pltpu.VMEM((seq, LANES), _BF16),
                        pltpu.VMEM((8, LANES), _F32),
                        pltpu.VMEM((VT_ROWS, 512), _F32),
                        pltpu.VMEM((VT_ROWS, 512), _F32)],
        compiler_params=pltpu.CompilerParams(dimension_semantics=("arbitrary", "arbitrary"),
                                             vmem_limit_bytes=VMEM_LIMIT),
        name="dsa_attention",
    )(iq, bq, iwt, gates, ki2, ki2, bvt)


def _out_kernel(a_ref, b_ref, x_ref, w_ref, bo_ref, gain_ref, bias_ref, out_ref, *, alpha):
    half = a_ref.shape[1]
    y = jnp.dot(a_ref[...], w_ref[0:half, :], preferred_element_type=_F32)
    y = y + jnp.dot(b_ref[...], w_ref[half:2 * half, :], preferred_element_type=_F32)
    z = alpha * x_ref[...] + (y + bo_ref[...])
    mu = jnp.mean(z, axis=-1, keepdims=True)
    zc = z - mu
    var = jnp.mean(zc * zc, axis=-1, keepdims=True)
    out_ref[...] = zc * lax.rsqrt(var + LN_EPS) * gain_ref[...] + bias_ref[...]


def _out_projection(a, b, x2, w_out, b_out, gain, bias, alpha):
    n, d = x2.shape
    tm = PROJ_ROWS
    row = lambda w: pl.BlockSpec((tm, w), lambda i: (i, 0))
    full = lambda a_: pl.BlockSpec(a_.shape, lambda i: (0,) * a_.ndim)
    consts = (w_out.astype(_BF16), b_out[None, :], gain[None, :], bias[None, :])
    return pl.pallas_call(
        functools.partial(_out_kernel, alpha=alpha),
        grid=(n // tm,),
        in_specs=[row(a.shape[1]), row(b.shape[1]), row(d)] + [full(c) for c in consts],
        out_specs=row(d),
        out_shape=jax.ShapeDtypeStruct((n, d), x2.dtype),
        compiler_params=pltpu.CompilerParams(dimension_semantics=("arbitrary",),
                                             vmem_limit_bytes=VMEM_LIMIT),
        name="out_projection",
    )(a, b, x2, *consts)


def _layer(h, pos2, w_in, b_in, sinks, w_out, b_out, gain, bias, alpha):
    batch, seq, d = h.shape
    x2 = h.reshape(batch * seq, d)
    aq, ak2, bq, iq, ki2, gates, bvt, avt, iwt = _projection(x2, pos2, w_in, b_in)
    a = _swa_attention(aq, ak2, avt, gates, sinks, batch, seq)
    b = _dsa_attention(iq, bq, iwt, gates, ki2, bvt, batch, seq)
    return _out_projection(a, b, x2, w_out, b_out, gain, bias, alpha).reshape(batch, seq, d)


def kernel(x, positions, w_in, b_in, swa_sinks, w_out, b_out, ln_gain, ln_bias):
    batch, seq, d = x.shape
    depth = w_in.shape[0]
    assert d == 1024 and seq % KEY_CHUNK == 0 and (batch * seq) % PROJ_ROWS == 0
    assert KEY_CHUNK % PROJ_ROWS == 0 and KEY_CHUNK % TIE_CHUNK == 0
    alpha = (2.0 * depth) ** 0.25
    pos2 = positions.reshape(batch * seq, 1)
    h = x
    for layer in range(depth):
        h = _layer(h, pos2, w_in[layer], b_in[layer], swa_sinks[layer], w_out[layer], b_out[layer],
                   ln_gain[layer], ln_bias[layer], alpha)
    return h
```

```python
import functools

import numpy as np
import jax
import jax.numpy as jnp
from jax import lax
from jax.experimental import pallas as pl
from jax.experimental.pallas import tpu as pltpu

HEAD_DIM = 64
SWA_Q_HEADS = 8
SWA_KV_HEADS = 2
DSA_Q_HEADS = 8
IDX_HEADS = 4
IDX_DIM = 64
WINDOW = 128
BLOCK = 128
TOPK_MAX = 256
ROPE_THETA = 10000.0
LN_EPS = 1e-5

LANES = 128
PROJ_ROWS = 512
KEY_CHUNK = 1024
TIE_CHUNK = 512
VMEM_LIMIT = 56 * 1024 * 1024

LOG2E = 1.4426950408889634
NEG_BIG = -1e30
SHIFT_LIMIT = 60.0
F32_LOWEST = -3.4028234663852886e38
INT_MAX = 2 ** 31 - 1
KEY_NEG_INF = 0x807FFFFF - 2 ** 32
KEY_MIN_NORMAL = 0x00800000
SEARCH_GROUP = 3

_F32 = jnp.float32
_BF16 = jnp.bfloat16
_I32 = jnp.int32
_NT = (((1,), (1,)), ((), ()))


def _split_columns(m):
    swa_w = SWA_Q_HEADS * HEAD_DIM
    kv_w = SWA_KV_HEADS * HEAD_DIM
    dsa_w = DSA_Q_HEADS * HEAD_DIM
    sizes = (swa_w, kv_w, kv_w, swa_w, dsa_w, HEAD_DIM, HEAD_DIM, dsa_w,
             IDX_HEADS * IDX_DIM, IDX_DIM, IDX_HEADS)
    o = np.cumsum((0,) + sizes)
    aq, ak, av, ag, bq, bk, bv, bg, iq, ik, iw = [m[:, o[i]:o[i + 1]] for i in range(len(sizes))]

    def dup(c):
        return jnp.concatenate([c[:, j:j + HEAD_DIM] for j in range(0, c.shape[1], HEAD_DIM) for _ in (0, 1)],
                               axis=1)

    def pairs(c):
        r = c.reshape(c.shape[0], -1, 2, 2, HEAD_DIM // 2)
        return jnp.swapaxes(r, 2, 3).reshape(c.shape[0], -1)

    roped = pairs(jnp.concatenate([aq, dup(ak), bq, iq, dup(bk), dup(ik)], axis=1))
    return jnp.concatenate([roped, ag, bg], axis=1), (bv, av[:, :HEAD_DIM], av[:, HEAD_DIM:]), iw


def _proj_kernel(x_ref, pos_ref, inv_ref, sgn_ref, wm_ref, bm_ref,
                 wvt_ref, bvt_ref, wiw_ref, biw_ref,
                 aq_ref, ak2_ref, bq_ref, iq_ref, ki2_ref, g_ref, bvt_out_ref, avt_out_ref, iwt_ref):
    xb = x_ref[...].astype(_BF16)
    ang = pos_ref[...].astype(_F32) * inv_ref[...]
    cos = jnp.cos(ang)
    sin = jnp.sin(ang) * sgn_ref[...]

    q_scale = 0.125 * LOG2E
    roped_dst = ((aq_ref, 0, q_scale), (aq_ref, 256, q_scale), (ak2_ref, 0, 1.0), (bq_ref, 0, q_scale),
                 (bq_ref, 256, q_scale), (iq_ref, 0, 1.0), (ki2_ref, 0, 1.0))
    for g, (dst, off, scale) in enumerate(roped_dst):
        c0 = 256 * g
        hm = jnp.dot(xb, wm_ref[:, c0:c0 + 256], preferred_element_type=_F32) + bm_ref[:, c0:c0 + 256]
        for s in range(2):
            sl = slice(LANES * s, LANES * (s + 1))
            o = hm[:, sl] * cos + pltpu.roll(hm[:, sl], LANES // 2, axis=1) * sin
            if scale != 1.0:
                o = o * scale
            dst[:, off + LANES * s:off + LANES * (s + 1)] = o.astype(dst.dtype)

    n_roped = 256 * len(roped_dst)
    for g in range(4):
        c0 = n_roped + 256 * g
        h = jnp.dot(xb, wm_ref[:, c0:c0 + 256], preferred_element_type=_F32) + bm_ref[:, c0:c0 + 256]
        g_ref[:, 256 * g:256 * (g + 1)] = h * (1.0 / (1.0 + jnp.exp(-h)))

    vt = lax.dot_general(wvt_ref[...], xb, _NT, preferred_element_type=_F32) + bvt_ref[...]
    n_b = bvt_out_ref.shape[1]
    bvt_out_ref[0] = vt[0:n_b].astype(bvt_out_ref.dtype)
    avt_out_ref[0] = vt[n_b:].astype(avt_out_ref.dtype)
    iwt = lax.dot_general(wiw_ref[...], xb, _NT, preferred_element_type=_F32) + biw_ref[...]
    iwt_ref[...] = iwt * (IDX_HEADS ** -0.5 * IDX_DIM ** -0.5)


def _projection(x2, pos2, w_in, b_in):
    n, d = x2.shape
    tm = PROJ_ROWS
    wm, wvs, wiw = _split_columns(w_in)
    bm, bvs, biw = _split_columns(b_in[None, :])
    wm = wm.astype(_BF16)
    zpad = jnp.zeros((HEAD_DIM, d), w_in.dtype)
    one_row = jnp.zeros((HEAD_DIM,), b_in.dtype).at[0].set(1.0)
    wvt = jnp.concatenate([p for wv in wvs for p in (wv.T, zpad, zpad, wv.T)], axis=0).astype(_BF16)
    bvt = jnp.concatenate([p for bv in bvs for p in (bv[0], one_row, one_row, bv[0])])[:, None]
    wiw_t = jnp.concatenate([wiw.T, jnp.zeros((16 - IDX_HEADS, d), w_in.dtype)], axis=0).astype(_BF16)
    biw_t = jnp.concatenate([biw[0], jnp.zeros((16 - IDX_HEADS,), b_in.dtype)])[:, None]
    half = HEAD_DIM // 2
    inv = ROPE_THETA ** (-jnp.arange(0, HEAD_DIM, 2, dtype=_F32) / HEAD_DIM)
    inv128 = jnp.tile(inv, LANES // half)[None, :]
    sgn128 = jnp.concatenate([-jnp.ones((LANES // 2,), _F32), jnp.ones((LANES // 2,), _F32)])[None, :]

    row = lambda w: pl.BlockSpec((tm, w), lambda i: (i, 0))
    full = lambda a: pl.BlockSpec(a.shape, lambda i: (0,) * a.ndim)
    out_shape = (
        jax.ShapeDtypeStruct((n, 512), _BF16),
        jax.ShapeDtypeStruct((n, 256), _BF16),
        jax.ShapeDtypeStruct((n, 512), _BF16),
        jax.ShapeDtypeStruct((n, 256), _BF16),
        jax.ShapeDtypeStruct((n, 256), _BF16),
        jax.ShapeDtypeStruct((n, 1024), _F32),
        jax.ShapeDtypeStruct((n // tm, 256, tm), _BF16),
        jax.ShapeDtypeStruct((n // tm, 512, tm), _BF16),
        jax.ShapeDtypeStruct((16, n), _F32),
    )
    out_specs = (row(512), row(256), row(512), row(256), row(256), row(1024),
                 pl.BlockSpec((1, 256, tm), lambda i: (i, 0, 0)),
                 pl.BlockSpec((1, 512, tm), lambda i: (i, 0, 0)),
                 pl.BlockSpec((16, tm), lambda i: (0, i)))
    consts = (inv128, sgn128, wm, bm, wvt, bvt, wiw_t, biw_t)
    return pl.pallas_call(
        _proj_kernel,
        grid=(n // tm,),
        in_specs=[row(d), row(1)] + [full(a) for a in consts],
        out_specs=out_specs,
        out_shape=out_shape,
        compiler_params=pltpu.CompilerParams(dimension_semantics=("arbitrary",),
                                             vmem_limit_bytes=VMEM_LIMIT),
        name="in_projection",
    )(x2, pos2, *consts)


def _swa_kernel(q_ref, kc_ref, kp_ref, vc_ref, vp_ref, g_ref, sink_ref, out_ref):
    i = pl.program_id(1)
    tq = BLOCK
    n_sub = q_ref.shape[0] // tq
    group = SWA_Q_HEADS // SWA_KV_HEADS
    lane = lax.broadcasted_iota(_I32, (tq, LANES), 1)
    lo_half = (lane & (HEAD_DIM // 2)) == 0
    r = lax.broadcasted_iota(_I32, (2 * tq, LANES), 0)
    c = lax.broadcasted_iota(_I32, (2 * tq, LANES), 1)
    in_window = (c < r) & (r <= c + WINDOW)
    row = lax.broadcasted_iota(_I32, (LANES, LANES), 0)

    kwin = jnp.concatenate([kp_ref[...], kc_ref[...]], axis=0)
    vwin = jnp.concatenate([vp_ref[0], vc_ref[0]], axis=1)
    for g in range(SWA_KV_HEADS):
        k = kwin[:, LANES * g:LANES * (g + 1)]
        for j in range(n_sub):
            valid = in_window if j > 0 else in_window & ((r >= tq) | (i > 0))
            rows = slice(tq * j, tq * (j + 1))
            win = slice(tq * j, tq * (j + 2))
            tiles = []
            for odd in (0, 1):
                qs = []
                for s in range(group // 2):
                    slab = q_ref[rows, LANES * (g * (group // 2) + s):LANES * (g * (group // 2) + s + 1)]
                    qs.append(jnp.where(lo_half if odd == 0 else ~lo_half, slab.astype(_F32), 0.0).astype(_BF16))
                qop = jnp.concatenate(qs, axis=0)
                logits = lax.dot_general(k[win], qop, _NT, preferred_element_type=_F32)
                heads = [group * g + 2 * s + odd for s in range(group // 2)]
                ps, sink_terms = [], []
                for s, h in enumerate(heads):
                    lg = jnp.where(valid, logits[:, LANES * s:LANES * (s + 1)], -jnp.inf)
                    sink = sink_ref[h:h + 1, :]
                    m = jnp.maximum(jnp.max(lg, axis=0, keepdims=True), sink)
                    ps.append(jnp.exp2(lg - m).astype(_BF16))
                    sink_terms.append(jnp.exp2(sink - m))
                v = vwin[LANES * (2 * g + odd):LANES * (2 * g + odd + 1), win]
                acc = jnp.dot(v, jnp.concatenate(ps, axis=1), preferred_element_type=_F32)
                l_row = HEAD_DIM if odd == 0 else 0
                denom = acc[l_row:l_row + 1, :] + jnp.concatenate(sink_terms, axis=1)
                tiles.append(acc / denom)
            for s in range(group // 2):
                sl = slice(LANES * s, LANES * (s + 1))
                tile = jnp.where(row < HEAD_DIM, tiles[0][:, sl], tiles[1][:, sl])
                cols = slice(LANES * (g * (group // 2) + s), LANES * (g * (group // 2) + s + 1))
                out_ref[rows, cols] = (tile.T * g_ref[rows, cols]).astype(out_ref.dtype)


def _swa_attention(aq, ak2, avt, gates, sinks, batch, seq):
    tm = PROJ_ROWS
    ns = seq // tm
    n = batch * seq
    per = tm // BLOCK
    sink_b = jnp.broadcast_to((sinks.astype(_F32) * LOG2E)[:, None], (SWA_Q_HEADS, LANES))
    cur = lambda w: pl.BlockSpec((tm, w), lambda b, i: (b * ns + i, 0))
    return pl.pallas_call(
        _swa_kernel,
        grid=(batch, ns),
        in_specs=[cur(512), cur(256),
                  pl.BlockSpec((BLOCK, 256), lambda b, i: (b * ns * per + jnp.maximum(i * per - 1, 0), 0)),
                  pl.BlockSpec((1, 512, tm), lambda b, i: (b * ns + i, 0, 0)),
                  pl.BlockSpec((1, 512, BLOCK), lambda b, i: (b * ns + jnp.maximum(i - 1, 0), 0, per - 1)),
                  cur(512),
                  pl.BlockSpec((SWA_Q_HEADS, LANES), lambda b, i: (0, 0))],
        out_specs=cur(512),
        out_shape=jax.ShapeDtypeStruct((n, 512), _BF16),
        compiler_params=pltpu.CompilerParams(dimension_semantics=("arbitrary", "arbitrary"),
                                             vmem_limit_bytes=VMEM_LIMIT),
        name="swa_attention",
    )(aq, ak2, ak2, avt, avt, gates, sink_b)


_DSA_HEAD_ORDER = (0, 2, 4, 6, 1, 3, 5, 7)


def _key_to_float(k):
    return lax.bitcast_convert_type(jnp.where(k < 0, k ^ 0x7FFFFFFF, k), _F32)


def _key16_to_float(k):
    return lax.bitcast_convert_type(jnp.where(k <= 0, k ^ 0x7FFF ^ jnp.where(k == 0, -1, 0), k) << 16, _F32)


def _midpoint(a, b):
    return (a >> 1) + (b >> 1) + (a & b & 1)


def _tree_sum(parts, chains=8):
    accs = list(parts[:chains])
    for j in range(chains, len(parts)):
        accs[j % chains] = accs[j % chains] + parts[j]
    while len(accs) > 1:
        accs = [accs[2 * j] + accs[2 * j + 1] for j in range(len(accs) // 2)]
    return accs[0]


def _dsa_kernel(iq_ref, bq_ref, iwt_ref, g_ref, bk2_ref, ik2_ref, bvt_ref, out_ref,
                sc_ref, sc16_ref, iqm_ref, bqm_ref, ka_ref, kb_ref, kmax_ref, acc_e_ref, acc_o_ref, *, n_sel):
    tq, kc = BLOCK, KEY_CHUNK
    nh, half = DSA_Q_HEADS, DSA_Q_HEADS // 2
    i = pl.program_id(1)
    q0 = i * tq
    lane = lax.broadcasted_iota(_I32, (tq, LANES), 1)
    lo_half = (lane & (HEAD_DIM // 2)) == 0
    one_even, one_odd = HEAD_DIM // 2, 0

    kh = kc // 2
    rem = (q0 + tq) % kc
    n_whole = (q0 + tq) // kc + jnp.where(rem > kh, 1, 0)
    has_half = (rem > 0) & (rem <= kh)
    k_half = n_whole * kc
    n_rows = k_half + jnp.where(has_half, kh, 0)

    def over_keys(body, carry):
        def two(c, cr):
            k0 = pl.multiple_of(c * 2 * kc, 2 * kc)
            return body(k0 + kc, kc, body(k0, kc, cr))

        carry = lax.fori_loop(0, n_whole // 2, two, carry)
        carry = lax.cond(n_whole % 2 == 1, lambda cr: body(pl.multiple_of((n_whole - 1) * kc, kc), kc, cr),
                         lambda cr: cr, carry)
        return lax.cond(has_half, lambda cr: body(pl.multiple_of(k_half, kh), kh, cr), lambda cr: cr, carry)

    @pl.when(i == 0)
    def _():
        klane = lax.broadcasted_iota(_I32, (kc, LANES), 1)
        klo = (klane & (HEAD_DIM // 2)) == 0

        def body(c, mx):
            k0 = pl.multiple_of(c * kc, kc)
            k = bk2_ref[pl.ds(k0, kc), :].astype(_F32)
            ka_ref[pl.ds(k0, kc), :] = jnp.where(klane == one_even, 1.0, k).astype(_BF16)
            kb_ref[pl.ds(k0, kc), :] = jnp.where(klane == one_odd, 1.0, k).astype(_BF16)
            return jnp.maximum(mx, jnp.sum(jnp.where(klo, k * k, 0.0), axis=1, keepdims=True))

        mx = lax.fori_loop(0, sc_ref.shape[0] // kc, body, jnp.zeros((kc, 1), _F32))
        kmax_ref[...] = jnp.broadcast_to(jnp.sqrt(jnp.max(mx, axis=0, keepdims=True)), kmax_ref.shape)

    def masked_half(ref, h):
        slab = ref[:, LANES * (h // 2):LANES * (h // 2 + 1)].astype(_F32)
        return jnp.where(lo_half if h % 2 == 0 else ~lo_half, slab, 0.0)

    for h in range(IDX_HEADS):
        iqm_ref[h * tq:(h + 1) * tq, :] = masked_half(iq_ref, h).astype(_BF16)
    kmax = kmax_ref[0:1, 0:1]
    shift_max = jnp.zeros((tq, 1), _F32)
    for p, h in enumerate(_DSA_HEAD_ORDER):
        qm = masked_half(bq_ref, h)
        shift = jnp.sqrt(jnp.sum(qm * qm, axis=1, keepdims=True)) * kmax
        shift_max = jnp.maximum(shift_max, shift)
        one_lane = one_even if h % 2 == 0 else one_odd
        bqm_ref[p * tq:(p + 1) * tq, :] = jnp.where(lane == one_lane, -shift, qm).astype(_BF16)
    bounded = jnp.max(shift_max) <= SHIFT_LIMIT

    w = [iwt_ref[h:h + 1, :] for h in range(IDX_HEADS)]
    def score_piece(k0, rows, diagonal):
        z = lax.dot_general(ik2_ref[pl.ds(k0, rows), :], iqm_ref[...], _NT, preferred_element_type=_F32)
        sc = w[0] * jnp.maximum(z[:, 0:LANES], 0.0)
        for h in range(1, IDX_HEADS):
            sc = sc + w[h] * jnp.maximum(z[:, LANES * h:LANES * (h + 1)], 0.0)
        if diagonal:
            kidx = k0 + lax.broadcasted_iota(_I32, (rows, LANES), 0)
            qidx = q0 + lax.broadcasted_iota(_I32, (rows, LANES), 1)
            sc = jnp.where(kidx <= qidx, sc, -jnp.inf)
        sc_ref[pl.ds(k0, rows), :] = sc
        top = lax.bitcast_convert_type(sc, _I32) & jnp.int32(-65536)
        sc16_ref[pl.ds(k0, rows), :] = lax.bitcast_convert_type(top, _F32).astype(_BF16)

    def score_two(c, carry):
        k0 = pl.multiple_of(c * 2 * kc, 2 * kc)
        score_piece(k0, kc, False)
        score_piece(k0 + kc, kc, False)
        return carry

    n_plain = n_whole - jnp.where(has_half, 0, 1)
    lax.fori_loop(0, n_plain // 2, score_two, 0)

    @pl.when(n_plain % 2 == 1)
    def _():
        score_piece(pl.multiple_of((n_plain - 1) * kc, kc), kc, False)

    @pl.when(has_half)
    def _():
        score_piece(pl.multiple_of(k_half, kh), kh, True)

    @pl.when(jnp.logical_not(has_half))
    def _():
        score_piece(pl.multiple_of((n_whole - 1) * kc, kc), kc, True)

    def count_ge(t):
        def body(k0, rows, acc):
            for h in range(0, rows, 512):
                ind = jnp.where(sc_ref[pl.ds(k0 + h, 512), :] >= t, 1, 0)
                acc = acc + jnp.sum(ind.reshape(64, 8, LANES), axis=0)
            return acc
        return jnp.sum(over_keys(body, jnp.zeros((8, LANES), _I32)), axis=0, keepdims=True)

    def count_ge16(t):
        t = t.astype(_BF16)

        def body(k0, rows, acc):
            d = pltpu.bitcast(sc16_ref[pl.ds(k0, rows), :] - t, _I32)
            neg = lax.shift_right_logical(d, 15) & 0x00010001
            return acc + jnp.sum(neg.reshape(rows // 16, 8, LANES), axis=0)
        acc = over_keys(body, jnp.zeros((8, LANES), _I32))
        below = jnp.sum((acc & 0xFFFF) + lax.shift_right_logical(acc, 16), axis=0, keepdims=True)
        return n_rows - below

    def search_pass(count, to_thr, st):
        lo, hi, clo, chi, t, thr = st
        t_up, t_dn = _midpoint(t, hi), _midpoint(lo, t)
        thr_up, thr_dn = to_thr(t_up), to_thr(t_dn)
        cnt = count(thr)
        ok = cnt >= n_sel
        return (jnp.where(ok, t, lo), jnp.where(ok, hi, t), jnp.where(ok, cnt, clo), jnp.where(ok, chi, cnt),
                jnp.where(ok, t_up, t_dn), jnp.where(ok, thr_up, thr_dn))

    vec = lambda v: jnp.full((1, LANES), v, _I32)
    st = (vec(KEY_NEG_INF >> 16), vec(1 << 15), vec(0) + n_rows, vec(0), vec(0), _key16_to_float(vec(0)))
    st = lax.fori_loop(0, 16, lambda j, s: search_pass(count_ge16, _key16_to_float, s), st)
    hint_lo = jnp.maximum(st[0] << 16, KEY_NEG_INF)
    hint_hi = jnp.where(st[1] >= (1 << 15), INT_MAX, st[1] << 16)
    hint_hi = jnp.where((hint_hi > 0) & (hint_hi < KEY_MIN_NORMAL), KEY_MIN_NORMAL, hint_hi)
    c1 = count_ge(_key_to_float(hint_lo))
    ok1 = c1 >= n_sel
    lo, clo = jnp.where(ok1, hint_lo, KEY_NEG_INF), jnp.where(ok1, c1, n_rows)
    hi, chi = jnp.where(ok1, INT_MAX, hint_lo), jnp.where(ok1, 0, c1)
    c2 = count_ge(_key_to_float(hint_hi))
    up2 = (c2 >= n_sel) & (hint_hi > lo)
    dn2 = (c2 < n_sel) & (hint_hi < hi)
    lo, clo = jnp.where(up2, hint_hi, lo), jnp.where(up2, c2, clo)
    hi, chi = jnp.where(dn2, hint_hi, hi), jnp.where(dn2, c2, chi)

    def unsettled(lo, hi, clo):
        return ~((clo == n_sel) | (hi - 1 <= lo) | ((lo >= 0) & (hi <= KEY_MIN_NORMAL)))

    def fine_group(c):
        s = c[1:7]
        for _ in range(SEARCH_GROUP):
            s = search_pass(count_ge, _key_to_float, s)
        return (c[0] + 1,) + s + (jnp.max(jnp.where(unsettled(*s[:3]), 1, 0)),)

    t0 = _midpoint(lo, hi)
    init = (jnp.int32(0), lo, hi, clo, chi, t0, _key_to_float(t0),
            jnp.max(jnp.where(unsettled(lo, hi, clo), 1, 0)))
    _, lo, hi, clo, chi, _, _, _ = lax.while_loop(lambda c: (c[7] > 0) & (c[0] < 16), fine_group, init)
    t_lo = _key_to_float(lo)
    t_hi = _key_to_float(hi)

    need = (clo > n_sel) & (lo > KEY_NEG_INF)

    @pl.when(jnp.max(jnp.where(need, 1, 0)) > 0)
    def _():
        tk = TIE_CHUNK
        room = (n_sel - chi).astype(_F32)
        before = (lax.broadcasted_iota(_I32, (tk, tk), 0) > lax.broadcasted_iota(_I32, (tk, tk), 1))
        before = jnp.where(before, 1.0, 0.0).astype(_BF16)

        def body(k0, rows, seen):
            for h in range(rows // tk):
                x = sc_ref[pl.ds(k0 + h * tk, tk), :]
                tie = (x >= t_lo) & ~(x >= t_hi)
                tf = jnp.where(tie, 1.0, 0.0)
                rank = seen + jnp.dot(before, tf.astype(_BF16), preferred_element_type=_F32)
                sc_ref[pl.ds(k0 + h * tk, tk), :] = jnp.where(tie & (rank >= room), -jnp.inf, x)
                seen = seen + jnp.sum(tf, axis=0, keepdims=True)
            return seen

        over_keys(body, jnp.zeros((1, LANES), _F32))

    thr = jnp.maximum(t_lo, F32_LOWEST)

    acc_e_ref[...] = jnp.zeros_like(acc_e_ref)
    acc_o_ref[...] = jnp.zeros_like(acc_o_ref)
    vt_rows = bvt_ref.shape[2]

    def masked_logits(k0, rows):
        sel = sc_ref[pl.ds(k0, rows), :] >= thr
        lge = lax.dot_general(ka_ref[pl.ds(k0, rows), :], bqm_ref[0:half * tq, :], _NT,
                              preferred_element_type=_F32)
        lgo = lax.dot_general(kb_ref[pl.ds(k0, rows), :], bqm_ref[half * tq:nh * tq, :], _NT,
                              preferred_element_type=_F32)
        tiles = lambda lg: [jnp.where(sel, lg[:, LANES * j:LANES * (j + 1)], NEG_BIG) for j in range(half)]
        vt = jnp.concatenate([bvt_ref[k0 // vt_rows + j] for j in range(rows // vt_rows)], axis=1)
        return ((acc_e_ref, tiles(lge), vt[0:LANES, :]), (acc_o_ref, tiles(lgo), vt[LANES:2 * LANES, :]))

    @pl.when(bounded)
    def _():
        def body(k0, rows, carry):
            for ref, tiles, v in masked_logits(k0, rows):
                p = jnp.concatenate([jnp.exp2(t).astype(_BF16) for t in tiles], axis=1)
                ref[...] += jnp.dot(v, p, preferred_element_type=_F32)
            return carry
        over_keys(body, jnp.int32(0))

    @pl.when(jnp.logical_not(bounded))
    def _():
        def body(k0, rows, m):
            m_out = []
            for g, (ref, tiles, v) in enumerate(masked_logits(k0, rows)):
                ps, alphas = [], []
                for j, t in enumerate(tiles):
                    sl = slice(LANES * (g * half + j), LANES * (g * half + j + 1))
                    m_new = jnp.maximum(m[:, sl], jnp.max(t, axis=0, keepdims=True))
                    ps.append(jnp.exp2(t - m_new).astype(_BF16))
                    alphas.append(jnp.exp2(m[:, sl] - m_new))
                    m_out.append(m_new)
                ref[...] = ref[...] * jnp.concatenate(alphas, axis=1) + jnp.dot(
                    v, jnp.concatenate(ps, axis=1), preferred_element_type=_F32)
            return jnp.concatenate(m_out, axis=1)
        over_keys(body, jnp.full((1, nh * LANES), NEG_BIG, _F32))

    row = lax.broadcasted_iota(_I32, (LANES, LANES), 0)
    for j in range(half):
        sl = slice(LANES * j, LANES * (j + 1))
        even = acc_e_ref[:, sl] / acc_e_ref[HEAD_DIM:HEAD_DIM + 1, sl]
        odd = acc_o_ref[:, sl] / acc_o_ref[0:1, sl]
        tile = jnp.where(row < HEAD_DIM, even, odd)
        out_ref[:, sl] = (tile.T * g_ref[:, sl]).astype(out_ref.dtype)


def _dsa_attention(iq, bq, iwt, gates, ki2, bvt, batch, seq):
    nb = seq // BLOCK
    n = batch * seq
    nvt = seq // PROJ_ROWS
    n_sel = min(TOPK_MAX, seq // 4)
    blk = lambda w, col=0: pl.BlockSpec((BLOCK, w), lambda b, i: (b * nb + i, col))
    return pl.pallas_call(
        functools.partial(_dsa_kernel, n_sel=n_sel),
        grid=(batch, nb),
        in_specs=[blk(256), blk(512),
                  pl.BlockSpec((16, BLOCK), lambda b, i: (0, b * nb + i)),
                  blk(512, 1),
                  pl.BlockSpec((seq, LANES), lambda b, i: (b, 0)),
                  pl.BlockSpec((seq, LANES), lambda b, i: (b, 1)),
                  pl.BlockSpec((nvt, 256, PROJ_ROWS), lambda b, i: (b, 0, 0))],
        out_specs=blk(512),
        out_shape=jax.ShapeDtypeStruct((n, 512), _BF16),
        scratch_shapes=[pltpu.VMEM((seq, LANES), _F32),
                        pltpu.VMEM((seq, LANES), _BF16),
                        pltpu.VMEM((IDX_HEADS * BLOCK, LANES), _BF16),
                        pltpu.VMEM((DSA_Q_HEADS * BLOCK, LANES), _BF16),
                        pltpu.VMEM((seq, LANES), _BF16),
                        pltpu.VMEM((seq, LANES), _BF16),
                        pltpu.VMEM((8, LANES), _F32),
                        pltpu.VMEM((LANES, 512), _F32),
                        pltpu.VMEM((LANES, 512), _F32)],
        compiler_params=pltpu.CompilerParams(dimension_semantics=("arbitrary", "arbitrary"),
                                             vmem_limit_bytes=VMEM_LIMIT),
        name="dsa_attention",
    )(iq, bq, iwt, gates, ki2, ki2, bvt)


def _out_kernel(a_ref, b_ref, x_ref, w_ref, bo_ref, gain_ref, bias_ref, out_ref, *, alpha):
    half = a_ref.shape[1]
    y = jnp.dot(a_ref[...], w_ref[0:half, :], preferred_element_type=_F32)
    y = y + jnp.dot(b_ref[...], w_ref[half:2 * half, :], preferred_element_type=_F32)
    z = alpha * x_ref[...] + (y + bo_ref[...])
    mu = jnp.mean(z, axis=-1, keepdims=True)
    zc = z - mu
    var = jnp.mean(zc * zc, axis=-1, keepdims=True)
    out_ref[...] = zc * lax.rsqrt(var + LN_EPS) * gain_ref[...] + bias_ref[...]


def _out_projection(a, b, x2, w_out, b_out, gain, bias, alpha):
    n, d = x2.shape
    tm = PROJ_ROWS
    row = lambda w: pl.BlockSpec((tm, w), lambda i: (i, 0))
    full = lambda a_: pl.BlockSpec(a_.shape, lambda i: (0,) * a_.ndim)
    consts = (w_out.astype(_BF16), b_out[None, :], gain[None, :], bias[None, :])
    return pl.pallas_call(
        functools.partial(_out_kernel, alpha=alpha),
        grid=(n // tm,),
        in_specs=[row(a.shape[1]), row(b.shape[1]), row(d)] + [full(c) for c in consts],
        out_specs=row(d),
        out_shape=jax.ShapeDtypeStruct((n, d), x2.dtype),
        compiler_params=pltpu.CompilerParams(dimension_semantics=("arbitrary",),
                                             vmem_limit_bytes=VMEM_LIMIT),
        name="out_projection",
    )(a, b, x2, *consts)


def _layer(h, pos2, w_in, b_in, sinks, w_out, b_out, gain, bias, alpha):
    batch, seq, d = h.shape
    x2 = h.reshape(batch * seq, d)
    aq, ak2, bq, iq, ki2, gates, bvt, avt, iwt = _projection(x2, pos2, w_in, b_in)
    a = _swa_attention(aq, ak2, avt, gates, sinks, batch, seq)
    b = _dsa_attention(iq, bq, iwt, gates, ki2, bvt, batch, seq)
    return _out_projection(a, b, x2, w_out, b_out, gain, bias, alpha).reshape(batch, seq, d)


def kernel(x, positions, w_in, b_in, swa_sinks, w_out, b_out, ln_gain, ln_bias):
    batch, seq, d = x.shape
    depth = w_in.shape[0]
    assert d == 1024 and seq % KEY_CHUNK == 0 and (batch * seq) % PROJ_ROWS == 0
    assert KEY_CHUNK % PROJ_ROWS == 0 and KEY_CHUNK % TIE_CHUNK == 0
    alpha = (2.0 * depth) ** 0.25
    pos2 = positions.reshape(batch * seq, 1)
    h = x
    for layer in range(depth):
        h = _layer(h, pos2, w_in[layer], b_in[layer], swa_sinks[layer], w_out[layer], b_out[layer],
                   ln_gain[layer], ln_bias[layer], alpha)
    return h
```

```python
import functools

import numpy as np
import jax
import jax.numpy as jnp
from jax import lax
from jax.experimental import pallas as pl
from jax.experimental.pallas import tpu as pltpu

HEAD_DIM = 64
SWA_Q_HEADS = 8
SWA_KV_HEADS = 2
DSA_Q_HEADS = 8
IDX_HEADS = 4
IDX_DIM = 64
WINDOW = 128
BLOCK = 128
TOPK_MAX = 256
ROPE_THETA = 10000.0
LN_EPS = 1e-5

LANES = 128
PROJ_ROWS = 512
KEY_CHUNK = 1024
TIE_CHUNK = 512
VMEM_LIMIT = 56 * 1024 * 1024

LOG2E = 1.4426950408889634
NEG_BIG = -1e30
SHIFT_LIMIT = 60.0
F32_LOWEST = -3.4028234663852886e38
INT_MAX = 2 ** 31 - 1
KEY_NEG_INF = 0x807FFFFF - 2 ** 32
KEY_MIN_NORMAL = 0x00800000
SEARCH_GROUP = 3

_F32 = jnp.float32
_BF16 = jnp.bfloat16
_I32 = jnp.int32
_NT = (((1,), (1,)), ((), ()))


def _split_columns(m):
    swa_w = SWA_Q_HEADS * HEAD_DIM
    kv_w = SWA_KV_HEADS * HEAD_DIM
    dsa_w = DSA_Q_HEADS * HEAD_DIM
    sizes = (swa_w, kv_w, kv_w, swa_w, dsa_w, HEAD_DIM, HEAD_DIM, dsa_w,
             IDX_HEADS * IDX_DIM, IDX_DIM, IDX_HEADS)
    o = np.cumsum((0,) + sizes)
    aq, ak, av, ag, bq, bk, bv, bg, iq, ik, iw = [m[:, o[i]:o[i + 1]] for i in range(len(sizes))]

    def dup(c):
        return jnp.concatenate([c[:, j:j + HEAD_DIM] for j in range(0, c.shape[1], HEAD_DIM) for _ in (0, 1)],
                               axis=1)

    def pairs(c):
        r = c.reshape(c.shape[0], -1, 2, 2, HEAD_DIM // 2)
        return jnp.swapaxes(r, 2, 3).reshape(c.shape[0], -1)

    roped = pairs(jnp.concatenate([aq, dup(ak), bq, iq, dup(bk), dup(ik)], axis=1))
    return jnp.concatenate([roped, ag, bg], axis=1), (bv, av[:, :HEAD_DIM], av[:, HEAD_DIM:]), iw


def _proj_kernel(x_ref, pos_ref, inv_ref, sgn_ref, wm_ref, bm_ref,
                 wvt_ref, bvt_ref, wiw_ref, biw_ref,
                 aq_ref, ak2_ref, bq_ref, iq_ref, ki2_ref, g_ref, bvt_out_ref, avt_out_ref, iwt_ref):
    xb = x_ref[...].astype(_BF16)
    ang = pos_ref[...].astype(_F32) * inv_ref[...]
    cos = jnp.cos(ang)
    sin = jnp.sin(ang) * sgn_ref[...]

    q_scale = 0.125 * LOG2E
    roped_dst = ((aq_ref, 0, q_scale), (aq_ref, 256, q_scale), (ak2_ref, 0, 1.0), (bq_ref, 0, q_scale),
                 (bq_ref, 256, q_scale), (iq_ref, 0, 1.0), (ki2_ref, 0, 1.0))
    for g, (dst, off, scale) in enumerate(roped_dst):
        c0 = 256 * g
        hm = jnp.dot(xb, wm_ref[:, c0:c0 + 256], preferred_element_type=_F32) + bm_ref[:, c0:c0 + 256]
        for s in range(2):
            sl = slice(LANES * s, LANES * (s + 1))
            o = hm[:, sl] * cos + pltpu.roll(hm[:, sl], LANES // 2, axis=1) * sin
            if scale != 1.0:
                o = o * scale
            dst[:, off + LANES * s:off + LANES * (s + 1)] = o.astype(dst.dtype)

    n_roped = 256 * len(roped_dst)
    for g in range(4):
        c0 = n_roped + 256 * g
        h = jnp.dot(xb, wm_ref[:, c0:c0 + 256], preferred_element_type=_F32) + bm_ref[:, c0:c0 + 256]
        g_ref[:, 256 * g:256 * (g + 1)] = h * (1.0 / (1.0 + jnp.exp(-h)))

    vt = lax.dot_general(wvt_ref[...], xb, _NT, preferred_element_type=_F32) + bvt_ref[...]
    n_b = bvt_out_ref.shape[1]
    bvt_out_ref[0] = vt[0:n_b].astype(bvt_out_ref.dtype)
    avt_out_ref[0] = vt[n_b:].astype(avt_out_ref.dtype)
    iwt = lax.dot_general(wiw_ref[...], xb, _NT, preferred_element_type=_F32) + biw_ref[...]
    iwt_ref[...] = iwt * (IDX_HEADS ** -0.5 * IDX_DIM ** -0.5)


def _projection(x2, pos2, w_in, b_in):
    n, d = x2.shape
    tm = PROJ_ROWS
    wm, wvs, wiw = _split_columns(w_in)
    bm, bvs, biw = _split_columns(b_in[None, :])
    wm = wm.astype(_BF16)
    zpad = jnp.zeros((HEAD_DIM, d), w_in.dtype)
    one_row = jnp.zeros((HEAD_DIM,), b_in.dtype).at[0].set(1.0)
    wvt = jnp.concatenate([p for wv in wvs for p in (wv.T, zpad, zpad, wv.T)], axis=0).astype(_BF16)
    bvt = jnp.concatenate([p for bv in bvs for p in (bv[0], one_row, one_row, bv[0])])[:, None]
    wiw_t = jnp.concatenate([wiw.T, jnp.zeros((16 - IDX_HEADS, d), w_in.dtype)], axis=0).astype(_BF16)
    biw_t = jnp.concatenate([biw[0], jnp.zeros((16 - IDX_HEADS,), b_in.dtype)])[:, None]
    half = HEAD_DIM // 2
    inv = ROPE_THETA ** (-jnp.arange(0, HEAD_DIM, 2, dtype=_F32) / HEAD_DIM)
    inv128 = jnp.tile(inv, LANES // half)[None, :]
    sgn128 = jnp.concatenate([-jnp.ones((LANES // 2,), _F32), jnp.ones((LANES // 2,), _F32)])[None, :]

    row = lambda w: pl.BlockSpec((tm, w), lambda i: (i, 0))
    full = lambda a: pl.BlockSpec(a.shape, lambda i: (0,) * a.ndim)
    out_shape = (
        jax.ShapeDtypeStruct((n, 512), _BF16),
        jax.ShapeDtypeStruct((n, 256), _BF16),
        jax.ShapeDtypeStruct((n, 512), _BF16),
        jax.ShapeDtypeStruct((n, 256), _BF16),
        jax.ShapeDtypeStruct((n, 256), _BF16),
        jax.ShapeDtypeStruct((n, 1024), _F32),
        jax.ShapeDtypeStruct((n // tm, 256, tm), _BF16),
        jax.ShapeDtypeStruct((n // tm, 512, tm), _BF16),
        jax.ShapeDtypeStruct((16, n), _F32),
    )
    out_specs = (row(512), row(256), row(512), row(256), row(256), row(1024),
                 pl.BlockSpec((1, 256, tm), lambda i: (i, 0, 0)),
                 pl.BlockSpec((1, 512, tm), lambda i: (i, 0, 0)),
                 pl.BlockSpec((16, tm), lambda i: (0, i)))
    consts = (inv128, sgn128, wm, bm, wvt, bvt, wiw_t, biw_t)
    return pl.pallas_call(
        _proj_kernel,
        grid=(n // tm,),
        in_specs=[row(d), row(1)] + [full(a) for a in consts],
        out_specs=out_specs,
        out_shape=out_shape,
        compiler_params=pltpu.CompilerParams(dimension_semantics=("arbitrary",),
                                             vmem_limit_bytes=VMEM_LIMIT),
        name="in_projection",
    )(x2, pos2, *consts)


def _swa_kernel(q_ref, kc_ref, kp_ref, vc_ref, vp_ref, g_ref, sink_ref, out_ref):
    i = pl.program_id(1)
    tq = BLOCK
    n_sub = q_ref.shape[0] // tq
    group = SWA_Q_HEADS // SWA_KV_HEADS
    lane = lax.broadcasted_iota(_I32, (tq, LANES), 1)
    lo_half = (lane & (HEAD_DIM // 2)) == 0
    r = lax.broadcasted_iota(_I32, (2 * tq, LANES), 0)
    c = lax.broadcasted_iota(_I32, (2 * tq, LANES), 1)
    in_window = (c < r) & (r <= c + WINDOW)
    row = lax.broadcasted_iota(_I32, (LANES, LANES), 0)

    kwin = jnp.concatenate([kp_ref[...], kc_ref[...]], axis=0)
    vwin = jnp.concatenate([vp_ref[0], vc_ref[0]], axis=1)
    combos = [(g, j, odd) for g in range(SWA_KV_HEADS) for j in range(n_sub) for odd in (0, 1)]
    rows = lambda j: slice(tq * j, tq * (j + 1))
    win = lambda j: slice(tq * j, tq * (j + 2))
    slab_of = lambda g, s: slice(LANES * (g * (group // 2) + s), LANES * (g * (group // 2) + s + 1))

    logits = {}
    for g, j, odd in combos:
        qs = [jnp.where(lo_half if odd == 0 else ~lo_half, q_ref[rows(j), slab_of(g, s)].astype(_F32),
                        0.0).astype(_BF16) for s in range(group // 2)]
        logits[g, j, odd] = lax.dot_general(kwin[win(j), LANES * g:LANES * (g + 1)], jnp.concatenate(qs, axis=0),
                                            _NT, preferred_element_type=_F32)
    probs = {}
    for g, j, odd in combos:
        valid = in_window if j > 0 else in_window & ((r >= tq) | (i > 0))
        ps, sink_terms = [], []
        for s in range(group // 2):
            lg = jnp.where(valid, logits[g, j, odd][:, LANES * s:LANES * (s + 1)], -jnp.inf)
            sink = sink_ref[group * g + 2 * s + odd:group * g + 2 * s + odd + 1, :]
            m = jnp.maximum(jnp.max(lg, axis=0, keepdims=True), sink)
            ps.append(jnp.exp2(lg - m).astype(_BF16))
            sink_terms.append(jnp.exp2(sink - m))
        probs[g, j, odd] = jnp.concatenate(ps, axis=1), jnp.concatenate(sink_terms, axis=1)
    tiles = {}
    for g, j, odd in combos:
        p, sink_term = probs[g, j, odd]
        v = vwin[LANES * (2 * g + odd):LANES * (2 * g + odd + 1), win(j)]
        acc = jnp.dot(v, p, preferred_element_type=_F32)
        l_row = HEAD_DIM if odd == 0 else 0
        tiles[g, j, odd] = acc / (acc[l_row:l_row + 1, :] + sink_term)
    for g in range(SWA_KV_HEADS):
        for j in range(n_sub):
            for s in range(group // 2):
                sl = slice(LANES * s, LANES * (s + 1))
                tile = jnp.where(row < HEAD_DIM, tiles[g, j, 0][:, sl], tiles[g, j, 1][:, sl])
                out_ref[rows(j), slab_of(g, s)] = (tile.T * g_ref[rows(j), slab_of(g, s)]).astype(out_ref.dtype)


def _swa_attention(aq, ak2, avt, gates, sinks, batch, seq):
    tm = PROJ_ROWS
    ns = seq // tm
    n = batch * seq
    per = tm // BLOCK
    sink_b = jnp.broadcast_to((sinks.astype(_F32) * LOG2E)[:, None], (SWA_Q_HEADS, LANES))
    cur = lambda w: pl.BlockSpec((tm, w), lambda b, i: (b * ns + i, 0))
    return pl.pallas_call(
        _swa_kernel,
        grid=(batch, ns),
        in_specs=[cur(512), cur(256),
                  pl.BlockSpec((BLOCK, 256), lambda b, i: (b * ns * per + jnp.maximum(i * per - 1, 0), 0)),
                  pl.BlockSpec((1, 512, tm), lambda b, i: (b * ns + i, 0, 0)),
                  pl.BlockSpec((1, 512, BLOCK), lambda b, i: (b * ns + jnp.maximum(i - 1, 0), 0, per - 1)),
                  cur(512),
                  pl.BlockSpec((SWA_Q_HEADS, LANES), lambda b, i: (0, 0))],
        out_specs=cur(512),
        out_shape=jax.ShapeDtypeStruct((n, 512), _BF16),
        compiler_params=pltpu.CompilerParams(dimension_semantics=("arbitrary", "arbitrary"),
                                             vmem_limit_bytes=VMEM_LIMIT),
        name="swa_attention",
    )(aq, ak2, ak2, avt, avt, gates, sink_b)


_DSA_HEAD_ORDER = (0, 2, 4, 6, 1, 3, 5, 7)


def _key_to_float(k):
    return lax.bitcast_convert_type(jnp.where(k < 0, k ^ 0x7FFFFFFF, k), _F32)


def _key16_to_float(k):
    return lax.bitcast_convert_type(jnp.where(k <= 0, k ^ 0x7FFF ^ jnp.where(k == 0, -1, 0), k) << 16, _F32)


def _midpoint(a, b):
    return (a >> 1) + (b >> 1) + (a & b & 1)


def _tree_sum(parts, chains=8):
    accs = list(parts[:chains])
    for j in range(chains, len(parts)):
        accs[j % chains] = accs[j % chains] + parts[j]
    while len(accs) > 1:
        accs = [accs[2 * j] + accs[2 * j + 1] for j in range(len(accs) // 2)]
    return accs[0]


def _dsa_kernel(iq_ref, bq_ref, iwt_ref, g_ref, bk2_ref, ik2_ref, bvt_ref, out_ref,
                sc_ref, sc16_ref, iqm_ref, bqm_ref, ka_ref, kb_ref, kmax_ref, acc_e_ref, acc_o_ref, *, n_sel):
    tq, kc = BLOCK, KEY_CHUNK
    nh, half = DSA_Q_HEADS, DSA_Q_HEADS // 2
    i = pl.program_id(1)
    q0 = i * tq
    lane = lax.broadcasted_iota(_I32, (tq, LANES), 1)
    lo_half = (lane & (HEAD_DIM // 2)) == 0
    one_even, one_odd = HEAD_DIM // 2, 0

    kh = kc // 2
    rem = (q0 + tq) % kc
    n_whole = (q0 + tq) // kc + jnp.where(rem > kh, 1, 0)
    has_half = (rem > 0) & (rem <= kh)
    k_half = n_whole * kc
    n_rows = k_half + jnp.where(has_half, kh, 0)

    def over_keys(body, carry):
        def two(c, cr):
            k0 = pl.multiple_of(c * 2 * kc, 2 * kc)
            return body(k0 + kc, kc, body(k0, kc, cr))

        carry = lax.fori_loop(0, n_whole // 2, two, carry)
        carry = lax.cond(n_whole % 2 == 1, lambda cr: body(pl.multiple_of((n_whole - 1) * kc, kc), kc, cr),
                         lambda cr: cr, carry)
        return lax.cond(has_half, lambda cr: body(pl.multiple_of(k_half, kh), kh, cr), lambda cr: cr, carry)

    @pl.when(i == 0)
    def _():
        klane = lax.broadcasted_iota(_I32, (kc, LANES), 1)
        klo = (klane & (HEAD_DIM // 2)) == 0

        def body(c, mx):
            k0 = pl.multiple_of(c * kc, kc)
            k = bk2_ref[pl.ds(k0, kc), :].astype(_F32)
            ka_ref[pl.ds(k0, kc), :] = jnp.where(klane == one_even, 1.0, k).astype(_BF16)
            kb_ref[pl.ds(k0, kc), :] = jnp.where(klane == one_odd, 1.0, k).astype(_BF16)
            return jnp.maximum(mx, jnp.sum(jnp.where(klo, k * k, 0.0), axis=1, keepdims=True))

        mx = lax.fori_loop(0, sc_ref.shape[0] // kc, body, jnp.zeros((kc, 1), _F32))
        kmax_ref[...] = jnp.broadcast_to(jnp.sqrt(jnp.max(mx, axis=0, keepdims=True)), kmax_ref.shape)

    def masked_half(ref, h):
        slab = ref[:, LANES * (h // 2):LANES * (h // 2 + 1)].astype(_F32)
        return jnp.where(lo_half if h % 2 == 0 else ~lo_half, slab, 0.0)

    for h in range(IDX_HEADS):
        iqm_ref[h * tq:(h + 1) * tq, :] = masked_half(iq_ref, h).astype(_BF16)
    kmax = kmax_ref[0:1, 0:1]
    shift_max = jnp.zeros((tq, 1), _F32)
    for p, h in enumerate(_DSA_HEAD_ORDER):
        qm = masked_half(bq_ref, h)
        shift = jnp.sqrt(jnp.sum(qm * qm, axis=1, keepdims=True)) * kmax
        shift_max = jnp.maximum(shift_max, shift)
        one_lane = one_even if h % 2 == 0 else one_odd
        bqm_ref[p * tq:(p + 1) * tq, :] = jnp.where(lane == one_lane, -shift, qm).astype(_BF16)
    bounded = jnp.max(shift_max) <= SHIFT_LIMIT

    w = [iwt_ref[h:h + 1, :] for h in range(IDX_HEADS)]
    def score_piece(k0, rows, diagonal):
        z = lax.dot_general(ik2_ref[pl.ds(k0, rows), :], iqm_ref[...], _NT, preferred_element_type=_F32)
        sc = w[0] * jnp.maximum(z[:, 0:LANES], 0.0)
        for h in range(1, IDX_HEADS):
            sc = sc + w[h] * jnp.maximum(z[:, LANES * h:LANES * (h + 1)], 0.0)
        if diagonal:
            kidx = k0 + lax.broadcasted_iota(_I32, (rows, LANES), 0)
            qidx = q0 + lax.broadcasted_iota(_I32, (rows, LANES), 1)
            sc = jnp.where(kidx <= qidx, sc, -jnp.inf)
        sc_ref[pl.ds(k0, rows), :] = sc
        top = lax.bitcast_convert_type(sc, _I32) & jnp.int32(-65536)
        sc16_ref[pl.ds(k0, rows), :] = lax.bitcast_convert_type(top, _F32).astype(_BF16)

    def score_two(c, carry):
        k0 = pl.multiple_of(c * 2 * kc, 2 * kc)
        score_piece(k0, kc, False)
        score_piece(k0 + kc, kc, False)
        return carry

    n_plain = n_whole - jnp.where(has_half, 0, 1)
    lax.fori_loop(0, n_plain // 2, score_two, 0)

    @pl.when(n_plain % 2 == 1)
    def _():
        score_piece(pl.multiple_of((n_plain - 1) * kc, kc), kc, False)

    @pl.when(has_half)
    def _():
        score_piece(pl.multiple_of(k_half, kh), kh, True)

    @pl.when(jnp.logical_not(has_half))
    def _():
        score_piece(pl.multiple_of((n_whole - 1) * kc, kc), kc, True)

    def count_ge(t):
        def body(k0, rows, acc):
            for h in range(0, rows, 512):
                ind = jnp.where(sc_ref[pl.ds(k0 + h, 512), :] >= t, 1, 0)
                acc = acc + jnp.sum(ind.reshape(64, 8, LANES), axis=0)
            return acc
        return jnp.sum(over_keys(body, jnp.zeros((8, LANES), _I32)), axis=0, keepdims=True)

    def count_ge16(t):
        t = t.astype(_BF16)

        def body(k0, rows, acc):
            d = pltpu.bitcast(sc16_ref[pl.ds(k0, rows), :] - t, _I32)
            neg = lax.shift_right_logical(d, 15) & 0x00010001
            return acc + jnp.sum(neg.reshape(rows // 16, 8, LANES), axis=0)
        acc = over_keys(body, jnp.zeros((8, LANES), _I32))
        below = jnp.sum((acc & 0xFFFF) + lax.shift_right_logical(acc, 16), axis=0, keepdims=True)
        return n_rows - below

    def search_pass(count, to_thr, st):
        lo, hi, clo, chi, t, thr = st
        t_up, t_dn = _midpoint(t, hi), _midpoint(lo, t)
        thr_up, thr_dn = to_thr(t_up), to_thr(t_dn)
        cnt = count(thr)
        ok = cnt >= n_sel
        return (jnp.where(ok, t, lo), jnp.where(ok, hi, t), jnp.where(ok, cnt, clo), jnp.where(ok, chi, cnt),
                jnp.where(ok, t_up, t_dn), jnp.where(ok, thr_up, thr_dn))

    vec = lambda v: jnp.full((1, LANES), v, _I32)
    st = (vec(KEY_NEG_INF >> 16), vec(1 << 15), vec(0) + n_rows, vec(0), vec(0), _key16_to_float(vec(0)))
    st = lax.fori_loop(0, 16, lambda j, s: search_pass(count_ge16, _key16_to_float, s), st)
    hint_lo = jnp.maximum(st[0] << 16, KEY_NEG_INF)
    hint_hi = jnp.where(st[1] >= (1 << 15), INT_MAX, st[1] << 16)
    hint_hi = jnp.where((hint_hi > 0) & (hint_hi < KEY_MIN_NORMAL), KEY_MIN_NORMAL, hint_hi)
    c1 = count_ge(_key_to_float(hint_lo))
    ok1 = c1 >= n_sel
    lo, clo = jnp.where(ok1, hint_lo, KEY_NEG_INF), jnp.where(ok1, c1, n_rows)
    hi, chi = jnp.where(ok1, INT_MAX, hint_lo), jnp.where(ok1, 0, c1)
    c2 = count_ge(_key_to_float(hint_hi))
    up2 = (c2 >= n_sel) & (hint_hi > lo)
    dn2 = (c2 < n_sel) & (hint_hi < hi)
    lo, clo = jnp.where(up2, hint_hi, lo), jnp.where(up2, c2, clo)
    hi, chi = jnp.where(dn2, hint_hi, hi), jnp.where(dn2, c2, chi)

    def unsettled(lo, hi, clo):
        return ~((clo == n_sel) | (hi - 1 <= lo) | ((lo >= 0) & (hi <= KEY_MIN_NORMAL)))

    def fine_group(c):
        s = c[1:7]
        for _ in range(SEARCH_GROUP):
            s = search_pass(count_ge, _key_to_float, s)
        return (c[0] + 1,) + s + (jnp.max(jnp.where(unsettled(*s[:3]), 1, 0)),)

    t0 = _midpoint(lo, hi)
    init = (jnp.int32(0), lo, hi, clo, chi, t0, _key_to_float(t0),
            jnp.max(jnp.where(unsettled(lo, hi, clo), 1, 0)))
    _, lo, hi, clo, chi, _, _, _ = lax.while_loop(lambda c: (c[7] > 0) & (c[0] < 16), fine_group, init)
    t_lo = _key_to_float(lo)
    t_hi = _key_to_float(hi)

    need = (clo > n_sel) & (lo > KEY_NEG_INF)

    @pl.when(jnp.max(jnp.where(need, 1, 0)) > 0)
    def _():
        tk = TIE_CHUNK
        room = (n_sel - chi).astype(_F32)
        before = (lax.broadcasted_iota(_I32, (tk, tk), 0) > lax.broadcasted_iota(_I32, (tk, tk), 1))
        before = jnp.where(before, 1.0, 0.0).astype(_BF16)

        def body(k0, rows, seen):
            for h in range(rows // tk):
                x = sc_ref[pl.ds(k0 + h * tk, tk), :]
                tie = (x >= t_lo) & ~(x >= t_hi)
                tf = jnp.where(tie, 1.0, 0.0)
                rank = seen + jnp.dot(before, tf.astype(_BF16), preferred_element_type=_F32)
                sc_ref[pl.ds(k0 + h * tk, tk), :] = jnp.where(tie & (rank >= room), -jnp.inf, x)
                seen = seen + jnp.sum(tf, axis=0, keepdims=True)
            return seen

        over_keys(body, jnp.zeros((1, LANES), _F32))

    thr = jnp.maximum(t_lo, F32_LOWEST)

    acc_e_ref[...] = jnp.zeros_like(acc_e_ref)
    acc_o_ref[...] = jnp.zeros_like(acc_o_ref)
    vt_rows = bvt_ref.shape[2]

    def masked_logits(k0, rows):
        sel = sc_ref[pl.ds(k0, rows), :] >= thr
        lge = lax.dot_general(ka_ref[pl.ds(k0, rows), :], bqm_ref[0:half * tq, :], _NT,
                              preferred_element_type=_F32)
        lgo = lax.dot_general(kb_ref[pl.ds(k0, rows), :], bqm_ref[half * tq:nh * tq, :], _NT,
                              preferred_element_type=_F32)
        tiles = lambda lg: [jnp.where(sel, lg[:, LANES * j:LANES * (j + 1)], NEG_BIG) for j in range(half)]
        vt = jnp.concatenate([bvt_ref[k0 // vt_rows + j] for j in range(rows // vt_rows)], axis=1)
        return ((acc_e_ref, tiles(lge), vt[0:LANES, :]), (acc_o_ref, tiles(lgo), vt[LANES:2 * LANES, :]))

    @pl.when(bounded)
    def _():
        def body(k0, rows, carry):
            for ref, tiles, v in masked_logits(k0, rows):
                p = jnp.concatenate([jnp.exp2(t).astype(_BF16) for t in tiles], axis=1)
                ref[...] += jnp.dot(v, p, preferred_element_type=_F32)
            return carry
        over_keys(body, jnp.int32(0))

    @pl.when(jnp.logical_not(bounded))
    def _():
        def body(k0, rows, m):
            m_out = []
            for g, (ref, tiles, v) in enumerate(masked_logits(k0, rows)):
                ps, alphas = [], []
                for j, t in enumerate(tiles):
                    sl = slice(LANES * (g * half + j), LANES * (g * half + j + 1))
                    m_new = jnp.maximum(m[:, sl], jnp.max(t, axis=0, keepdims=True))
                    ps.append(jnp.exp2(t - m_new).astype(_BF16))
                    alphas.append(jnp.exp2(m[:, sl] - m_new))
                    m_out.append(m_new)
                ref[...] = ref[...] * jnp.concatenate(alphas, axis=1) + jnp.dot(
                    v, jnp.concatenate(ps, axis=1), preferred_element_type=_F32)
            return jnp.concatenate(m_out, axis=1)
        over_keys(body, jnp.full((1, nh * LANES), NEG_BIG, _F32))

    row = lax.broadcasted_iota(_I32, (LANES, LANES), 0)
    for j in range(half):
        sl = slice(LANES * j, LANES * (j + 1))
        even = acc_e_ref[:, sl] / acc_e_ref[HEAD_DIM:HEAD_DIM + 1, sl]
        odd = acc_o_ref[:, sl] / acc_o_ref[0:1, sl]
        tile = jnp.where(row < HEAD_DIM, even, odd)
        out_ref[:, sl] = (tile.T * g_ref[:, sl]).astype(out_ref.dtype)


def _dsa_attention(iq, bq, iwt, gates, ki2, bvt, batch, seq):
    nb = seq // BLOCK
    n = batch * seq
    nvt = seq // PROJ_ROWS
    n_sel = min(TOPK_MAX, seq // 4)
    blk = lambda w, col=0: pl.BlockSpec((BLOCK, w), lambda b, i: (b * nb + i, col))
    return pl.pallas_call(
        functools.partial(_dsa_kernel, n_sel=n_sel),
        grid=(batch, nb),
        in_specs=[blk(256), blk(512),
                  pl.BlockSpec((16, BLOCK), lambda b, i: (0, b * nb + i)),
                  blk(512, 1),
                  pl.BlockSpec((seq, LANES), lambda b, i: (b, 0)),
                  pl.BlockSpec((seq, LANES), lambda b, i: (b, 1)),
                  pl.BlockSpec((nvt, 256, PROJ_ROWS), lambda b, i: (b, 0, 0))],
        out_specs=blk(512),
        out_shape=jax.ShapeDtypeStruct((n, 512), _BF16),
        scratch_shapes=[pltpu.VMEM((seq, LANES), _F32),
                        pltpu.VMEM((seq, LANES), _BF16),
                        pltpu.VMEM((IDX_HEADS * BLOCK, LANES), _BF16),
                        pltpu.VMEM((DSA_Q_HEADS * BLOCK, LANES), _BF16),
                        pltpu.VMEM((seq, LANES), _BF16),
                        pltpu.VMEM((seq, LANES), _BF16),
                        pltpu.VMEM((8, LANES), _F32),
                        pltpu.VMEM((LANES, 512), _F32),
                        pltpu.VMEM((LANES, 512), _F32)],
        compiler_params=pltpu.CompilerParams(dimension_semantics=("arbitrary", "arbitrary"),
                                             vmem_limit_bytes=VMEM_LIMIT),
        name="dsa_attention",
    )(iq, bq, iwt, gates, ki2, ki2, bvt)


def _out_kernel(a_ref, b_ref, x_ref, w_ref, bo_ref, gain_ref, bias_ref, out_ref, *, alpha):
    half = a_ref.shape[1]
    y = jnp.dot(a_ref[...], w_ref[0:half, :], preferred_element_type=_F32)
    y = y + jnp.dot(b_ref[...], w_ref[half:2 * half, :], preferred_element_type=_F32)
    z = alpha * x_ref[...] + (y + bo_ref[...])
    mu = jnp.mean(z, axis=-1, keepdims=True)
    zc = z - mu
    var = jnp.mean(zc * zc, axis=-1, keepdims=True)
    out_ref[...] = zc * lax.rsqrt(var + LN_EPS) * gain_ref[...] + bias_ref[...]


def _out_projection(a, b, x2, w_out, b_out, gain, bias, alpha):
    n, d = x2.shape
    tm = PROJ_ROWS
    row = lambda w: pl.BlockSpec((tm, w), lambda i: (i, 0))
    full = lambda a_: pl.BlockSpec(a_.shape, lambda i: (0,) * a_.ndim)
    consts = (w_out.astype(_BF16), b_out[None, :], gain[None, :], bias[None, :])
    return pl.pallas_call(
        functools.partial(_out_kernel, alpha=alpha),
        grid=(n // tm,),
        in_specs=[row(a.shape[1]), row(b.shape[1]), row(d)] + [full(c) for c in consts],
        out_specs=row(d),
        out_shape=jax.ShapeDtypeStruct((n, d), x2.dtype),
        compiler_params=pltpu.CompilerParams(dimension_semantics=("arbitrary",),
                                             vmem_limit_bytes=VMEM_LIMIT),
        name="out_projection",
    )(a, b, x2, *consts)


def _layer(h, pos2, w_in, b_in, sinks, w_out, b_out, gain, bias, alpha):
    batch, seq, d = h.shape
    x2 = h.reshape(batch * seq, d)
    aq, ak2, bq, iq, ki2, gates, bvt, avt, iwt = _projection(x2, pos2, w_in, b_in)
    a = _swa_attention(aq, ak2, avt, gates, sinks, batch, seq)
    b = _dsa_attention(iq, bq, iwt, gates, ki2, bvt, batch, seq)
    return _out_projection(a, b, x2, w_out, b_out, gain, bias, alpha).reshape(batch, seq, d)


def kernel(x, positions, w_in, b_in, swa_sinks, w_out, b_out, ln_gain, ln_bias):
    batch, seq, d = x.shape
    depth = w_in.shape[0]
    assert d == 1024 and seq % KEY_CHUNK == 0 and (batch * seq) % PROJ_ROWS == 0
    assert KEY_CHUNK % PROJ_ROWS == 0 and KEY_CHUNK % TIE_CHUNK == 0
    alpha = (2.0 * depth) ** 0.25
    pos2 = positions.reshape(batch * seq, 1)
    h = x
    for layer in range(depth):
        h = _layer(h, pos2, w_in[layer], b_in[layer], swa_sinks[layer], w_out[layer], b_out[layer],
                   ln_gain[layer], ln_bias[layer], alpha)
    return h
```

```python
import functools

import numpy as np
import jax
import jax.numpy as jnp
from jax import lax
from jax.experimental import pallas as pl
from jax.experimental.pallas import tpu as pltpu

HEAD_DIM = 64
SWA_Q_HEADS = 8
SWA_KV_HEADS = 2
DSA_Q_HEADS = 8
IDX_HEADS = 4
IDX_DIM = 64
WINDOW = 128
BLOCK = 128
TOPK_MAX = 256
ROPE_THETA = 10000.0
LN_EPS = 1e-5

LANES = 128
PROJ_ROWS = 512
KEY_CHUNK = 1024
TIE_CHUNK = 512
VMEM_LIMIT = 56 * 1024 * 1024

LOG2E = 1.4426950408889634
NEG_BIG = -1e30
SHIFT_LIMIT = 60.0
F32_LOWEST = -3.4028234663852886e38
INT_MAX = 2 ** 31 - 1
KEY_NEG_INF = 0x807FFFFF - 2 ** 32
KEY_MIN_NORMAL = 0x00800000
SEARCH_GROUP = 3

_F32 = jnp.float32
_BF16 = jnp.bfloat16
_I32 = jnp.int32
_NT = (((1,), (1,)), ((), ()))


def _split_columns(m):
    swa_w = SWA_Q_HEADS * HEAD_DIM
    kv_w = SWA_KV_HEADS * HEAD_DIM
    dsa_w = DSA_Q_HEADS * HEAD_DIM
    sizes = (swa_w, kv_w, kv_w, swa_w, dsa_w, HEAD_DIM, HEAD_DIM, dsa_w,
             IDX_HEADS * IDX_DIM, IDX_DIM, IDX_HEADS)
    o = np.cumsum((0,) + sizes)
    aq, ak, av, ag, bq, bk, bv, bg, iq, ik, iw = [m[:, o[i]:o[i + 1]] for i in range(len(sizes))]

    def dup(c):
        return jnp.concatenate([c[:, j:j + HEAD_DIM] for j in range(0, c.shape[1], HEAD_DIM) for _ in (0, 1)],
                               axis=1)

    def pairs(c):
        r = c.reshape(c.shape[0], -1, 2, 2, HEAD_DIM // 2)
        return jnp.swapaxes(r, 2, 3).reshape(c.shape[0], -1)

    roped = pairs(jnp.concatenate([aq, dup(ak), bq, iq, dup(bk), dup(ik)], axis=1))
    return jnp.concatenate([roped, ag, bg], axis=1), (bv, av[:, :HEAD_DIM], av[:, HEAD_DIM:]), iw


def _proj_kernel(x_ref, pos_ref, inv_ref, sgn_ref, wm_ref, bm_ref,
                 wvt_ref, bvt_ref, wiw_ref, biw_ref,
                 aq_ref, ak2_ref, bq_ref, iq_ref, ki2_ref, g_ref, bvt_out_ref, avt_out_ref, iwt_ref):
    xb = x_ref[...].astype(_BF16)
    ang = pos_ref[...].astype(_F32) * inv_ref[...]
    cos = jnp.cos(ang)
    sin = jnp.sin(ang) * sgn_ref[...]

    q_scale = 0.125 * LOG2E
    roped_dst = ((aq_ref, 0, q_scale), (aq_ref, 256, q_scale), (ak2_ref, 0, 1.0), (bq_ref, 0, q_scale),
                 (bq_ref, 256, q_scale), (iq_ref, 0, 1.0), (ki2_ref, 0, 1.0))
    for g, (dst, off, scale) in enumerate(roped_dst):
        c0 = 256 * g
        hm = jnp.dot(xb, wm_ref[:, c0:c0 + 256], preferred_element_type=_F32) + bm_ref[:, c0:c0 + 256]
        for s in range(2):
            sl = slice(LANES * s, LANES * (s + 1))
            o = hm[:, sl] * cos + pltpu.roll(hm[:, sl], LANES // 2, axis=1) * sin
            if scale != 1.0:
                o = o * scale
            dst[:, off + LANES * s:off + LANES * (s + 1)] = o.astype(dst.dtype)

    n_roped = 256 * len(roped_dst)
    for g in range(4):
        c0 = n_roped + 256 * g
        h = jnp.dot(xb, wm_ref[:, c0:c0 + 256], preferred_element_type=_F32) + bm_ref[:, c0:c0 + 256]
        g_ref[:, 256 * g:256 * (g + 1)] = h * (1.0 / (1.0 + jnp.exp(-h)))

    vt = lax.dot_general(wvt_ref[...], xb, _NT, preferred_element_type=_F32) + bvt_ref[...]
    n_b = bvt_out_ref.shape[1]
    bvt_out_ref[0] = vt[0:n_b].astype(bvt_out_ref.dtype)
    avt_out_ref[0] = vt[n_b:].astype(avt_out_ref.dtype)
    iwt = lax.dot_general(wiw_ref[...], xb, _NT, preferred_element_type=_F32) + biw_ref[...]
    iwt_ref[...] = iwt * (IDX_HEADS ** -0.5 * IDX_DIM ** -0.5)


def _projection(x2, pos2, w_in, b_in):
    n, d = x2.shape
    tm = PROJ_ROWS
    wm, wvs, wiw = _split_columns(w_in)
    bm, bvs, biw = _split_columns(b_in[None, :])
    wm = wm.astype(_BF16)
    zpad = jnp.zeros((HEAD_DIM, d), w_in.dtype)
    one_row = jnp.zeros((HEAD_DIM,), b_in.dtype).at[0].set(1.0)
    wvt = jnp.concatenate([p for wv in wvs for p in (wv.T, zpad, zpad, wv.T)], axis=0).astype(_BF16)
    bvt = jnp.concatenate([p for bv in bvs for p in (bv[0], one_row, one_row, bv[0])])[:, None]
    wiw_t = jnp.concatenate([wiw.T, jnp.zeros((16 - IDX_HEADS, d), w_in.dtype)], axis=0).astype(_BF16)
    biw_t = jnp.concatenate([biw[0], jnp.zeros((16 - IDX_HEADS,), b_in.dtype)])[:, None]
    half = HEAD_DIM // 2
    inv = ROPE_THETA ** (-jnp.arange(0, HEAD_DIM, 2, dtype=_F32) / HEAD_DIM)
    inv128 = jnp.tile(inv, LANES // half)[None, :]
    sgn128 = jnp.concatenate([-jnp.ones((LANES // 2,), _F32), jnp.ones((LANES // 2,), _F32)])[None, :]

    row = lambda w: pl.BlockSpec((tm, w), lambda i: (i, 0))
    full = lambda a: pl.BlockSpec(a.shape, lambda i: (0,) * a.ndim)
    out_shape = (
        jax.ShapeDtypeStruct((n, 512), _BF16),
        jax.ShapeDtypeStruct((n, 256), _BF16),
        jax.ShapeDtypeStruct((n, 512), _BF16),
        jax.ShapeDtypeStruct((n, 256), _BF16),
        jax.ShapeDtypeStruct((n, 256), _BF16),
        jax.ShapeDtypeStruct((n, 1024), _F32),
        jax.ShapeDtypeStruct((n // tm, 256, tm), _BF16),
        jax.ShapeDtypeStruct((n // tm, 512, tm), _BF16),
        jax.ShapeDtypeStruct((16, n), _F32),
    )
    out_specs = (row(512), row(256), row(512), row(256), row(256), row(1024),
                 pl.BlockSpec((1, 256, tm), lambda i: (i, 0, 0)),
                 pl.BlockSpec((1, 512, tm), lambda i: (i, 0, 0)),
                 pl.BlockSpec((16, tm), lambda i: (0, i)))
    consts = (inv128, sgn128, wm, bm, wvt, bvt, wiw_t, biw_t)
    return pl.pallas_call(
        _proj_kernel,
        grid=(n // tm,),
        in_specs=[row(d), row(1)] + [full(a) for a in consts],
        out_specs=out_specs,
        out_shape=out_shape,
        compiler_params=pltpu.CompilerParams(dimension_semantics=("arbitrary",),
                                             vmem_limit_bytes=VMEM_LIMIT),
        name="in_projection",
    )(x2, pos2, *consts)


def _swa_kernel(q_ref, kc_ref, kp_ref, vc_ref, vp_ref, g_ref, sink_ref, out_ref):
    i = pl.program_id(1)
    tq = BLOCK
    n_sub = q_ref.shape[0] // tq
    group = SWA_Q_HEADS // SWA_KV_HEADS
    lane = lax.broadcasted_iota(_I32, (tq, LANES), 1)
    lo_half = (lane & (HEAD_DIM // 2)) == 0
    r = lax.broadcasted_iota(_I32, (2 * tq, LANES), 0)
    c = lax.broadcasted_iota(_I32, (2 * tq, LANES), 1)
    in_window = (c < r) & (r <= c + WINDOW)
    row = lax.broadcasted_iota(_I32, (LANES, LANES), 0)

    kwin = jnp.concatenate([kp_ref[...], kc_ref[...]], axis=0)
    vwin = jnp.concatenate([vp_ref[0], vc_ref[0]], axis=1)
    combos = [(g, j, odd) for g in range(SWA_KV_HEADS) for j in range(n_sub) for odd in (0, 1)]
    rows = lambda j: slice(tq * j, tq * (j + 1))
    win = lambda j: slice(tq * j, tq * (j + 2))
    slab_of = lambda g, s: slice(LANES * (g * (group // 2) + s), LANES * (g * (group // 2) + s + 1))

    logits = {}
    for g, j, odd in combos:
        qs = [jnp.where(lo_half if odd == 0 else ~lo_half, q_ref[rows(j), slab_of(g, s)].astype(_F32),
                        0.0).astype(_BF16) for s in range(group // 2)]
        logits[g, j, odd] = lax.dot_general(kwin[win(j), LANES * g:LANES * (g + 1)], jnp.concatenate(qs, axis=0),
                                            _NT, preferred_element_type=_F32)
    probs = {}
    for g, j, odd in combos:
        valid = in_window if j > 0 else in_window & ((r >= tq) | (i > 0))
        ps, sink_terms = [], []
        for s in range(group // 2):
            lg = jnp.where(valid, logits[g, j, odd][:, LANES * s:LANES * (s + 1)], -jnp.inf)
            sink = sink_ref[group * g + 2 * s + odd:group * g + 2 * s + odd + 1, :]
            m = jnp.maximum(jnp.max(lg, axis=0, keepdims=True), sink)
            ps.append(jnp.exp2(lg - m).astype(_BF16))
            sink_terms.append(jnp.exp2(sink - m))
        probs[g, j, odd] = jnp.concatenate(ps, axis=1), jnp.concatenate(sink_terms, axis=1)
    tiles = {}
    for g, j, odd in combos:
        p, sink_term = probs[g, j, odd]
        v = vwin[LANES * (2 * g + odd):LANES * (2 * g + odd + 1), win(j)]
        acc = jnp.dot(v, p, preferred_element_type=_F32)
        l_row = HEAD_DIM if odd == 0 else 0
        tiles[g, j, odd] = acc / (acc[l_row:l_row + 1, :] + sink_term)
    for g in range(SWA_KV_HEADS):
        for j in range(n_sub):
            for s in range(group // 2):
                sl = slice(LANES * s, LANES * (s + 1))
                tile = jnp.where(row < HEAD_DIM, tiles[g, j, 0][:, sl], tiles[g, j, 1][:, sl])
                out_ref[rows(j), slab_of(g, s)] = (tile.T * g_ref[rows(j), slab_of(g, s)]).astype(out_ref.dtype)


def _swa_attention(aq, ak2, avt, gates, sinks, batch, seq):
    tm = PROJ_ROWS
    ns = seq // tm
    n = batch * seq
    per = tm // BLOCK
    sink_b = jnp.broadcast_to((sinks.astype(_F32) * LOG2E)[:, None], (SWA_Q_HEADS, LANES))
    cur = lambda w: pl.BlockSpec((tm, w), lambda b, i: (b * ns + i, 0))
    return pl.pallas_call(
        _swa_kernel,
        grid=(batch, ns),
        in_specs=[cur(512), cur(256),
                  pl.BlockSpec((BLOCK, 256), lambda b, i: (b * ns * per + jnp.maximum(i * per - 1, 0), 0)),
                  pl.BlockSpec((1, 512, tm), lambda b, i: (b * ns + i, 0, 0)),
                  pl.BlockSpec((1, 512, BLOCK), lambda b, i: (b * ns + jnp.maximum(i - 1, 0), 0, per - 1)),
                  cur(512),
                  pl.BlockSpec((SWA_Q_HEADS, LANES), lambda b, i: (0, 0))],
        out_specs=cur(512),
        out_shape=jax.ShapeDtypeStruct((n, 512), _BF16),
        compiler_params=pltpu.CompilerParams(dimension_semantics=("arbitrary", "arbitrary"),
                                             vmem_limit_bytes=VMEM_LIMIT),
        name="swa_attention",
    )(aq, ak2, ak2, avt, avt, gates, sink_b)


_DSA_HEAD_ORDER = (0, 2, 4, 6, 1, 3, 5, 7)


def _key_to_float(k):
    return lax.bitcast_convert_type(jnp.where(k < 0, k ^ 0x7FFFFFFF, k), _F32)


def _key16_to_float(k):
    return lax.bitcast_convert_type(jnp.where(k <= 0, k ^ 0x7FFF ^ jnp.where(k == 0, -1, 0), k) << 16, _F32)


def _midpoint(a, b):
    return (a >> 1) + (b >> 1) + (a & b & 1)


def _tree_sum(parts, chains=8):
    accs = list(parts[:chains])
    for j in range(chains, len(parts)):
        accs[j % chains] = accs[j % chains] + parts[j]
    while len(accs) > 1:
        accs = [accs[2 * j] + accs[2 * j + 1] for j in range(len(accs) // 2)]
    return accs[0]


def _dsa_kernel(iq_ref, bq_ref, iwt_ref, g_ref, bk2_ref, ik2_ref, bvt_ref, out_ref,
                sc_ref, sc16_ref, iqm_ref, bqm_ref, ka_ref, kb_ref, kmax_ref, acc_e_ref, acc_o_ref, *, n_sel):
    tq, kc = BLOCK, KEY_CHUNK
    nh, half = DSA_Q_HEADS, DSA_Q_HEADS // 2
    i = pl.program_id(1)
    q0 = i * tq
    lane = lax.broadcasted_iota(_I32, (tq, LANES), 1)
    lo_half = (lane & (HEAD_DIM // 2)) == 0
    one_even, one_odd = HEAD_DIM // 2, 0

    kh = kc // 2
    rem = (q0 + tq) % kc
    n_whole = (q0 + tq) // kc + jnp.where(rem > kh, 1, 0)
    has_half = (rem > 0) & (rem <= kh)
    k_half = n_whole * kc
    n_rows = k_half + jnp.where(has_half, kh, 0)

    def over_keys(body, carry, stages=()):
        def prepared(pieces):
            states = list(pieces)
            for stage in stages:
                states = [stage(s) for s in states]
            return states

        def run(pieces, cr):
            for (k0, rows), state in zip(pieces, prepared(pieces)):
                cr = body(k0, rows, cr, state) if stages else body(k0, rows, cr)
            return cr

        def two(c, cr):
            k0 = pl.multiple_of(c * 2 * kc, 2 * kc)
            return run([(k0, kc), (k0 + kc, kc)], cr)

        carry = lax.fori_loop(0, n_whole // 2, two, carry)
        carry = lax.cond(n_whole % 2 == 1, lambda cr: run([(pl.multiple_of((n_whole - 1) * kc, kc), kc)], cr),
                         lambda cr: cr, carry)
        return lax.cond(has_half, lambda cr: run([(pl.multiple_of(k_half, kh), kh)], cr), lambda cr: cr, carry)

    @pl.when(i == 0)
    def _():
        klane = lax.broadcasted_iota(_I32, (kc, LANES), 1)
        klo = (klane & (HEAD_DIM // 2)) == 0

        def body(c, mx):
            k0 = pl.multiple_of(c * kc, kc)
            k = bk2_ref[pl.ds(k0, kc), :].astype(_F32)
            ka_ref[pl.ds(k0, kc), :] = jnp.where(klane == one_even, 1.0, k).astype(_BF16)
            kb_ref[pl.ds(k0, kc), :] = jnp.where(klane == one_odd, 1.0, k).astype(_BF16)
            return jnp.maximum(mx, jnp.sum(jnp.where(klo, k * k, 0.0), axis=1, keepdims=True))

        mx = lax.fori_loop(0, sc_ref.shape[0] // kc, body, jnp.zeros((kc, 1), _F32))
        kmax_ref[...] = jnp.broadcast_to(jnp.sqrt(jnp.max(mx, axis=0, keepdims=True)), kmax_ref.shape)

    def masked_half(ref, h):
        slab = ref[:, LANES * (h // 2):LANES * (h // 2 + 1)].astype(_F32)
        return jnp.where(lo_half if h % 2 == 0 else ~lo_half, slab, 0.0)

    for h in range(IDX_HEADS):
        iqm_ref[h * tq:(h + 1) * tq, :] = masked_half(iq_ref, h).astype(_BF16)
    kmax = kmax_ref[0:1, 0:1]
    shift_max = jnp.zeros((tq, 1), _F32)
    for p, h in enumerate(_DSA_HEAD_ORDER):
        qm = masked_half(bq_ref, h)
        shift = jnp.sqrt(jnp.sum(qm * qm, axis=1, keepdims=True)) * kmax
        shift_max = jnp.maximum(shift_max, shift)
        one_lane = one_even if h % 2 == 0 else one_odd
        bqm_ref[p * tq:(p + 1) * tq, :] = jnp.where(lane == one_lane, -shift, qm).astype(_BF16)
    bounded = jnp.max(shift_max) <= SHIFT_LIMIT

    w = [iwt_ref[h:h + 1, :] for h in range(IDX_HEADS)]
    def score_heads(k0, rows):
        return lax.dot_general(ik2_ref[pl.ds(k0, rows), :], iqm_ref[...], _NT, preferred_element_type=_F32)

    def score_piece(k0, rows, diagonal, z=None):
        z = score_heads(k0, rows) if z is None else z
        sc = w[0] * jnp.maximum(z[:, 0:LANES], 0.0)
        for h in range(1, IDX_HEADS):
            sc = sc + w[h] * jnp.maximum(z[:, LANES * h:LANES * (h + 1)], 0.0)
        if diagonal:
            kidx = k0 + lax.broadcasted_iota(_I32, (rows, LANES), 0)
            qidx = q0 + lax.broadcasted_iota(_I32, (rows, LANES), 1)
            sc = jnp.where(kidx <= qidx, sc, -jnp.inf)
        sc_ref[pl.ds(k0, rows), :] = sc
        top = lax.bitcast_convert_type(sc, _I32) & jnp.int32(-65536)
        sc16_ref[pl.ds(k0, rows), :] = lax.bitcast_convert_type(top, _F32).astype(_BF16)

    def score_two(c, carry):
        k0 = pl.multiple_of(c * 2 * kc, 2 * kc)
        za, zb = score_heads(k0, kc), score_heads(k0 + kc, kc)
        score_piece(k0, kc, False, za)
        score_piece(k0 + kc, kc, False, zb)
        return carry

    n_plain = n_whole - jnp.where(has_half, 0, 1)
    lax.fori_loop(0, n_plain // 2, score_two, 0)

    @pl.when(n_plain % 2 == 1)
    def _():
        score_piece(pl.multiple_of((n_plain - 1) * kc, kc), kc, False)

    @pl.when(has_half)
    def _():
        score_piece(pl.multiple_of(k_half, kh), kh, True)

    @pl.when(jnp.logical_not(has_half))
    def _():
        score_piece(pl.multiple_of((n_whole - 1) * kc, kc), kc, True)

    def count_ge(t):
        def body(k0, rows, acc):
            for h in range(0, rows, 512):
                ind = jnp.where(sc_ref[pl.ds(k0 + h, 512), :] >= t, 1, 0)
                acc = acc + jnp.sum(ind.reshape(64, 8, LANES), axis=0)
            return acc
        return jnp.sum(over_keys(body, jnp.zeros((8, LANES), _I32)), axis=0, keepdims=True)

    def count_ge16(t):
        t = t.astype(_BF16)

        def body(k0, rows, acc):
            d = pltpu.bitcast(sc16_ref[pl.ds(k0, rows), :] - t, _I32)
            neg = lax.shift_right_logical(d, 15) & 0x00010001
            return acc + jnp.sum(neg.reshape(rows // 16, 8, LANES), axis=0)
        acc = over_keys(body, jnp.zeros((8, LANES), _I32))
        below = jnp.sum((acc & 0xFFFF) + lax.shift_right_logical(acc, 16), axis=0, keepdims=True)
        return n_rows - below

    def search_pass(count, to_thr, st):
        lo, hi, clo, chi, t, thr = st
        t_up, t_dn = _midpoint(t, hi), _midpoint(lo, t)
        thr_up, thr_dn = to_thr(t_up), to_thr(t_dn)
        cnt = count(thr)
        ok = cnt >= n_sel
        return (jnp.where(ok, t, lo), jnp.where(ok, hi, t), jnp.where(ok, cnt, clo), jnp.where(ok, chi, cnt),
                jnp.where(ok, t_up, t_dn), jnp.where(ok, thr_up, thr_dn))

    vec = lambda v: jnp.full((1, LANES), v, _I32)
    st = (vec(KEY_NEG_INF >> 16), vec(1 << 15), vec(0) + n_rows, vec(0), vec(0), _key16_to_float(vec(0)))
    st = lax.fori_loop(0, 16, lambda j, s: search_pass(count_ge16, _key16_to_float, s), st)
    hint_lo = jnp.maximum(st[0] << 16, KEY_NEG_INF)
    hint_hi = jnp.where(st[1] >= (1 << 15), INT_MAX, st[1] << 16)
    hint_hi = jnp.where((hint_hi > 0) & (hint_hi < KEY_MIN_NORMAL), KEY_MIN_NORMAL, hint_hi)
    c1 = count_ge(_key_to_float(hint_lo))
    ok1 = c1 >= n_sel
    lo, clo = jnp.where(ok1, hint_lo, KEY_NEG_INF), jnp.where(ok1, c1, n_rows)
    hi, chi = jnp.where(ok1, INT_MAX, hint_lo), jnp.where(ok1, 0, c1)
    c2 = count_ge(_key_to_float(hint_hi))
    up2 = (c2 >= n_sel) & (hint_hi > lo)
    dn2 = (c2 < n_sel) & (hint_hi < hi)
    lo, clo = jnp.where(up2, hint_hi, lo), jnp.where(up2, c2, clo)
    hi, chi = jnp.where(dn2, hint_hi, hi), jnp.where(dn2, c2, chi)

    def unsettled(lo, hi, clo):
        return ~((clo == n_sel) | (hi - 1 <= lo) | ((lo >= 0) & (hi <= KEY_MIN_NORMAL)))

    def fine_group(c):
        s = c[1:7]
        for _ in range(SEARCH_GROUP):
            s = search_pass(count_ge, _key_to_float, s)
        return (c[0] + 1,) + s + (jnp.max(jnp.where(unsettled(*s[:3]), 1, 0)),)

    t0 = _midpoint(lo, hi)
    init = (jnp.int32(0), lo, hi, clo, chi, t0, _key_to_float(t0),
            jnp.max(jnp.where(unsettled(lo, hi, clo), 1, 0)))
    _, lo, hi, clo, chi, _, _, _ = lax.while_loop(lambda c: (c[7] > 0) & (c[0] < 16), fine_group, init)
    t_lo = _key_to_float(lo)
    t_hi = _key_to_float(hi)

    need = (clo > n_sel) & (lo > KEY_NEG_INF)

    @pl.when(jnp.max(jnp.where(need, 1, 0)) > 0)
    def _():
        tk = TIE_CHUNK
        room = (n_sel - chi).astype(_F32)
        before = (lax.broadcasted_iota(_I32, (tk, tk), 0) > lax.broadcasted_iota(_I32, (tk, tk), 1))
        before = jnp.where(before, 1.0, 0.0).astype(_BF16)

        def body(k0, rows, seen):
            for h in range(rows // tk):
                x = sc_ref[pl.ds(k0 + h * tk, tk), :]
                tie = (x >= t_lo) & ~(x >= t_hi)
                tf = jnp.where(tie, 1.0, 0.0)
                rank = seen + jnp.dot(before, tf.astype(_BF16), preferred_element_type=_F32)
                sc_ref[pl.ds(k0 + h * tk, tk), :] = jnp.where(tie & (rank >= room), -jnp.inf, x)
                seen = seen + jnp.sum(tf, axis=0, keepdims=True)
            return seen

        over_keys(body, jnp.zeros((1, LANES), _F32))

    thr = jnp.maximum(t_lo, F32_LOWEST)

    acc_e_ref[...] = jnp.zeros_like(acc_e_ref)
    acc_o_ref[...] = jnp.zeros_like(acc_o_ref)
    vt_rows = bvt_ref.shape[2]

    def masked_logits(piece):
        k0, rows = piece
        sel = sc_ref[pl.ds(k0, rows), :] >= thr
        lge = lax.dot_general(ka_ref[pl.ds(k0, rows), :], bqm_ref[0:half * tq, :], _NT,
                              preferred_element_type=_F32)
        lgo = lax.dot_general(kb_ref[pl.ds(k0, rows), :], bqm_ref[half * tq:nh * tq, :], _NT,
                              preferred_element_type=_F32)
        tiles = lambda lg: [jnp.where(sel, lg[:, LANES * j:LANES * (j + 1)], NEG_BIG) for j in range(half)]
        vt = jnp.concatenate([bvt_ref[k0 // vt_rows + j] for j in range(rows // vt_rows)], axis=1)
        return ((acc_e_ref, tiles(lge), vt[0:LANES, :]), (acc_o_ref, tiles(lgo), vt[LANES:2 * LANES, :]))

    @pl.when(bounded)
    def _():
        def weights(groups):
            return [(ref, jnp.concatenate([jnp.exp2(t).astype(_BF16) for t in tiles], axis=1), v)
                    for ref, tiles, v in groups]

        def body(k0, rows, carry, groups):
            for ref, p, v in groups:
                ref[...] += jnp.dot(v, p, preferred_element_type=_F32)
            return carry
        over_keys(body, jnp.int32(0), stages=(masked_logits, weights))

    @pl.when(jnp.logical_not(bounded))
    def _():
        def body(k0, rows, m):
            m_out = []
            for g, (ref, tiles, v) in enumerate(masked_logits((k0, rows))):
                ps, alphas = [], []
                for j, t in enumerate(tiles):
                    sl = slice(LANES * (g * half + j), LANES * (g * half + j + 1))
                    m_new = jnp.maximum(m[:, sl], jnp.max(t, axis=0, keepdims=True))
                    ps.append(jnp.exp2(t - m_new).astype(_BF16))
                    alphas.append(jnp.exp2(m[:, sl] - m_new))
                    m_out.append(m_new)
                ref[...] = ref[...] * jnp.concatenate(alphas, axis=1) + jnp.dot(
                    v, jnp.concatenate(ps, axis=1), preferred_element_type=_F32)
            return jnp.concatenate(m_out, axis=1)
        over_keys(body, jnp.full((1, nh * LANES), NEG_BIG, _F32))

    row = lax.broadcasted_iota(_I32, (LANES, LANES), 0)
    for j in range(half):
        sl = slice(LANES * j, LANES * (j + 1))
        even = acc_e_ref[:, sl] / acc_e_ref[HEAD_DIM:HEAD_DIM + 1, sl]
        odd = acc_o_ref[:, sl] / acc_o_ref[0:1, sl]
        tile = jnp.where(row < HEAD_DIM, even, odd)
        out_ref[:, sl] = (tile.T * g_ref[:, sl]).astype(out_ref.dtype)


def _dsa_attention(iq, bq, iwt, gates, ki2, bvt, batch, seq):
    nb = seq // BLOCK
    n = batch * seq
    nvt = seq // PROJ_ROWS
    n_sel = min(TOPK_MAX, seq // 4)
    blk = lambda w, col=0: pl.BlockSpec((BLOCK, w), lambda b, i: (b * nb + i, col))
    return pl.pallas_call(
        functools.partial(_dsa_kernel, n_sel=n_sel),
        grid=(batch, nb),
        in_specs=[blk(256), blk(512),
                  pl.BlockSpec((16, BLOCK), lambda b, i: (0, b * nb + i)),
                  blk(512, 1),
                  pl.BlockSpec((seq, LANES), lambda b, i: (b, 0)),
                  pl.BlockSpec((seq, LANES), lambda b, i: (b, 1)),
                  pl.BlockSpec((nvt, 256, PROJ_ROWS), lambda b, i: (b, 0, 0))],
        out_specs=blk(512),
        out_shape=jax.ShapeDtypeStruct((n, 512), _BF16),
        scratch_shapes=[pltpu.VMEM((seq, LANES), _F32),
                        pltpu.VMEM((seq, LANES), _BF16),
                        pltpu.VMEM((IDX_HEADS * BLOCK, LANES), _BF16),
                        pltpu.VMEM((DSA_Q_HEADS * BLOCK, LANES), _BF16),
                        pltpu.VMEM((seq, LANES), _BF16),
                        pltpu.VMEM((seq, LANES), _BF16),
                        pltpu.VMEM((8, LANES), _F32),
                        pltpu.VMEM((LANES, 512), _F32),
                        pltpu.VMEM((LANES, 512), _F32)],
        compiler_params=pltpu.CompilerParams(dimension_semantics=("arbitrary", "arbitrary"),
                                             vmem_limit_bytes=VMEM_LIMIT),
        name="dsa_attention",
    )(iq, bq, iwt, gates, ki2, ki2, bvt)


def _out_kernel(a_ref, b_ref, x_ref, w_ref, bo_ref, gain_ref, bias_ref, out_ref, *, alpha):
    half = a_ref.shape[1]
    y = jnp.dot(a_ref[...], w_ref[0:half, :], preferred_element_type=_F32)
    y = y + jnp.dot(b_ref[...], w_ref[half:2 * half, :], preferred_element_type=_F32)
    z = alpha * x_ref[...] + (y + bo_ref[...])
    mu = jnp.mean(z, axis=-1, keepdims=True)
    zc = z - mu
    var = jnp.mean(zc * zc, axis=-1, keepdims=True)
    out_ref[...] = zc * lax.rsqrt(var + LN_EPS) * gain_ref[...] + bias_ref[...]


def _out_projection(a, b, x2, w_out, b_out, gain, bias, alpha):
    n, d = x2.shape
    tm = PROJ_ROWS
    row = lambda w: pl.BlockSpec((tm, w), lambda i: (i, 0))
    full = lambda a_: pl.BlockSpec(a_.shape, lambda i: (0,) * a_.ndim)
    consts = (w_out.astype(_BF16), b_out[None, :], gain[None, :], bias[None, :])
    return pl.pallas_call(
        functools.partial(_out_kernel, alpha=alpha),
        grid=(n // tm,),
        in_specs=[row(a.shape[1]), row(b.shape[1]), row(d)] + [full(c) for c in consts],
        out_specs=row(d),
        out_shape=jax.ShapeDtypeStruct((n, d), x2.dtype),
        compiler_params=pltpu.CompilerParams(dimension_semantics=("arbitrary",),
                                             vmem_limit_bytes=VMEM_LIMIT),
        name="out_projection",
    )(a, b, x2, *consts)


def _layer(h, pos2, w_in, b_in, sinks, w_out, b_out, gain, bias, alpha):
    batch, seq, d = h.shape
    x2 = h.reshape(batch * seq, d)
    aq, ak2, bq, iq, ki2, gates, bvt, avt, iwt = _projection(x2, pos2, w_in, b_in)
    a = _swa_attention(aq, ak2, avt, gates, sinks, batch, seq)
    b = _dsa_attention(iq, bq, iwt, gates, ki2, bvt, batch, seq)
    return _out_projection(a, b, x2, w_out, b_out, gain, bias, alpha).reshape(batch, seq, d)


def kernel(x, positions, w_in, b_in, swa_sinks, w_out, b_out, ln_gain, ln_bias):
    batch, seq, d = x.shape
    depth = w_in.shape[0]
    assert d == 1024 and seq % KEY_CHUNK == 0 and (batch * seq) % PROJ_ROWS == 0
    assert KEY_CHUNK % PROJ_ROWS == 0 and KEY_CHUNK % TIE_CHUNK == 0
    alpha = (2.0 * depth) ** 0.25
    pos2 = positions.reshape(batch * seq, 1)
    h = x
    for layer in range(depth):
        h = _layer(h, pos2, w_in[layer], b_in[layer], swa_sinks[layer], w_out[layer], b_out[layer],
                   ln_gain[layer], ln_bias[layer], alpha)
    return h
```

```python
import functools

import numpy as np
import jax
import jax.numpy as jnp
from jax import lax
from jax.experimental import pallas as pl
from jax.experimental.pallas import tpu as pltpu

HEAD_DIM = 64
SWA_Q_HEADS = 8
SWA_KV_HEADS = 2
DSA_Q_HEADS = 8
IDX_HEADS = 4
IDX_DIM = 64
WINDOW = 128
BLOCK = 128
TOPK_MAX = 256
ROPE_THETA = 10000.0
LN_EPS = 1e-5

LANES = 128
PROJ_ROWS = 512
KEY_CHUNK = 1024
TIE_CHUNK = 512
VMEM_LIMIT = 56 * 1024 * 1024

LOG2E = 1.4426950408889634
NEG_BIG = -1e30
SHIFT_LIMIT = 60.0
F32_LOWEST = -3.4028234663852886e38
INT_MAX = 2 ** 31 - 1
KEY_NEG_INF = 0x807FFFFF - 2 ** 32
KEY_MIN_NORMAL = 0x00800000
SEARCH_GROUP = 3

_F32 = jnp.float32
_BF16 = jnp.bfloat16
_I32 = jnp.int32
_NT = (((1,), (1,)), ((), ()))


def _split_columns(m):
    swa_w = SWA_Q_HEADS * HEAD_DIM
    kv_w = SWA_KV_HEADS * HEAD_DIM
    dsa_w = DSA_Q_HEADS * HEAD_DIM
    sizes = (swa_w, kv_w, kv_w, swa_w, dsa_w, HEAD_DIM, HEAD_DIM, dsa_w,
             IDX_HEADS * IDX_DIM, IDX_DIM, IDX_HEADS)
    o = np.cumsum((0,) + sizes)
    aq, ak, av, ag, bq, bk, bv, bg, iq, ik, iw = [m[:, o[i]:o[i + 1]] for i in range(len(sizes))]

    def dup(c):
        return jnp.concatenate([c[:, j:j + HEAD_DIM] for j in range(0, c.shape[1], HEAD_DIM) for _ in (0, 1)],
                               axis=1)

    def pairs(c):
        r = c.reshape(c.shape[0], -1, 2, 2, HEAD_DIM // 2)
        return jnp.swapaxes(r, 2, 3).reshape(c.shape[0], -1)

    roped = pairs(jnp.concatenate([aq, dup(ak), bq, iq, dup(bk), dup(ik)], axis=1))
    return jnp.concatenate([roped, ag, bg], axis=1), (bv, av[:, :HEAD_DIM], av[:, HEAD_DIM:]), iw


def _proj_kernel(x_ref, pos_ref, inv_ref, sgn_ref, wm_ref, bm_ref,
                 wvt_ref, bvt_ref, wiw_ref, biw_ref,
                 aq_ref, ak2_ref, bq_ref, iq_ref, ki2_ref, g_ref, bvt_out_ref, avt_out_ref, iwt_ref):
    xb = x_ref[...].astype(_BF16)
    ang = pos_ref[...].astype(_F32) * inv_ref[...]
    cos = jnp.cos(ang)
    sin = jnp.sin(ang) * sgn_ref[...]

    q_scale = 0.125 * LOG2E
    roped_dst = ((aq_ref, 0, q_scale), (aq_ref, 256, q_scale), (ak2_ref, 0, 1.0), (bq_ref, 0, q_scale),
                 (bq_ref, 256, q_scale), (iq_ref, 0, 1.0), (ki2_ref, 0, 1.0))
    for g, (dst, off, scale) in enumerate(roped_dst):
        c0 = 256 * g
        hm = jnp.dot(xb, wm_ref[:, c0:c0 + 256], preferred_element_type=_F32) + bm_ref[:, c0:c0 + 256]
        for s in range(2):
            sl = slice(LANES * s, LANES * (s + 1))
            o = hm[:, sl] * cos + pltpu.roll(hm[:, sl], LANES // 2, axis=1) * sin
            if scale != 1.0:
                o = o * scale
            dst[:, off + LANES * s:off + LANES * (s + 1)] = o.astype(dst.dtype)

    n_roped = 256 * len(roped_dst)
    for g in range(4):
        c0 = n_roped + 256 * g
        h = jnp.dot(xb, wm_ref[:, c0:c0 + 256], preferred_element_type=_F32) + bm_ref[:, c0:c0 + 256]
        g_ref[:, 256 * g:256 * (g + 1)] = h * (1.0 / (1.0 + jnp.exp(-h)))

    vt = lax.dot_general(wvt_ref[...], xb, _NT, preferred_element_type=_F32) + bvt_ref[...]
    n_b = bvt_out_ref.shape[1]
    bvt_out_ref[0] = vt[0:n_b].astype(bvt_out_ref.dtype)
    avt_out_ref[0] = vt[n_b:].astype(avt_out_ref.dtype)
    iwt = lax.dot_general(wiw_ref[...], xb, _NT, preferred_element_type=_F32) + biw_ref[...]
    iwt_ref[...] = iwt * (IDX_HEADS ** -0.5 * IDX_DIM ** -0.5)


def _projection(x2, pos2, w_in, b_in):
    n, d = x2.shape
    tm = PROJ_ROWS
    wm, wvs, wiw = _split_columns(w_in)
    bm, bvs, biw = _split_columns(b_in[None, :])
    wm = wm.astype(_BF16)
    zpad = jnp.zeros((HEAD_DIM, d), w_in.dtype)
    one_row = jnp.zeros((HEAD_DIM,), b_in.dtype).at[0].set(1.0)
    wvt = jnp.concatenate([p for wv in wvs for p in (wv.T, zpad, zpad, wv.T)], axis=0).astype(_BF16)
    bvt = jnp.concatenate([p for bv in bvs for p in (bv[0], one_row, one_row, bv[0])])[:, None]
    wiw_t = jnp.concatenate([wiw.T, jnp.zeros((16 - IDX_HEADS, d), w_in.dtype)], axis=0).astype(_BF16)
    biw_t = jnp.concatenate([biw[0], jnp.zeros((16 - IDX_HEADS,), b_in.dtype)])[:, None]
    half = HEAD_DIM // 2
    inv = ROPE_THETA ** (-jnp.arange(0, HEAD_DIM, 2, dtype=_F32) / HEAD_DIM)
    inv128 = jnp.tile(inv, LANES // half)[None, :]
    sgn128 = jnp.concatenate([-jnp.ones((LANES // 2,), _F32), jnp.ones((LANES // 2,), _F32)])[None, :]

    row = lambda w: pl.BlockSpec((tm, w), lambda i: (i, 0))
    full = lambda a: pl.BlockSpec(a.shape, lambda i: (0,) * a.ndim)
    out_shape = (
        jax.ShapeDtypeStruct((n, 512), _BF16),
        jax.ShapeDtypeStruct((n, 256), _BF16),
        jax.ShapeDtypeStruct((n, 512), _BF16),
        jax.ShapeDtypeStruct((n, 256), _BF16),
        jax.ShapeDtypeStruct((n, 256), _BF16),
        jax.ShapeDtypeStruct((n, 1024), _F32),
        jax.ShapeDtypeStruct((n // tm, 256, tm), _BF16),
        jax.ShapeDtypeStruct((n // tm, 512, tm), _BF16),
        jax.ShapeDtypeStruct((16, n), _F32),
    )
    out_specs = (row(512), row(256), row(512), row(256), row(256), row(1024),
                 pl.BlockSpec((1, 256, tm), lambda i: (i, 0, 0)),
                 pl.BlockSpec((1, 512, tm), lambda i: (i, 0, 0)),
                 pl.BlockSpec((16, tm), lambda i: (0, i)))
    consts = (inv128, sgn128, wm, bm, wvt, bvt, wiw_t, biw_t)
    return pl.pallas_call(
        _proj_kernel,
        grid=(n // tm,),
        in_specs=[row(d), row(1)] + [full(a) for a in consts],
        out_specs=out_specs,
        out_shape=out_shape,
        compiler_params=pltpu.CompilerParams(dimension_semantics=("arbitrary",),
                                             vmem_limit_bytes=VMEM_LIMIT),
        name="in_projection",
    )(x2, pos2, *consts)


def _swa_kernel(q_ref, kc_ref, kp_ref, vc_ref, vp_ref, g_ref, sink_ref, out_ref):
    i = pl.program_id(1)
    tq = BLOCK
    n_sub = q_ref.shape[0] // tq
    group = SWA_Q_HEADS // SWA_KV_HEADS
    lane = lax.broadcasted_iota(_I32, (tq, LANES), 1)
    lo_half = (lane & (HEAD_DIM // 2)) == 0
    r = lax.broadcasted_iota(_I32, (2 * tq, LANES), 0)
    c = lax.broadcasted_iota(_I32, (2 * tq, LANES), 1)
    in_window = (c < r) & (r <= c + WINDOW)
    row = lax.broadcasted_iota(_I32, (LANES, LANES), 0)

    kwin = jnp.concatenate([kp_ref[...], kc_ref[...]], axis=0)
    vwin = jnp.concatenate([vp_ref[0], vc_ref[0]], axis=1)
    combos = [(g, j, odd) for g in range(SWA_KV_HEADS) for j in range(n_sub) for odd in (0, 1)]
    rows = lambda j: slice(tq * j, tq * (j + 1))
    win = lambda j: slice(tq * j, tq * (j + 2))
    slab_of = lambda g, s: slice(LANES * (g * (group // 2) + s), LANES * (g * (group // 2) + s + 1))

    logits = {}
    for g, j, odd in combos:
        qs = [jnp.where(lo_half if odd == 0 else ~lo_half, q_ref[rows(j), slab_of(g, s)].astype(_F32),
                        0.0).astype(_BF16) for s in range(group // 2)]
        logits[g, j, odd] = lax.dot_general(kwin[win(j), LANES * g:LANES * (g + 1)], jnp.concatenate(qs, axis=0),
                                            _NT, preferred_element_type=_F32)
    probs = {}
    for g, j, odd in combos:
        valid = in_window if j > 0 else in_window & ((r >= tq) | (i > 0))
        ps, sink_terms = [], []
        for s in range(group // 2):
            lg = jnp.where(valid, logits[g, j, odd][:, LANES * s:LANES * (s + 1)], -jnp.inf)
            sink = sink_ref[group * g + 2 * s + odd:group * g + 2 * s + odd + 1, :]
            m = jnp.maximum(jnp.max(lg, axis=0, keepdims=True), sink)
            ps.append(jnp.exp2(lg - m).astype(_BF16))
            sink_terms.append(jnp.exp2(sink - m))
        probs[g, j, odd] = jnp.concatenate(ps, axis=1), jnp.concatenate(sink_terms, axis=1)
    tiles = {}
    for g, j, odd in combos:
        p, sink_term = probs[g, j, odd]
        v = vwin[LANES * (2 * g + odd):LANES * (2 * g + odd + 1), win(j)]
        acc = jnp.dot(v, p, preferred_element_type=_F32)
        l_row = HEAD_DIM if odd == 0 else 0
        tiles[g, j, odd] = acc / (acc[l_row:l_row + 1, :] + sink_term)
    for g in range(SWA_KV_HEADS):
        for j in range(n_sub):
            for s in range(group // 2):
                sl = slice(LANES * s, LANES * (s + 1))
                tile = jnp.where(row < HEAD_DIM, tiles[g, j, 0][:, sl], tiles[g, j, 1][:, sl])
                out_ref[rows(j), slab_of(g, s)] = (tile.T * g_ref[rows(j), slab_of(g, s)]).astype(out_ref.dtype)


def _swa_attention(aq, ak2, avt, gates, sinks, batch, seq):
    tm = PROJ_ROWS
    ns = seq // tm
    n = batch * seq
    per = tm // BLOCK
    sink_b = jnp.broadcast_to((sinks.astype(_F32) * LOG2E)[:, None], (SWA_Q_HEADS, LANES))
    cur = lambda w: pl.BlockSpec((tm, w), lambda b, i: (b * ns + i, 0))
    return pl.pallas_call(
        _swa_kernel,
        grid=(batch, ns),
        in_specs=[cur(512), cur(256),
                  pl.BlockSpec((BLOCK, 256), lambda b, i: (b * ns * per + jnp.maximum(i * per - 1, 0), 0)),
                  pl.BlockSpec((1, 512, tm), lambda b, i: (b * ns + i, 0, 0)),
                  pl.BlockSpec((1, 512, BLOCK), lambda b, i: (b * ns + jnp.maximum(i - 1, 0), 0, per - 1)),
                  cur(512),
                  pl.BlockSpec((SWA_Q_HEADS, LANES), lambda b, i: (0, 0))],
        out_specs=cur(512),
        out_shape=jax.ShapeDtypeStruct((n, 512), _BF16),
        compiler_params=pltpu.CompilerParams(dimension_semantics=("arbitrary", "arbitrary"),
                                             vmem_limit_bytes=VMEM_LIMIT),
        name="swa_attention",
    )(aq, ak2, ak2, avt, avt, gates, sink_b)


_DSA_HEAD_ORDER = (0, 2, 4, 6, 1, 3, 5, 7)


def _key_to_float(k):
    return lax.bitcast_convert_type(jnp.where(k < 0, k ^ 0x7FFFFFFF, k), _F32)


def _key16_to_float(k):
    return lax.bitcast_convert_type(jnp.where(k <= 0, k ^ 0x7FFF ^ jnp.where(k == 0, -1, 0), k) << 16, _F32)


def _midpoint(a, b):
    return (a >> 1) + (b >> 1) + (a & b & 1)


def _tree_sum(parts, chains=8):
    accs = list(parts[:chains])
    for j in range(chains, len(parts)):
        accs[j % chains] = accs[j % chains] + parts[j]
    while len(accs) > 1:
        accs = [accs[2 * j] + accs[2 * j + 1] for j in range(len(accs) // 2)]
    return accs[0]


def _dsa_kernel(iq_ref, bq_ref, iwt_ref, g_ref, bk2_ref, ik2_ref, bvt_ref, out_ref,
                sc_ref, sc16_ref, iqm_ref, bqm_ref, ka_ref, kb_ref, kmax_ref, acc_e_ref, acc_o_ref, *, n_sel):
    tq, kc = BLOCK, KEY_CHUNK
    nh, half = DSA_Q_HEADS, DSA_Q_HEADS // 2
    i = pl.program_id(1)
    q0 = i * tq
    lane = lax.broadcasted_iota(_I32, (tq, LANES), 1)
    lo_half = (lane & (HEAD_DIM // 2)) == 0
    one_even, one_odd = HEAD_DIM // 2, 0

    kh, kq = kc // 2, kc // 4
    rem = (q0 + tq) % kc
    n_whole = (q0 + tq) // kc + jnp.where(rem > kh + kq, 1, 0)
    has_half = (rem > kq) & (rem <= kh + kq)
    has_quarter = ((rem > 0) & (rem <= kq)) | ((rem > kh) & (rem <= kh + kq))
    k_half = n_whole * kc
    k_quarter = k_half + jnp.where(has_half, kh, 0)
    n_rows = k_quarter + jnp.where(has_quarter, kq, 0)

    def over_keys(body, carry, stages=()):
        def prepared(pieces):
            states = list(pieces)
            for stage in stages:
                states = [stage(s) for s in states]
            return states

        def run(pieces, cr):
            for (k0, rows), state in zip(pieces, prepared(pieces)):
                cr = body(k0, rows, cr, state) if stages else body(k0, rows, cr)
            return cr

        def two(c, cr):
            k0 = pl.multiple_of(c * 2 * kc, 2 * kc)
            return run([(k0, kc), (k0 + kc, kc)], cr)

        carry = lax.fori_loop(0, n_whole // 2, two, carry)
        carry = lax.cond(n_whole % 2 == 1, lambda cr: run([(pl.multiple_of((n_whole - 1) * kc, kc), kc)], cr),
                         lambda cr: cr, carry)
        carry = lax.cond(has_half, lambda cr: run([(pl.multiple_of(k_half, kh), kh)], cr), lambda cr: cr, carry)
        return lax.cond(has_quarter, lambda cr: run([(pl.multiple_of(k_quarter, kh), kq)], cr), lambda cr: cr, carry)

    @pl.when(i == 0)
    def _():
        klane = lax.broadcasted_iota(_I32, (kc, LANES), 1)
        klo = (klane & (HEAD_DIM // 2)) == 0

        def body(c, mx):
            k0 = pl.multiple_of(c * kc, kc)
            k = bk2_ref[pl.ds(k0, kc), :].astype(_F32)
            ka_ref[pl.ds(k0, kc), :] = jnp.where(klane == one_even, 1.0, k).astype(_BF16)
            kb_ref[pl.ds(k0, kc), :] = jnp.where(klane == one_odd, 1.0, k).astype(_BF16)
            return jnp.maximum(mx, jnp.sum(jnp.where(klo, k * k, 0.0), axis=1, keepdims=True))

        mx = lax.fori_loop(0, sc_ref.shape[0] // kc, body, jnp.zeros((kc, 1), _F32))
        kmax_ref[...] = jnp.broadcast_to(jnp.sqrt(jnp.max(mx, axis=0, keepdims=True)), kmax_ref.shape)

    def masked_half(ref, h):
        slab = ref[:, LANES * (h // 2):LANES * (h // 2 + 1)].astype(_F32)
        return jnp.where(lo_half if h % 2 == 0 else ~lo_half, slab, 0.0)

    for h in range(IDX_HEADS):
        iqm_ref[h * tq:(h + 1) * tq, :] = masked_half(iq_ref, h).astype(_BF16)
    kmax = kmax_ref[0:1, 0:1]
    shift_max = jnp.zeros((tq, 1), _F32)
    for p, h in enumerate(_DSA_HEAD_ORDER):
        qm = masked_half(bq_ref, h)
        shift = jnp.sqrt(jnp.sum(qm * qm, axis=1, keepdims=True)) * kmax
        shift_max = jnp.maximum(shift_max, shift)
        one_lane = one_even if h % 2 == 0 else one_odd
        bqm_ref[p * tq:(p + 1) * tq, :] = jnp.where(lane == one_lane, -shift, qm).astype(_BF16)
    bounded = jnp.max(shift_max) <= SHIFT_LIMIT

    w = [iwt_ref[h:h + 1, :] for h in range(IDX_HEADS)]
    def score_heads(k0, rows):
        return lax.dot_general(ik2_ref[pl.ds(k0, rows), :], iqm_ref[...], _NT, preferred_element_type=_F32)

    def score_piece(k0, rows, diagonal, z=None):
        z = score_heads(k0, rows) if z is None else z
        sc = w[0] * jnp.maximum(z[:, 0:LANES], 0.0)
        for h in range(1, IDX_HEADS):
            sc = sc + w[h] * jnp.maximum(z[:, LANES * h:LANES * (h + 1)], 0.0)
        if diagonal:
            kidx = k0 + lax.broadcasted_iota(_I32, (rows, LANES), 0)
            qidx = q0 + lax.broadcasted_iota(_I32, (rows, LANES), 1)
            sc = jnp.where(kidx <= qidx, sc, -jnp.inf)
        sc_ref[pl.ds(k0, rows), :] = sc
        top = lax.bitcast_convert_type(sc, _I32) & jnp.int32(-65536)
        sc16_ref[pl.ds(k0, rows), :] = lax.bitcast_convert_type(top, _F32).astype(_BF16)

    def score_two(c, carry):
        k0 = pl.multiple_of(c * 2 * kc, 2 * kc)
        za, zb = score_heads(k0, kc), score_heads(k0 + kc, kc)
        score_piece(k0, kc, False, za)
        score_piece(k0 + kc, kc, False, zb)
        return carry

    any_tail = has_half | has_quarter
    n_plain = n_whole - jnp.where(any_tail, 0, 1)
    lax.fori_loop(0, n_plain // 2, score_two, 0)

    @pl.when(n_plain % 2 == 1)
    def _():
        score_piece(pl.multiple_of((n_plain - 1) * kc, kc), kc, False)

    @pl.when(has_half)
    def _():
        score_piece(pl.multiple_of(k_half, kh), kh, True)

    @pl.when(has_quarter)
    def _():
        score_piece(pl.multiple_of(k_quarter, kh), kq, True)

    @pl.when(jnp.logical_not(any_tail))
    def _():
        score_piece(pl.multiple_of((n_whole - 1) * kc, kc), kc, True)

    def count_ge(t):
        def body(k0, rows, acc):
            step = min(rows, 512)
            for h in range(0, rows, step):
                ind = jnp.where(sc_ref[pl.ds(k0 + h, step), :] >= t, 1, 0)
                acc = acc + jnp.sum(ind.reshape(step // 8, 8, LANES), axis=0)
            return acc
        return jnp.sum(over_keys(body, jnp.zeros((8, LANES), _I32)), axis=0, keepdims=True)

    def count_ge16(t):
        t = t.astype(_BF16)

        def body(k0, rows, acc):
            d = pltpu.bitcast(sc16_ref[pl.ds(k0, rows), :] - t, _I32)
            neg = lax.shift_right_logical(d, 15) & 0x00010001
            return acc + jnp.sum(neg.reshape(rows // 16, 8, LANES), axis=0)
        acc = over_keys(body, jnp.zeros((8, LANES), _I32))
        below = jnp.sum((acc & 0xFFFF) + lax.shift_right_logical(acc, 16), axis=0, keepdims=True)
        return n_rows - below

    def search_pass(count, to_thr, st):
        lo, hi, clo, chi, t, thr = st
        t_up, t_dn = _midpoint(t, hi), _midpoint(lo, t)
        thr_up, thr_dn = to_thr(t_up), to_thr(t_dn)
        cnt = count(thr)
        ok = cnt >= n_sel
        return (jnp.where(ok, t, lo), jnp.where(ok, hi, t), jnp.where(ok, cnt, clo), jnp.where(ok, chi, cnt),
                jnp.where(ok, t_up, t_dn), jnp.where(ok, thr_up, thr_dn))

    vec = lambda v: jnp.full((1, LANES), v, _I32)
    st = (vec(KEY_NEG_INF >> 16), vec(1 << 15), vec(0) + n_rows, vec(0), vec(0), _key16_to_float(vec(0)))
    st = lax.fori_loop(0, 16, lambda j, s: search_pass(count_ge16, _key16_to_float, s), st)
    hint_lo = jnp.maximum(st[0] << 16, KEY_NEG_INF)
    hint_hi = jnp.where(st[1] >= (1 << 15), INT_MAX, st[1] << 16)
    hint_hi = jnp.where((hint_hi > 0) & (hint_hi < KEY_MIN_NORMAL), KEY_MIN_NORMAL, hint_hi)
    c1 = count_ge(_key_to_float(hint_lo))
    ok1 = c1 >= n_sel
    lo, clo = jnp.where(ok1, hint_lo, KEY_NEG_INF), jnp.where(ok1, c1, n_rows)
    hi, chi = jnp.where(ok1, INT_MAX, hint_lo), jnp.where(ok1, 0, c1)
    c2 = count_ge(_key_to_float(hint_hi))
    up2 = (c2 >= n_sel) & (hint_hi > lo)
    dn2 = (c2 < n_sel) & (hint_hi < hi)
    lo, clo = jnp.where(up2, hint_hi, lo), jnp.where(up2, c2, clo)
    hi, chi = jnp.where(dn2, hint_hi, hi), jnp.where(dn2, c2, chi)

    def unsettled(lo, hi, clo):
        return ~((clo == n_sel) | (hi - 1 <= lo) | ((lo >= 0) & (hi <= KEY_MIN_NORMAL)))

    def fine_group(c):
        s = c[1:7]
        for _ in range(SEARCH_GROUP):
            s = search_pass(count_ge, _key_to_float, s)
        return (c[0] + 1,) + s + (jnp.max(jnp.where(unsettled(*s[:3]), 1, 0)),)

    t0 = _midpoint(lo, hi)
    init = (jnp.int32(0), lo, hi, clo, chi, t0, _key_to_float(t0),
            jnp.max(jnp.where(unsettled(lo, hi, clo), 1, 0)))
    _, lo, hi, clo, chi, _, _, _ = lax.while_loop(lambda c: (c[7] > 0) & (c[0] < 16), fine_group, init)
    t_lo = _key_to_float(lo)
    t_hi = _key_to_float(hi)

    need = (clo > n_sel) & (lo > KEY_NEG_INF)

    @pl.when(jnp.max(jnp.where(need, 1, 0)) > 0)
    def _():
        tk = TIE_CHUNK
        room = (n_sel - chi).astype(_F32)
        before = (lax.broadcasted_iota(_I32, (tk, tk), 0) > lax.broadcasted_iota(_I32, (tk, tk), 1))
        before = jnp.where(before, 1.0, 0.0).astype(_BF16)

        def body(k0, rows, seen):
            sub = min(rows, tk)
            for h in range(rows // sub):
                x = sc_ref[pl.ds(k0 + h * sub, sub), :]
                tie = (x >= t_lo) & ~(x >= t_hi)
                tf = jnp.where(tie, 1.0, 0.0)
                rank = seen + jnp.dot(before[0:sub, 0:sub], tf.astype(_BF16), preferred_element_type=_F32)
                sc_ref[pl.ds(k0 + h * sub, sub), :] = jnp.where(tie & (rank >= room), -jnp.inf, x)
                seen = seen + jnp.sum(tf, axis=0, keepdims=True)
            return seen

        over_keys(body, jnp.zeros((1, LANES), _F32))

    thr = jnp.maximum(t_lo, F32_LOWEST)

    acc_e_ref[...] = jnp.zeros_like(acc_e_ref)
    acc_o_ref[...] = jnp.zeros_like(acc_o_ref)
    vt_rows = bvt_ref.shape[2]

    def masked_logits(piece):
        k0, rows = piece
        sel = sc_ref[pl.ds(k0, rows), :] >= thr
        lge = lax.dot_general(ka_ref[pl.ds(k0, rows), :], bqm_ref[0:half * tq, :], _NT,
                              preferred_element_type=_F32)
        lgo = lax.dot_general(kb_ref[pl.ds(k0, rows), :], bqm_ref[half * tq:nh * tq, :], _NT,
                              preferred_element_type=_F32)
        tiles = lambda lg: [jnp.where(sel, lg[:, LANES * j:LANES * (j + 1)], NEG_BIG) for j in range(half)]
        if rows < vt_rows:
            vt = bvt_ref[k0 // vt_rows][:, 0:rows]
        else:
            vt = jnp.concatenate([bvt_ref[k0 // vt_rows + j] for j in range(rows // vt_rows)], axis=1)
        return ((acc_e_ref, tiles(lge), vt[0:LANES, :]), (acc_o_ref, tiles(lgo), vt[LANES:2 * LANES, :]))

    @pl.when(bounded)
    def _():
        def weights(groups):
            return [(ref, jnp.concatenate([jnp.exp2(t).astype(_BF16) for t in tiles], axis=1), v)
                    for ref, tiles, v in groups]

        def body(k0, rows, carry, groups):
            for ref, p, v in groups:
                ref[...] += jnp.dot(v, p, preferred_element_type=_F32)
            return carry
        over_keys(body, jnp.int32(0), stages=(masked_logits, weights))

    @pl.when(jnp.logical_not(bounded))
    def _():
        def body(k0, rows, m):
            m_out = []
            for g, (ref, tiles, v) in enumerate(masked_logits((k0, rows))):
                ps, alphas = [], []
                for j, t in enumerate(tiles):
                    sl = slice(LANES * (g * half + j), LANES * (g * half + j + 1))
                    m_new = jnp.maximum(m[:, sl], jnp.max(t, axis=0, keepdims=True))
                    ps.append(jnp.exp2(t - m_new).astype(_BF16))
                    alphas.append(jnp.exp2(m[:, sl] - m_new))
                    m_out.append(m_new)
                ref[...] = ref[...] * jnp.concatenate(alphas, axis=1) + jnp.dot(
                    v, jnp.concatenate(ps, axis=1), preferred_element_type=_F32)
            return jnp.concatenate(m_out, axis=1)
        over_keys(body, jnp.full((1, nh * LANES), NEG_BIG, _F32))

    row = lax.broadcasted_iota(_I32, (LANES, LANES), 0)
    for j in range(half):
        sl = slice(LANES * j, LANES * (j + 1))
        even = acc_e_ref[:, sl] / acc_e_ref[HEAD_DIM:HEAD_DIM + 1, sl]
        odd = acc_o_ref[:, sl] / acc_o_ref[0:1, sl]
        tile = jnp.where(row < HEAD_DIM, even, odd)
        out_ref[:, sl] = (tile.T * g_ref[:, sl]).astype(out_ref.dtype)


def _dsa_attention(iq, bq, iwt, gates, ki2, bvt, batch, seq):
    nb = seq // BLOCK
    n = batch * seq
    nvt = seq // PROJ_ROWS
    n_sel = min(TOPK_MAX, seq // 4)
    blk = lambda w, col=0: pl.BlockSpec((BLOCK, w), lambda b, i: (b * nb + i, col))
    return pl.pallas_call(
        functools.partial(_dsa_kernel, n_sel=n_sel),
        grid=(batch, nb),
        in_specs=[blk(256), blk(512),
                  pl.BlockSpec((16, BLOCK), lambda b, i: (0, b * nb + i)),
                  blk(512, 1),
                  pl.BlockSpec((seq, LANES), lambda b, i: (b, 0)),
                  pl.BlockSpec((seq, LANES), lambda b, i: (b, 1)),
                  pl.BlockSpec((nvt, 256, PROJ_ROWS), lambda b, i: (b, 0, 0))],
        out_specs=blk(512),
        out_shape=jax.ShapeDtypeStruct((n, 512), _BF16),
        scratch_shapes=[pltpu.VMEM((seq, LANES), _F32),
                        pltpu.VMEM((seq, LANES), _BF16),
                        pltpu.VMEM((IDX_HEADS * BLOCK, LANES), _BF16),
                        pltpu.VMEM((DSA_Q_HEADS * BLOCK, LANES), _BF16),
                        pltpu.VMEM((seq, LANES), _BF16),
                        pltpu.VMEM((seq, LANES), _BF16),
                        pltpu.VMEM((8, LANES), _F32),
                        pltpu.VMEM((LANES, 512), _F32),
                        pltpu.VMEM((LANES, 512), _F32)],
        compiler_params=pltpu.CompilerParams(dimension_semantics=("arbitrary", "arbitrary"),
                                             vmem_limit_bytes=VMEM_LIMIT),
        name="dsa_attention",
    )(iq, bq, iwt, gates, ki2, ki2, bvt)


def _out_kernel(a_ref, b_ref, x_ref, w_ref, bo_ref, gain_ref, bias_ref, out_ref, *, alpha):
    half = a_ref.shape[1]
    y = jnp.dot(a_ref[...], w_ref[0:half, :], preferred_element_type=_F32)
    y = y + jnp.dot(b_ref[...], w_ref[half:2 * half, :], preferred_element_type=_F32)
    z = alpha * x_ref[...] + (y + bo_ref[...])
    mu = jnp.mean(z, axis=-1, keepdims=True)
    zc = z - mu
    var = jnp.mean(zc * zc, axis=-1, keepdims=True)
    out_ref[...] = zc * lax.rsqrt(var + LN_EPS) * gain_ref[...] + bias_ref[...]


def _out_projection(a, b, x2, w_out, b_out, gain, bias, alpha):
    n, d = x2.shape
    tm = PROJ_ROWS
    row = lambda w: pl.BlockSpec((tm, w), lambda i: (i, 0))
    full = lambda a_: pl.BlockSpec(a_.shape, lambda i: (0,) * a_.ndim)
    consts = (w_out.astype(_BF16), b_out[None, :], gain[None, :], bias[None, :])
    return pl.pallas_call(
        functools.partial(_out_kernel, alpha=alpha),
        grid=(n // tm,),
        in_specs=[row(a.shape[1]), row(b.shape[1]), row(d)] + [full(c) for c in consts],
        out_specs=row(d),
        out_shape=jax.ShapeDtypeStruct((n, d), x2.dtype),
        compiler_params=pltpu.CompilerParams(dimension_semantics=("arbitrary",),
                                             vmem_limit_bytes=VMEM_LIMIT),
        name="out_projection",
    )(a, b, x2, *consts)


def _layer(h, pos2, w_in, b_in, sinks, w_out, b_out, gain, bias, alpha):
    batch, seq, d = h.shape
    x2 = h.reshape(batch * seq, d)
    aq, ak2, bq, iq, ki2, gates, bvt, avt, iwt = _projection(x2, pos2, w_in, b_in)
    a = _swa_attention(aq, ak2, avt, gates, sinks, batch, seq)
    b = _dsa_attention(iq, bq, iwt, gates, ki2, bvt, batch, seq)
    return _out_projection(a, b, x2, w_out, b_out, gain, bias, alpha).reshape(batch, seq, d)


def kernel(x, positions, w_in, b_in, swa_sinks, w_out, b_out, ln_gain, ln_bias):
    batch, seq, d = x.shape
    depth = w_in.shape[0]
    assert d == 1024 and seq % KEY_CHUNK == 0 and (batch * seq) % PROJ_ROWS == 0
    assert KEY_CHUNK % PROJ_ROWS == 0 and KEY_CHUNK % TIE_CHUNK == 0
    alpha = (2.0 * depth) ** 0.25
    pos2 = positions.reshape(batch * seq, 1)
    h = x
    for layer in range(depth):
        h = _layer(h, pos2, w_in[layer], b_in[layer], swa_sinks[layer], w_out[layer], b_out[layer],
                   ln_gain[layer], ln_bias[layer], alpha)
    return h
```

```python
import functools

import numpy as np
import jax
import jax.numpy as jnp
from jax import lax
from jax.experimental import pallas as pl
from jax.experimental.pallas import tpu as pltpu

HEAD_DIM = 64
SWA_Q_HEADS = 8
SWA_KV_HEADS = 2
DSA_Q_HEADS = 8
IDX_HEADS = 4
IDX_DIM = 64
WINDOW = 128
BLOCK = 128
TOPK_MAX = 256
ROPE_THETA = 10000.0
LN_EPS = 1e-5

LANES = 128
PROJ_ROWS = 512
KEY_CHUNK = 1024
TIE_CHUNK = 512
VMEM_LIMIT = 56 * 1024 * 1024

LOG2E = 1.4426950408889634
NEG_BIG = -1e30
SHIFT_LIMIT = 60.0
F32_LOWEST = -3.4028234663852886e38
INT_MAX = 2 ** 31 - 1
KEY_NEG_INF = 0x807FFFFF - 2 ** 32
KEY_MIN_NORMAL = 0x00800000
SEARCH_GROUP = 3

_F32 = jnp.float32
_BF16 = jnp.bfloat16
_I32 = jnp.int32
_NT = (((1,), (1,)), ((), ()))


def _split_columns(m):
    swa_w = SWA_Q_HEADS * HEAD_DIM
    kv_w = SWA_KV_HEADS * HEAD_DIM
    dsa_w = DSA_Q_HEADS * HEAD_DIM
    sizes = (swa_w, kv_w, kv_w, swa_w, dsa_w, HEAD_DIM, HEAD_DIM, dsa_w,
             IDX_HEADS * IDX_DIM, IDX_DIM, IDX_HEADS)
    o = np.cumsum((0,) + sizes)
    aq, ak, av, ag, bq, bk, bv, bg, iq, ik, iw = [m[:, o[i]:o[i + 1]] for i in range(len(sizes))]

    def dup(c):
        return jnp.concatenate([c[:, j:j + HEAD_DIM] for j in range(0, c.shape[1], HEAD_DIM) for _ in (0, 1)],
                               axis=1)

    def pairs(c):
        r = c.reshape(c.shape[0], -1, 2, 2, HEAD_DIM // 2)
        return jnp.swapaxes(r, 2, 3).reshape(c.shape[0], -1)

    roped = pairs(jnp.concatenate([aq, dup(ak), bq, iq, dup(bk), dup(ik)], axis=1))
    return jnp.concatenate([roped, ag, bg], axis=1), (bv, av[:, :HEAD_DIM], av[:, HEAD_DIM:]), iw


def _proj_kernel(x_ref, pos_ref, inv_ref, sgn_ref, wm_ref, bm_ref,
                 wvt_ref, bvt_ref, wiw_ref, biw_ref,
                 aq_ref, ak2_ref, bq_ref, iq_ref, ki2_ref, g_ref, bvt_out_ref, avt_out_ref, iwt_ref):
    xb = x_ref[...].astype(_BF16)
    ang = pos_ref[...].astype(_F32) * inv_ref[...]
    cos = jnp.cos(ang)
    sin = jnp.sin(ang) * sgn_ref[...]

    q_scale = 0.125 * LOG2E
    roped_dst = ((aq_ref, 0, q_scale), (aq_ref, 256, q_scale), (ak2_ref, 0, 1.0), (bq_ref, 0, q_scale),
                 (bq_ref, 256, q_scale), (iq_ref, 0, 1.0), (ki2_ref, 0, 1.0))
    for g, (dst, off, scale) in enumerate(roped_dst):
        c0 = 256 * g
        hm = jnp.dot(xb, wm_ref[:, c0:c0 + 256], preferred_element_type=_F32) + bm_ref[:, c0:c0 + 256]
        for s in range(2):
            sl = slice(LANES * s, LANES * (s + 1))
            o = hm[:, sl] * cos + pltpu.roll(hm[:, sl], LANES // 2, axis=1) * sin
            if scale != 1.0:
                o = o * scale
            dst[:, off + LANES * s:off + LANES * (s + 1)] = o.astype(dst.dtype)

    n_roped = 256 * len(roped_dst)
    for g in range(4):
        c0 = n_roped + 256 * g
        h = jnp.dot(xb, wm_ref[:, c0:c0 + 256], preferred_element_type=_F32) + bm_ref[:, c0:c0 + 256]
        g_ref[:, 256 * g:256 * (g + 1)] = h * (1.0 / (1.0 + jnp.exp(-h)))

    vt = lax.dot_general(wvt_ref[...], xb, _NT, preferred_element_type=_F32) + bvt_ref[...]
    n_b = bvt_out_ref.shape[1]
    bvt_out_ref[0] = vt[0:n_b].astype(bvt_out_ref.dtype)
    avt_out_ref[0] = vt[n_b:].astype(avt_out_ref.dtype)
    iwt = lax.dot_general(wiw_ref[...], xb, _NT, preferred_element_type=_F32) + biw_ref[...]
    iwt_ref[...] = iwt * (IDX_HEADS ** -0.5 * IDX_DIM ** -0.5)


def _projection(x2, pos2, w_in, b_in):
    n, d = x2.shape
    tm = PROJ_ROWS
    wm, wvs, wiw = _split_columns(w_in)
    bm, bvs, biw = _split_columns(b_in[None, :])
    wm = wm.astype(_BF16)
    zpad = jnp.zeros((HEAD_DIM, d), w_in.dtype)
    one_row = jnp.zeros((HEAD_DIM,), b_in.dtype).at[0].set(1.0)
    wvt = jnp.concatenate([p for wv in wvs for p in (wv.T, zpad, zpad, wv.T)], axis=0).astype(_BF16)
    bvt = jnp.concatenate([p for bv in bvs for p in (bv[0], one_row, one_row, bv[0])])[:, None]
    wiw_t = jnp.concatenate([wiw.T, jnp.zeros((16 - IDX_HEADS, d), w_in.dtype)], axis=0).astype(_BF16)
    biw_t = jnp.concatenate([biw[0], jnp.zeros((16 - IDX_HEADS,), b_in.dtype)])[:, None]
    half = HEAD_DIM // 2
    inv = ROPE_THETA ** (-jnp.arange(0, HEAD_DIM, 2, dtype=_F32) / HEAD_DIM)
    inv128 = jnp.tile(inv, LANES // half)[None, :]
    sgn128 = jnp.concatenate([-jnp.ones((LANES // 2,), _F32), jnp.ones((LANES // 2,), _F32)])[None, :]

    row = lambda w: pl.BlockSpec((tm, w), lambda i: (i, 0))
    full = lambda a: pl.BlockSpec(a.shape, lambda i: (0,) * a.ndim)
    out_shape = (
        jax.ShapeDtypeStruct((n, 512), _BF16),
        jax.ShapeDtypeStruct((n, 256), _BF16),
        jax.ShapeDtypeStruct((n, 512), _BF16),
        jax.ShapeDtypeStruct((n, 256), _BF16),
        jax.ShapeDtypeStruct((n, 256), _BF16),
        jax.ShapeDtypeStruct((n, 1024), _F32),
        jax.ShapeDtypeStruct((n // tm, 256, tm), _BF16),
        jax.ShapeDtypeStruct((n // tm, 512, tm), _BF16),
        jax.ShapeDtypeStruct((16, n), _F32),
    )
    out_specs = (row(512), row(256), row(512), row(256), row(256), row(1024),
                 pl.BlockSpec((1, 256, tm), lambda i: (i, 0, 0)),
                 pl.BlockSpec((1, 512, tm), lambda i: (i, 0, 0)),
                 pl.BlockSpec((16, tm), lambda i: (0, i)))
    consts = (inv128, sgn128, wm, bm, wvt, bvt, wiw_t, biw_t)
    return pl.pallas_call(
        _proj_kernel,
        grid=(n // tm,),
        in_specs=[row(d), row(1)] + [full(a) for a in consts],
        out_specs=out_specs,
        out_shape=out_shape,
        compiler_params=pltpu.CompilerParams(dimension_semantics=("arbitrary",),
                                             vmem_limit_bytes=VMEM_LIMIT),
        name="in_projection",
    )(x2, pos2, *consts)


def _swa_kernel(q_ref, kc_ref, kp_ref, vc_ref, vp_ref, g_ref, sink_ref, out_ref):
    i = pl.program_id(1)
    tq = BLOCK
    n_sub = q_ref.shape[0] // tq
    group = SWA_Q_HEADS // SWA_KV_HEADS
    lane = lax.broadcasted_iota(_I32, (tq, LANES), 1)
    lo_half = (lane & (HEAD_DIM // 2)) == 0
    r = lax.broadcasted_iota(_I32, (2 * tq, LANES), 0)
    c = lax.broadcasted_iota(_I32, (2 * tq, LANES), 1)
    in_window = (c < r) & (r <= c + WINDOW)
    row = lax.broadcasted_iota(_I32, (LANES, LANES), 0)

    kwin = jnp.concatenate([kp_ref[...], kc_ref[...]], axis=0)
    vwin = jnp.concatenate([vp_ref[0], vc_ref[0]], axis=1)
    combos = [(g, j, odd) for g in range(SWA_KV_HEADS) for j in range(n_sub) for odd in (0, 1)]
    rows = lambda j: slice(tq * j, tq * (j + 1))
    win = lambda j: slice(tq * j, tq * (j + 2))
    slab_of = lambda g, s: slice(LANES * (g * (group // 2) + s), LANES * (g * (group // 2) + s + 1))

    logits = {}
    for g, j, odd in combos:
        qs = [jnp.where(lo_half if odd == 0 else ~lo_half, q_ref[rows(j), slab_of(g, s)].astype(_F32),
                        0.0).astype(_BF16) for s in range(group // 2)]
        logits[g, j, odd] = lax.dot_general(kwin[win(j), LANES * g:LANES * (g + 1)], jnp.concatenate(qs, axis=0),
                                            _NT, preferred_element_type=_F32)
    probs = {}
    for g, j, odd in combos:
        valid = in_window if j > 0 else in_window & ((r >= tq) | (i > 0))
        ps, sink_terms = [], []
        for s in range(group // 2):
            lg = jnp.where(valid, logits[g, j, odd][:, LANES * s:LANES * (s + 1)], -jnp.inf)
            sink = sink_ref[group * g + 2 * s + odd:group * g + 2 * s + odd + 1, :]
            m = jnp.maximum(jnp.max(lg, axis=0, keepdims=True), sink)
            ps.append(jnp.exp2(lg - m).astype(_BF16))
            sink_terms.append(jnp.exp2(sink - m))
        probs[g, j, odd] = jnp.concatenate(ps, axis=1), jnp.concatenate(sink_terms, axis=1)
    tiles = {}
    for g, j, odd in combos:
        p, sink_term = probs[g, j, odd]
        v = vwin[LANES * (2 * g + odd):LANES * (2 * g + odd + 1), win(j)]
        acc = jnp.dot(v, p, preferred_element_type=_F32)
        l_row = HEAD_DIM if odd == 0 else 0
        tiles[g, j, odd] = acc / (acc[l_row:l_row + 1, :] + sink_term)
    for g in range(SWA_KV_HEADS):
        for j in range(n_sub):
            for s in range(group // 2):
                sl = slice(LANES * s, LANES * (s + 1))
                tile = jnp.where(row < HEAD_DIM, tiles[g, j, 0][:, sl], tiles[g, j, 1][:, sl])
                out_ref[rows(j), slab_of(g, s)] = (tile.T * g_ref[rows(j), slab_of(g, s)]).astype(out_ref.dtype)


def _swa_attention(aq, ak2, avt, gates, sinks, batch, seq):
    tm = PROJ_ROWS
    ns = seq // tm
    n = batch * seq
    per = tm // BLOCK
    sink_b = jnp.broadcast_to((sinks.astype(_F32) * LOG2E)[:, None], (SWA_Q_HEADS, LANES))
    cur = lambda w: pl.BlockSpec((tm, w), lambda b, i: (b * ns + i, 0))
    return pl.pallas_call(
        _swa_kernel,
        grid=(batch, ns),
        in_specs=[cur(512), cur(256),
                  pl.BlockSpec((BLOCK, 256), lambda b, i: (b * ns * per + jnp.maximum(i * per - 1, 0), 0)),
                  pl.BlockSpec((1, 512, tm), lambda b, i: (b * ns + i, 0, 0)),
                  pl.BlockSpec((1, 512, BLOCK), lambda b, i: (b * ns + jnp.maximum(i - 1, 0), 0, per - 1)),
                  cur(512),
                  pl.BlockSpec((SWA_Q_HEADS, LANES), lambda b, i: (0, 0))],
        out_specs=cur(512),
        out_shape=jax.ShapeDtypeStruct((n, 512), _BF16),
        compiler_params=pltpu.CompilerParams(dimension_semantics=("arbitrary", "arbitrary"),
                                             vmem_limit_bytes=VMEM_LIMIT),
        name="swa_attention",
    )(aq, ak2, ak2, avt, avt, gates, sink_b)


_DSA_HEAD_ORDER = (0, 2, 4, 6, 1, 3, 5, 7)


def _key_to_float(k):
    return lax.bitcast_convert_type(jnp.where(k < 0, k ^ 0x7FFFFFFF, k), _F32)


def _key16_to_float(k):
    return lax.bitcast_convert_type(jnp.where(k <= 0, k ^ 0x7FFF ^ jnp.where(k == 0, -1, 0), k) << 16, _F32)


def _midpoint(a, b):
    return (a >> 1) + (b >> 1) + (a & b & 1)


def _dsa_kernel(iq_ref, bq_ref, iwt_ref, g_ref, bk2_ref, ik2_ref, bvt_ref, out_ref,
                sc_ref, sc16_ref, iqm_ref, bqm_ref, ka_ref, kb_ref, kmax_ref, acc_e_ref, acc_o_ref, *, n_sel):
    tq, kc = BLOCK, KEY_CHUNK
    nh, half = DSA_Q_HEADS, DSA_Q_HEADS // 2
    i = pl.program_id(1)
    q0 = i * tq
    lane = lax.broadcasted_iota(_I32, (tq, LANES), 1)
    lo_half = (lane & (HEAD_DIM // 2)) == 0
    one_even, one_odd = HEAD_DIM // 2, 0

    kh = kc // 2
    rem = (q0 + tq) % kc
    n_whole = (q0 + tq) // kc + jnp.where(rem > kh, 1, 0)
    has_half = (rem > 0) & (rem <= kh)
    k_half = n_whole * kc
    n_rows = k_half + jnp.where(has_half, kh, 0)

    def over_keys(body, carry, stages=()):
        def prepared(pieces):
            states = list(pieces)
            for stage in stages:
                states = [stage(s) for s in states]
            return states

        def run(pieces, cr):
            for (k0, rows), state in zip(pieces, prepared(pieces)):
                cr = body(k0, rows, cr, state) if stages else body(k0, rows, cr)
            return cr

        def two(c, cr):
            k0 = pl.multiple_of(c * 2 * kc, 2 * kc)
            return run([(k0, kc), (k0 + kc, kc)], cr)

        carry = lax.fori_loop(0, n_whole // 2, two, carry)
        carry = lax.cond(n_whole % 2 == 1, lambda cr: run([(pl.multiple_of((n_whole - 1) * kc, kc), kc)], cr),
                         lambda cr: cr, carry)
        return lax.cond(has_half, lambda cr: run([(pl.multiple_of(k_half, kh), kh)], cr), lambda cr: cr, carry)

    @pl.when(i == 0)
    def _():
        klane = lax.broadcasted_iota(_I32, (kc, LANES), 1)
        klo = (klane & (HEAD_DIM // 2)) == 0

        def body(c, mx):
            k0 = pl.multiple_of(c * kc, kc)
            k = bk2_ref[pl.ds(k0, kc), :].astype(_F32)
            ka_ref[pl.ds(k0, kc), :] = jnp.where(klane == one_even, 1.0, k).astype(_BF16)
            kb_ref[pl.ds(k0, kc), :] = jnp.where(klane == one_odd, 1.0, k).astype(_BF16)
            return jnp.maximum(mx, jnp.sum(jnp.where(klo, k * k, 0.0), axis=1, keepdims=True))

        mx = lax.fori_loop(0, sc_ref.shape[0] // kc, body, jnp.zeros((kc, 1), _F32))
        kmax_ref[...] = jnp.broadcast_to(jnp.sqrt(jnp.max(mx, axis=0, keepdims=True)), kmax_ref.shape)

    def masked_half(ref, h):
        slab = ref[:, LANES * (h // 2):LANES * (h // 2 + 1)].astype(_F32)
        return jnp.where(lo_half if h % 2 == 0 else ~lo_half, slab, 0.0)

    for h in range(IDX_HEADS):
        iqm_ref[h * tq:(h + 1) * tq, :] = masked_half(iq_ref, h).astype(_BF16)
    kmax = kmax_ref[0:1, 0:1]
    shift_max = jnp.zeros((tq, 1), _F32)
    for p, h in enumerate(_DSA_HEAD_ORDER):
        qm = masked_half(bq_ref, h)
        shift = jnp.sqrt(jnp.sum(qm * qm, axis=1, keepdims=True)) * kmax
        shift_max = jnp.maximum(shift_max, shift)
        one_lane = one_even if h % 2 == 0 else one_odd
        bqm_ref[p * tq:(p + 1) * tq, :] = jnp.where(lane == one_lane, -shift, qm).astype(_BF16)
    bounded = jnp.max(shift_max) <= SHIFT_LIMIT

    w = [iwt_ref[h:h + 1, :] for h in range(IDX_HEADS)]
    def score_heads(k0, rows):
        return lax.dot_general(ik2_ref[pl.ds(k0, rows), :], iqm_ref[...], _NT, preferred_element_type=_F32)

    def score_piece(k0, rows, diagonal, z=None):
        z = score_heads(k0, rows) if z is None else z
        sc = w[0] * jnp.maximum(z[:, 0:LANES], 0.0)
        for h in range(1, IDX_HEADS):
            sc = sc + w[h] * jnp.maximum(z[:, LANES * h:LANES * (h + 1)], 0.0)
        if diagonal:
            kidx = k0 + lax.broadcasted_iota(_I32, (rows, LANES), 0)
            qidx = q0 + lax.broadcasted_iota(_I32, (rows, LANES), 1)
            sc = jnp.where(kidx <= qidx, sc, -jnp.inf)
        sc_ref[pl.ds(k0, rows), :] = sc
        top = lax.bitcast_convert_type(sc, _I32) & jnp.int32(-65536)
        sc16_ref[pl.ds(k0, rows), :] = lax.bitcast_convert_type(top, _F32).astype(_BF16)

    def score_two(c, carry):
        k0 = pl.multiple_of(c * 2 * kc, 2 * kc)
        za, zb = score_heads(k0, kc), score_heads(k0 + kc, kc)
        score_piece(k0, kc, False, za)
        score_piece(k0 + kc, kc, False, zb)
        return carry

    n_plain = n_whole - jnp.where(has_half, 0, 1)
    lax.fori_loop(0, n_plain // 2, score_two, 0)

    @pl.when(n_plain % 2 == 1)
    def _():
        score_piece(pl.multiple_of((n_plain - 1) * kc, kc), kc, False)

    @pl.when(has_half)
    def _():
        score_piece(pl.multiple_of(k_half, kh), kh, True)

    @pl.when(jnp.logical_not(has_half))
    def _():
        score_piece(pl.multiple_of((n_whole - 1) * kc, kc), kc, True)

    def count_ge(t):
        def body(k0, rows, acc):
            for h in range(0, rows, 512):
                ind = jnp.where(sc_ref[pl.ds(k0 + h, 512), :] >= t, 1, 0)
                acc = acc + jnp.sum(ind.reshape(64, 8, LANES), axis=0)
            return acc
        return jnp.sum(over_keys(body, jnp.zeros((8, LANES), _I32)), axis=0, keepdims=True)

    def count_ge16(t):
        t = t.astype(_BF16)

        def body(k0, rows, acc):
            d = pltpu.bitcast(sc16_ref[pl.ds(k0, rows), :] - t, _I32)
            neg = lax.shift_right_logical(d, 15) & 0x00010001
            return acc + jnp.sum(neg.reshape(rows // 16, 8, LANES), axis=0)
        acc = over_keys(body, jnp.zeros((8, LANES), _I32))
        below = jnp.sum((acc & 0xFFFF) + lax.shift_right_logical(acc, 16), axis=0, keepdims=True)
        return n_rows - below

    def search_pass(count, to_thr, st):
        lo, hi, clo, chi, t, thr = st
        t_up, t_dn = _midpoint(t, hi), _midpoint(lo, t)
        thr_up, thr_dn = to_thr(t_up), to_thr(t_dn)
        cnt = count(thr)
        ok = cnt >= n_sel
        return (jnp.where(ok, t, lo), jnp.where(ok, hi, t), jnp.where(ok, cnt, clo), jnp.where(ok, chi, cnt),
                jnp.where(ok, t_up, t_dn), jnp.where(ok, thr_up, thr_dn))

    vec = lambda v: jnp.full((1, LANES), v, _I32)
    st = (vec(KEY_NEG_INF >> 16), vec(1 << 15), vec(0) + n_rows, vec(0), vec(0), _key16_to_float(vec(0)))
    st = lax.fori_loop(0, 16, lambda j, s: search_pass(count_ge16, _key16_to_float, s), st)
    hint_lo = jnp.maximum(st[0] << 16, KEY_NEG_INF)
    hint_hi = jnp.where(st[1] >= (1 << 15), INT_MAX, st[1] << 16)
    hint_hi = jnp.where((hint_hi > 0) & (hint_hi < KEY_MIN_NORMAL), KEY_MIN_NORMAL, hint_hi)
    c1 = count_ge(_key_to_float(hint_lo))
    ok1 = c1 >= n_sel
    lo, clo = jnp.where(ok1, hint_lo, KEY_NEG_INF), jnp.where(ok1, c1, n_rows)
    hi, chi = jnp.where(ok1, INT_MAX, hint_lo), jnp.where(ok1, 0, c1)
    c2 = count_ge(_key_to_float(hint_hi))
    up2 = (c2 >= n_sel) & (hint_hi > lo)
    dn2 = (c2 < n_sel) & (hint_hi < hi)
    lo, clo = jnp.where(up2, hint_hi, lo), jnp.where(up2, c2, clo)
    hi, chi = jnp.where(dn2, hint_hi, hi), jnp.where(dn2, c2, chi)

    def unsettled(lo, hi, clo):
        return ~((clo == n_sel) | (hi - 1 <= lo) | ((lo >= 0) & (hi <= KEY_MIN_NORMAL)))

    def fine_group(c):
        s = c[1:7]
        for _ in range(SEARCH_GROUP):
            s = search_pass(count_ge, _key_to_float, s)
        return (c[0] + 1,) + s + (jnp.max(jnp.where(unsettled(*s[:3]), 1, 0)),)

    t0 = _midpoint(lo, hi)
    init = (jnp.int32(0), lo, hi, clo, chi, t0, _key_to_float(t0),
            jnp.max(jnp.where(unsettled(lo, hi, clo), 1, 0)))
    _, lo, hi, clo, chi, _, _, _ = lax.while_loop(lambda c: (c[7] > 0) & (c[0] < 16), fine_group, init)
    t_lo = _key_to_float(lo)
    t_hi = _key_to_float(hi)

    need = (clo > n_sel) & (lo > KEY_NEG_INF)

    @pl.when(jnp.max(jnp.where(need, 1, 0)) > 0)
    def _():
        tk = TIE_CHUNK
        room = (n_sel - chi).astype(_F32)
        before = (lax.broadcasted_iota(_I32, (tk, tk), 0) > lax.broadcasted_iota(_I32, (tk, tk), 1))
        before = jnp.where(before, 1.0, 0.0).astype(_BF16)

        def body(k0, rows, seen):
            for h in range(rows // tk):
                x = sc_ref[pl.ds(k0 + h * tk, tk), :]
                tie = (x >= t_lo) & ~(x >= t_hi)
                tf = jnp.where(tie, 1.0, 0.0)
                rank = seen + jnp.dot(before, tf.astype(_BF16), preferred_element_type=_F32)
                sc_ref[pl.ds(k0 + h * tk, tk), :] = jnp.where(tie & (rank >= room), -jnp.inf, x)
                seen = seen + jnp.sum(tf, axis=0, keepdims=True)
            return seen

        over_keys(body, jnp.zeros((1, LANES), _F32))

    thr = jnp.maximum(t_lo, F32_LOWEST)

    acc_e_ref[...] = jnp.zeros_like(acc_e_ref)
    acc_o_ref[...] = jnp.zeros_like(acc_o_ref)
    vt_rows = bvt_ref.shape[2]

    def masked_logits(piece):
        k0, rows = piece
        sel = sc_ref[pl.ds(k0, rows), :] >= thr
        lge = lax.dot_general(ka_ref[pl.ds(k0, rows), :], bqm_ref[0:half * tq, :], _NT,
                              preferred_element_type=_F32)
        lgo = lax.dot_general(kb_ref[pl.ds(k0, rows), :], bqm_ref[half * tq:nh * tq, :], _NT,
                              preferred_element_type=_F32)
        tiles = lambda lg: [jnp.where(sel, lg[:, LANES * j:LANES * (j + 1)], NEG_BIG) for j in range(half)]
        vt = jnp.concatenate([bvt_ref[k0 // vt_rows + j] for j in range(rows // vt_rows)], axis=1)
        return ((acc_e_ref, tiles(lge), vt[0:LANES, :]), (acc_o_ref, tiles(lgo), vt[LANES:2 * LANES, :]))

    @pl.when(bounded)
    def _():
        def weights(groups):
            return [(ref, jnp.concatenate([jnp.exp2(t).astype(_BF16) for t in tiles], axis=1), v)
                    for ref, tiles, v in groups]

        def body(k0, rows, carry, groups):
            for ref, p, v in groups:
                ref[...] += jnp.dot(v, p, preferred_element_type=_F32)
            return carry
        over_keys(body, jnp.int32(0), stages=(masked_logits, weights))

    @pl.when(jnp.logical_not(bounded))
    def _():
        def body(k0, rows, m):
            m_out = []
            for g, (ref, tiles, v) in enumerate(masked_logits((k0, rows))):
                ps, alphas = [], []
                for j, t in enumerate(tiles):
                    sl = slice(LANES * (g * half + j), LANES * (g * half + j + 1))
                    m_new = jnp.maximum(m[:, sl], jnp.max(t, axis=0, keepdims=True))
                    ps.append(jnp.exp2(t - m_new).astype(_BF16))
                    alphas.append(jnp.exp2(m[:, sl] - m_new))
                    m_out.append(m_new)
                ref[...] = ref[...] * jnp.concatenate(alphas, axis=1) + jnp.dot(
                    v, jnp.concatenate(ps, axis=1), preferred_element_type=_F32)
            return jnp.concatenate(m_out, axis=1)
        over_keys(body, jnp.full((1, nh * LANES), NEG_BIG, _F32))

    row = lax.broadcasted_iota(_I32, (LANES, LANES), 0)
    for j in range(half):
        sl = slice(LANES * j, LANES * (j + 1))
        even = acc_e_ref[:, sl] / acc_e_ref[HEAD_DIM:HEAD_DIM + 1, sl]
        odd = acc_o_ref[:, sl] / acc_o_ref[0:1, sl]
        tile = jnp.where(row < HEAD_DIM, even, odd)
        out_ref[:, sl] = (tile.T * g_ref[:, sl]).astype(out_ref.dtype)


def _dsa_attention(iq, bq, iwt, gates, ki2, bvt, batch, seq):
    nb = seq // BLOCK
    n = batch * seq
    nvt = seq // PROJ_ROWS
    n_sel = min(TOPK_MAX, seq // 4)
    blk = lambda w, col=0: pl.BlockSpec((BLOCK, w), lambda b, i: (b * nb + i, col))
    return pl.pallas_call(
        functools.partial(_dsa_kernel, n_sel=n_sel),
        grid=(batch, nb),
        in_specs=[blk(256), blk(512),
                  pl.BlockSpec((16, BLOCK), lambda b, i: (0, b * nb + i)),
                  blk(512, 1),
                  pl.BlockSpec((seq, LANES), lambda b, i: (b, 0)),
                  pl.BlockSpec((seq, LANES), lambda b, i: (b, 1)),
                  pl.BlockSpec((nvt, 256, PROJ_ROWS), lambda b, i: (b, 0, 0))],
        out_specs=blk(512),
        out_shape=jax.ShapeDtypeStruct((n, 512), _BF16),
        scratch_shapes=[pltpu.VMEM((seq, LANES), _F32),
                        pltpu.VMEM((seq, LANES), _BF16),
                        pltpu.VMEM((IDX_HEADS * BLOCK, LANES), _BF16),
                        pltpu.VMEM((DSA_Q_HEADS * BLOCK, LANES), _BF16),
                        pltpu.VMEM((seq, LANES), _BF16),
                        pltpu.VMEM((seq, LANES), _BF16),
                        pltpu.VMEM((8, LANES), _F32),
                        pltpu.VMEM((LANES, 512), _F32),
                        pltpu.VMEM((LANES, 512), _F32)],
        compiler_params=pltpu.CompilerParams(dimension_semantics=("arbitrary", "arbitrary"),
                                             vmem_limit_bytes=VMEM_LIMIT),
        name="dsa_attention",
    )(iq, bq, iwt, gates, ki2, ki2, bvt)


def _out_kernel(a_ref, b_ref, x_ref, w_ref, bo_ref, gain_ref, bias_ref, out_ref, *, alpha):
    half = a_ref.shape[1]
    y = jnp.dot(a_ref[...], w_ref[0:half, :], preferred_element_type=_F32)
    y = y + jnp.dot(b_ref[...], w_ref[half:2 * half, :], preferred_element_type=_F32)
    z = alpha * x_ref[...] + (y + bo_ref[...])
    mu = jnp.mean(z, axis=-1, keepdims=True)
    zc = z - mu
    var = jnp.mean(zc * zc, axis=-1, keepdims=True)
    out_ref[...] = zc * lax.rsqrt(var + LN_EPS) * gain_ref[...] + bias_ref[...]


def _out_projection(a, b, x2, w_out, b_out, gain, bias, alpha):
    n, d = x2.shape
    tm = PROJ_ROWS
    row = lambda w: pl.BlockSpec((tm, w), lambda i: (i, 0))
    full = lambda a_: pl.BlockSpec(a_.shape, lambda i: (0,) * a_.ndim)
    consts = (w_out.astype(_BF16), b_out[None, :], gain[None, :], bias[None, :])
    return pl.pallas_call(
        functools.partial(_out_kernel, alpha=alpha),
        grid=(n // tm,),
        in_specs=[row(a.shape[1]), row(b.shape[1]), row(d)] + [full(c) for c in consts],
        out_specs=row(d),
        out_shape=jax.ShapeDtypeStruct((n, d), x2.dtype),
        compiler_params=pltpu.CompilerParams(dimension_semantics=("arbitrary",),
                                             vmem_limit_bytes=VMEM_LIMIT),
        name="out_projection",
    )(a, b, x2, *consts)


def _layer(h, pos2, w_in, b_in, sinks, w_out, b_out, gain, bias, alpha):
    batch, seq, d = h.shape
    x2 = h.reshape(batch * seq, d)
    aq, ak2, bq, iq, ki2, gates, bvt, avt, iwt = _projection(x2, pos2, w_in, b_in)
    a = _swa_attention(aq, ak2, avt, gates, sinks, batch, seq)
    b = _dsa_attention(iq, bq, iwt, gates, ki2, bvt, batch, seq)
    return _out_projection(a, b, x2, w_out, b_out, gain, bias, alpha).reshape(batch, seq, d)


def kernel(x, positions, w_in, b_in, swa_sinks, w_out, b_out, ln_gain, ln_bias):
    batch, seq, d = x.shape
    depth = w_in.shape[0]
    assert d == 1024 and seq % KEY_CHUNK == 0 and (batch * seq) % PROJ_ROWS == 0
    assert KEY_CHUNK % PROJ_ROWS == 0 and KEY_CHUNK % TIE_CHUNK == 0
    alpha = (2.0 * depth) ** 0.25
    pos2 = positions.reshape(batch * seq, 1)
    h = x
    for layer in range(depth):
        h = _layer(h, pos2, w_in[layer], b_in[layer], swa_sinks[layer], w_out[layer], b_out[layer],
                   ln_gain[layer], ln_bias[layer], alpha)
    return h
```

```python
import functools

import numpy as np
import jax
import jax.numpy as jnp
from jax import lax
from jax.experimental import pallas as pl
from jax.experimental.pallas import tpu as pltpu

HEAD_DIM = 64
SWA_Q_HEADS = 8
SWA_KV_HEADS = 2
DSA_Q_HEADS = 8
IDX_HEADS = 4
IDX_DIM = 64
WINDOW = 128
BLOCK = 128
TOPK_MAX = 256
ROPE_THETA = 10000.0
LN_EPS = 1e-5

LANES = 128
PROJ_ROWS = 512
KEY_CHUNK = 1024
TIE_CHUNK = 512
VMEM_LIMIT = 56 * 1024 * 1024

LOG2E = 1.4426950408889634
NEG_BIG = -1e30
SHIFT_LIMIT = 60.0
F32_LOWEST = -3.4028234663852886e38
INT_MAX = 2 ** 31 - 1
KEY_NEG_INF = 0x807FFFFF - 2 ** 32
KEY_MIN_NORMAL = 0x00800000
SEARCH_FIRST = 6
SEARCH_GROUP = 2

_F32 = jnp.float32
_BF16 = jnp.bfloat16
_I32 = jnp.int32
_NT = (((1,), (1,)), ((), ()))


def _split_columns(m):
    swa_w = SWA_Q_HEADS * HEAD_DIM
    kv_w = SWA_KV_HEADS * HEAD_DIM
    dsa_w = DSA_Q_HEADS * HEAD_DIM
    sizes = (swa_w, kv_w, kv_w, swa_w, dsa_w, HEAD_DIM, HEAD_DIM, dsa_w,
             IDX_HEADS * IDX_DIM, IDX_DIM, IDX_HEADS)
    o = np.cumsum((0,) + sizes)
    aq, ak, av, ag, bq, bk, bv, bg, iq, ik, iw = [m[:, o[i]:o[i + 1]] for i in range(len(sizes))]

    def dup(c):
        return jnp.concatenate([c[:, j:j + HEAD_DIM] for j in range(0, c.shape[1], HEAD_DIM) for _ in (0, 1)],
                               axis=1)

    def pairs(c):
        r = c.reshape(c.shape[0], -1, 2, 2, HEAD_DIM // 2)
        return jnp.swapaxes(r, 2, 3).reshape(c.shape[0], -1)

    roped = pairs(jnp.concatenate([aq, dup(ak), bq, iq, dup(bk), dup(ik)], axis=1))
    return jnp.concatenate([roped, ag, bg], axis=1), (bv, av[:, :HEAD_DIM], av[:, HEAD_DIM:]), iw


def _proj_kernel(x_ref, pos_ref, inv_ref, sgn_ref, wm_ref, bm_ref,
                 wvt_ref, bvt_ref, wiw_ref, biw_ref,
                 aq_ref, ak2_ref, bq_ref, iq_ref, ki2_ref, g_ref, bvt_out_ref, avt_out_ref, iwt_ref):
    xb = x_ref[...].astype(_BF16)
    ang = pos_ref[...].astype(_F32) * inv_ref[...]
    cos = jnp.cos(ang)
    sin = jnp.sin(ang) * sgn_ref[...]

    q_scale = 0.125 * LOG2E
    roped_dst = ((aq_ref, 0, q_scale), (aq_ref, 256, q_scale), (ak2_ref, 0, 1.0), (bq_ref, 0, q_scale),
                 (bq_ref, 256, q_scale), (iq_ref, 0, 1.0), (ki2_ref, 0, 1.0))
    for g, (dst, off, scale) in enumerate(roped_dst):
        c0 = 256 * g
        hm = jnp.dot(xb, wm_ref[:, c0:c0 + 256], preferred_element_type=_F32) + bm_ref[:, c0:c0 + 256]
        for s in range(2):
            sl = slice(LANES * s, LANES * (s + 1))
            o = hm[:, sl] * cos + pltpu.roll(hm[:, sl], LANES // 2, axis=1) * sin
            if scale != 1.0:
                o = o * scale
            dst[:, off + LANES * s:off + LANES * (s + 1)] = o.astype(dst.dtype)

    n_roped = 256 * len(roped_dst)
    for g in range(4):
        c0 = n_roped + 256 * g
        h = jnp.dot(xb, wm_ref[:, c0:c0 + 256], preferred_element_type=_F32) + bm_ref[:, c0:c0 + 256]
        g_ref[:, 256 * g:256 * (g + 1)] = h * (1.0 / (1.0 + jnp.exp(-h)))

    vt = lax.dot_general(wvt_ref[...], xb, _NT, preferred_element_type=_F32) + bvt_ref[...]
    n_b = bvt_out_ref.shape[1]
    bvt_out_ref[0] = vt[0:n_b].astype(bvt_out_ref.dtype)
    avt_out_ref[0] = vt[n_b:].astype(avt_out_ref.dtype)
    iwt = lax.dot_general(wiw_ref[...], xb, _NT, preferred_element_type=_F32) + biw_ref[...]
    iwt_ref[...] = iwt * (IDX_HEADS ** -0.5 * IDX_DIM ** -0.5)


def _projection(x2, pos2, w_in, b_in):
    n, d = x2.shape
    tm = PROJ_ROWS
    wm, wvs, wiw = _split_columns(w_in)
    bm, bvs, biw = _split_columns(b_in[None, :])
    wm = wm.astype(_BF16)
    zpad = jnp.zeros((HEAD_DIM, d), w_in.dtype)
    one_row = jnp.zeros((HEAD_DIM,), b_in.dtype).at[0].set(1.0)
    wvt = jnp.concatenate([p for wv in wvs for p in (wv.T, zpad, zpad, wv.T)], axis=0).astype(_BF16)
    bvt = jnp.concatenate([p for bv in bvs for p in (bv[0], one_row, one_row, bv[0])])[:, None]
    wiw_t = jnp.concatenate([wiw.T, jnp.zeros((16 - IDX_HEADS, d), w_in.dtype)], axis=0).astype(_BF16)
    biw_t = jnp.concatenate([biw[0], jnp.zeros((16 - IDX_HEADS,), b_in.dtype)])[:, None]
    half = HEAD_DIM // 2
    inv = ROPE_THETA ** (-jnp.arange(0, HEAD_DIM, 2, dtype=_F32) / HEAD_DIM)
    inv128 = jnp.tile(inv, LANES // half)[None, :]
    sgn128 = jnp.concatenate([-jnp.ones((LANES // 2,), _F32), jnp.ones((LANES // 2,), _F32)])[None, :]

    row = lambda w: pl.BlockSpec((tm, w), lambda i: (i, 0))
    full = lambda a: pl.BlockSpec(a.shape, lambda i: (0,) * a.ndim)
    out_shape = (
        jax.ShapeDtypeStruct((n, 512), _BF16),
        jax.ShapeDtypeStruct((n, 256), _BF16),
        jax.ShapeDtypeStruct((n, 512), _BF16),
        jax.ShapeDtypeStruct((n, 256), _BF16),
        jax.ShapeDtypeStruct((n, 256), _BF16),
        jax.ShapeDtypeStruct((n, 1024), _F32),
        jax.ShapeDtypeStruct((n // tm, 256, tm), _BF16),
        jax.ShapeDtypeStruct((n // tm, 512, tm), _BF16),
        jax.ShapeDtypeStruct((16, n), _F32),
    )
    out_specs = (row(512), row(256), row(512), row(256), row(256), row(1024),
                 pl.BlockSpec((1, 256, tm), lambda i: (i, 0, 0)),
                 pl.BlockSpec((1, 512, tm), lambda i: (i, 0, 0)),
                 pl.BlockSpec((16, tm), lambda i: (0, i)))
    consts = (inv128, sgn128, wm, bm, wvt, bvt, wiw_t, biw_t)
    return pl.pallas_call(
        _proj_kernel,
        grid=(n // tm,),
        in_specs=[row(d), row(1)] + [full(a) for a in consts],
        out_specs=out_specs,
        out_shape=out_shape,
        compiler_params=pltpu.CompilerParams(dimension_semantics=("arbitrary",),
                                             vmem_limit_bytes=VMEM_LIMIT),
        name="in_projection",
    )(x2, pos2, *consts)


def _swa_kernel(q_ref, kc_ref, kp_ref, vc_ref, vp_ref, g_ref, sink_ref, out_ref):
    i = pl.program_id(1)
    tq = BLOCK
    n_sub = q_ref.shape[0] // tq
    group = SWA_Q_HEADS // SWA_KV_HEADS
    lane = lax.broadcasted_iota(_I32, (tq, LANES), 1)
    lo_half = (lane & (HEAD_DIM // 2)) == 0
    r = lax.broadcasted_iota(_I32, (2 * tq, LANES), 0)
    c = lax.broadcasted_iota(_I32, (2 * tq, LANES), 1)
    in_window = (c < r) & (r <= c + WINDOW)
    row = lax.broadcasted_iota(_I32, (LANES, LANES), 0)

    kwin = jnp.concatenate([kp_ref[...], kc_ref[...]], axis=0)
    vwin = jnp.concatenate([vp_ref[0], vc_ref[0]], axis=1)
    combos = [(g, j, odd) for g in range(SWA_KV_HEADS) for j in range(n_sub) for odd in (0, 1)]
    rows = lambda j: slice(tq * j, tq * (j + 1))
    win = lambda j: slice(tq * j, tq * (j + 2))
    slab_of = lambda g, s: slice(LANES * (g * (group // 2) + s), LANES * (g * (group // 2) + s + 1))

    logits = {}
    for g, j, odd in combos:
        qs = [jnp.where(lo_half if odd == 0 else ~lo_half, q_ref[rows(j), slab_of(g, s)].astype(_F32),
                        0.0).astype(_BF16) for s in range(group // 2)]
        logits[g, j, odd] = lax.dot_general(kwin[win(j), LANES * g:LANES * (g + 1)], jnp.concatenate(qs, axis=0),
                                            _NT, preferred_element_type=_F32)
    probs = {}
    for g, j, odd in combos:
        valid = in_window if j > 0 else in_window & ((r >= tq) | (i > 0))
        ps, sink_terms = [], []
        for s in range(group // 2):
            lg = jnp.where(valid, logits[g, j, odd][:, LANES * s:LANES * (s + 1)], -jnp.inf)
            sink = sink_ref[group * g + 2 * s + odd:group * g + 2 * s + odd + 1, :]
            m = jnp.maximum(jnp.max(lg, axis=0, keepdims=True), sink)
            ps.append(jnp.exp2(lg - m).astype(_BF16))
            sink_terms.append(jnp.exp2(sink - m))
        probs[g, j, odd] = jnp.concatenate(ps, axis=1), jnp.concatenate(sink_terms, axis=1)
    tiles = {}
    for g, j, odd in combos:
        p, sink_term = probs[g, j, odd]
        v = vwin[LANES * (2 * g + odd):LANES * (2 * g + odd + 1), win(j)]
        acc = jnp.dot(v, p, preferred_element_type=_F32)
        l_row = HEAD_DIM if odd == 0 else 0
        tiles[g, j, odd] = acc / (acc[l_row:l_row + 1, :] + sink_term)
    for g in range(SWA_KV_HEADS):
        for j in range(n_sub):
            for s in range(group // 2):
                sl = slice(LANES * s, LANES * (s + 1))
                tile = jnp.where(row < HEAD_DIM, tiles[g, j, 0][:, sl], tiles[g, j, 1][:, sl])
                out_ref[rows(j), slab_of(g, s)] = (tile.T * g_ref[rows(j), slab_of(g, s)]).astype(out_ref.dtype)


def _swa_attention(aq, ak2, avt, gates, sinks, batch, seq):
    tm = PROJ_ROWS
    ns = seq // tm
    n = batch * seq
    per = tm // BLOCK
    sink_b = jnp.broadcast_to((sinks.astype(_F32) * LOG2E)[:, None], (SWA_Q_HEADS, LANES))
    cur = lambda w: pl.BlockSpec((tm, w), lambda b, i: (b * ns + i, 0))
    return pl.pallas_call(
        _swa_kernel,
        grid=(batch, ns),
        in_specs=[cur(512), cur(256),
                  pl.BlockSpec((BLOCK, 256), lambda b, i: (b * ns * per + jnp.maximum(i * per - 1, 0), 0)),
                  pl.BlockSpec((1, 512, tm), lambda b, i: (b * ns + i, 0, 0)),
                  pl.BlockSpec((1, 512, BLOCK), lambda b, i: (b * ns + jnp.maximum(i - 1, 0), 0, per - 1)),
                  cur(512),
                  pl.BlockSpec((SWA_Q_HEADS, LANES), lambda b, i: (0, 0))],
        out_specs=cur(512),
        out_shape=jax.ShapeDtypeStruct((n, 512), _BF16),
        compiler_params=pltpu.CompilerParams(dimension_semantics=("arbitrary", "arbitrary"),
                                             vmem_limit_bytes=VMEM_LIMIT),
        name="swa_attention",
    )(aq, ak2, ak2, avt, avt, gates, sink_b)


_DSA_HEAD_ORDER = (0, 2, 4, 6, 1, 3, 5, 7)


def _key_to_float(k):
    return lax.bitcast_convert_type(jnp.where(k < 0, k ^ 0x7FFFFFFF, k), _F32)


def _key16_to_float(k):
    return lax.bitcast_convert_type(jnp.where(k <= 0, k ^ 0x7FFF ^ jnp.where(k == 0, -1, 0), k) << 16, _F32)


def _midpoint(a, b):
    return (a >> 1) + (b >> 1) + (a & b & 1)


def _dsa_kernel(iq_ref, bq_ref, iwt_ref, g_ref, bk2_ref, ik2_ref, bvt_ref, out_ref,
                sc_ref, sc16_ref, iqm_ref, bqm_ref, ka_ref, kb_ref, kmax_ref, acc_e_ref, acc_o_ref, *, n_sel):
    tq, kc = BLOCK, KEY_CHUNK
    nh, half = DSA_Q_HEADS, DSA_Q_HEADS // 2
    i = pl.program_id(1)
    q0 = i * tq
    lane = lax.broadcasted_iota(_I32, (tq, LANES), 1)
    lo_half = (lane & (HEAD_DIM // 2)) == 0
    one_even, one_odd = HEAD_DIM // 2, 0

    kh = kc // 2
    rem = (q0 + tq) % kc
    n_whole = (q0 + tq) // kc + jnp.where(rem > kh, 1, 0)
    has_half = (rem > 0) & (rem <= kh)
    k_half = n_whole * kc
    n_rows = k_half + jnp.where(has_half, kh, 0)

    def over_keys(body, carry, stages=()):
        def prepared(pieces):
            states = list(pieces)
            for stage in stages:
                states = [stage(s) for s in states]
            return states

        def run(pieces, cr):
            for (k0, rows), state in zip(pieces, prepared(pieces)):
                cr = body(k0, rows, cr, state) if stages else body(k0, rows, cr)
            return cr

        def two(c, cr):
            k0 = pl.multiple_of(c * 2 * kc, 2 * kc)
            return run([(k0, kc), (k0 + kc, kc)], cr)

        carry = lax.fori_loop(0, n_whole // 2, two, carry)
        carry = lax.cond(n_whole % 2 == 1, lambda cr: run([(pl.multiple_of((n_whole - 1) * kc, kc), kc)], cr),
                         lambda cr: cr, carry)
        return lax.cond(has_half, lambda cr: run([(pl.multiple_of(k_half, kh), kh)], cr), lambda cr: cr, carry)

    @pl.when(i == 0)
    def _():
        klane = lax.broadcasted_iota(_I32, (kc, LANES), 1)
        klo = (klane & (HEAD_DIM // 2)) == 0

        def body(c, mx):
            k0 = pl.multiple_of(c * kc, kc)
            k = bk2_ref[pl.ds(k0, kc), :].astype(_F32)
            ka_ref[pl.ds(k0, kc), :] = jnp.where(klane == one_even, 1.0, k).astype(_BF16)
            kb_ref[pl.ds(k0, kc), :] = jnp.where(klane == one_odd, 1.0, k).astype(_BF16)
            return jnp.maximum(mx, jnp.sum(jnp.where(klo, k * k, 0.0), axis=1, keepdims=True))

        mx = lax.fori_loop(0, sc_ref.shape[0] // kc, body, jnp.zeros((kc, 1), _F32))
        kmax_ref[...] = jnp.broadcast_to(jnp.sqrt(jnp.max(mx, axis=0, keepdims=True)), kmax_ref.shape)

    def masked_half(ref, h):
        slab = ref[:, LANES * (h // 2):LANES * (h // 2 + 1)].astype(_F32)
        return jnp.where(lo_half if h % 2 == 0 else ~lo_half, slab, 0.0)

    for h in range(IDX_HEADS):
        iqm_ref[h * tq:(h + 1) * tq, :] = masked_half(iq_ref, h).astype(_BF16)
    kmax = kmax_ref[0:1, 0:1]
    shift_max = jnp.zeros((tq, 1), _F32)
    for p, h in enumerate(_DSA_HEAD_ORDER):
        qm = masked_half(bq_ref, h)
        shift = jnp.sqrt(jnp.sum(qm * qm, axis=1, keepdims=True)) * kmax
        shift_max = jnp.maximum(shift_max, shift)
        one_lane = one_even if h % 2 == 0 else one_odd
        bqm_ref[p * tq:(p + 1) * tq, :] = jnp.where(lane == one_lane, -shift, qm).astype(_BF16)
    bounded = jnp.max(shift_max) <= SHIFT_LIMIT

    w = [iwt_ref[h:h + 1, :] for h in range(IDX_HEADS)]
    def score_heads(k0, rows):
        return lax.dot_general(ik2_ref[pl.ds(k0, rows), :], iqm_ref[...], _NT, preferred_element_type=_F32)

    def score_piece(k0, rows, diagonal, z=None):
        z = score_heads(k0, rows) if z is None else z
        sc = w[0] * jnp.maximum(z[:, 0:LANES], 0.0)
        for h in range(1, IDX_HEADS):
            sc = sc + w[h] * jnp.maximum(z[:, LANES * h:LANES * (h + 1)], 0.0)
        if diagonal:
            kidx = k0 + lax.broadcasted_iota(_I32, (rows, LANES), 0)
            qidx = q0 + lax.broadcasted_iota(_I32, (rows, LANES), 1)
            sc = jnp.where(kidx <= qidx, sc, -jnp.inf)
        sc_ref[pl.ds(k0, rows), :] = sc
        top = lax.bitcast_convert_type(sc, _I32) & jnp.int32(-65536)
        sc16_ref[pl.ds(k0, rows), :] = lax.bitcast_convert_type(top, _F32).astype(_BF16)

    def score_two(c, carry):
        k0 = pl.multiple_of(c * 2 * kc, 2 * kc)
        za, zb = score_heads(k0, kc), score_heads(k0 + kc, kc)
        score_piece(k0, kc, False, za)
        score_piece(k0 + kc, kc, False, zb)
        return carry

    n_plain = n_whole - jnp.where(has_half, 0, 1)
    lax.fori_loop(0, n_plain // 2, score_two, 0)

    @pl.when(n_plain % 2 == 1)
    def _():
        score_piece(pl.multiple_of((n_plain - 1) * kc, kc), kc, False)

    @pl.when(has_half)
    def _():
        score_piece(pl.multiple_of(k_half, kh), kh, True)

    @pl.when(jnp.logical_not(has_half))
    def _():
        score_piece(pl.multiple_of((n_whole - 1) * kc, kc), kc, True)

    def count_ge(t):
        def body(k0, rows, acc):
            for h in range(0, rows, 512):
                ind = jnp.where(sc_ref[pl.ds(k0 + h, 512), :] >= t, 1, 0)
                acc = acc + jnp.sum(ind.reshape(64, 8, LANES), axis=0)
            return acc
        return jnp.sum(over_keys(body, jnp.zeros((8, LANES), _I32)), axis=0, keepdims=True)

    def count_ge16(t):
        t = t.astype(_BF16)

        def body(k0, rows, acc):
            d = pltpu.bitcast(sc16_ref[pl.ds(k0, rows), :] - t, _I32)
            neg = lax.shift_right_logical(d, 15) & 0x00010001
            return acc + jnp.sum(neg.reshape(rows // 16, 8, LANES), axis=0)
        acc = over_keys(body, jnp.zeros((8, LANES), _I32))
        below = jnp.sum((acc & 0xFFFF) + lax.shift_right_logical(acc, 16), axis=0, keepdims=True)
        return n_rows - below

    def search_pass(count, to_thr, st):
        lo, hi, clo, chi, t, thr = st
        t_up, t_dn = _midpoint(t, hi), _midpoint(lo, t)
        thr_up, thr_dn = to_thr(t_up), to_thr(t_dn)
        cnt = count(thr)
        ok = cnt >= n_sel
        return (jnp.where(ok, t, lo), jnp.where(ok, hi, t), jnp.where(ok, cnt, clo), jnp.where(ok, chi, cnt),
                jnp.where(ok, t_up, t_dn), jnp.where(ok, thr_up, thr_dn))

    vec = lambda v: jnp.full((1, LANES), v, _I32)
    st = (vec(KEY_NEG_INF >> 16), vec(1 << 15), vec(0) + n_rows, vec(0), vec(0), _key16_to_float(vec(0)))
    st = lax.fori_loop(0, 16, lambda j, s: search_pass(count_ge16, _key16_to_float, s), st)
    hint_lo = jnp.maximum(st[0] << 16, KEY_NEG_INF)
    hint_hi = jnp.where(st[1] >= (1 << 15), INT_MAX, st[1] << 16)
    hint_hi = jnp.where((hint_hi > 0) & (hint_hi < KEY_MIN_NORMAL), KEY_MIN_NORMAL, hint_hi)
    c1 = count_ge(_key_to_float(hint_lo))
    ok1 = c1 >= n_sel
    lo, clo = jnp.where(ok1, hint_lo, KEY_NEG_INF), jnp.where(ok1, c1, n_rows)
    hi, chi = jnp.where(ok1, INT_MAX, hint_lo), jnp.where(ok1, 0, c1)
    c2 = count_ge(_key_to_float(hint_hi))
    up2 = (c2 >= n_sel) & (hint_hi > lo)
    dn2 = (c2 < n_sel) & (hint_hi < hi)
    lo, clo = jnp.where(up2, hint_hi, lo), jnp.where(up2, c2, clo)
    hi, chi = jnp.where(dn2, hint_hi, hi), jnp.where(dn2, c2, chi)

    def unsettled(lo, hi, clo):
        return ~((clo == n_sel) | (hi - 1 <= lo) | ((lo >= 0) & (hi <= KEY_MIN_NORMAL)))

    def fine_passes(n, s):
        for _ in range(n):
            s = search_pass(count_ge, _key_to_float, s)
        return s

    any_unsettled = lambda s: jnp.max(jnp.where(unsettled(*s[:3]), 1, 0))

    def fine_group(c):
        s = fine_passes(SEARCH_GROUP, c[1:7])
        return (c[0] + 1,) + s + (any_unsettled(s),)

    t0 = _midpoint(lo, hi)
    s = (lo, hi, clo, chi, t0, _key_to_float(t0))
    s = lax.cond(any_unsettled(s) > 0, lambda s: fine_passes(SEARCH_FIRST, s), lambda s: s, s)
    init = (jnp.int32(0),) + s + (any_unsettled(s),)
    _, lo, hi, clo, chi, _, _, _ = lax.while_loop(lambda c: (c[7] > 0) & (c[0] < 24), fine_group, init)
    t_lo = _key_to_float(lo)
    t_hi = _key_to_float(hi)

    need = (clo > n_sel) & (lo > KEY_NEG_INF)

    @pl.when(jnp.max(jnp.where(need, 1, 0)) > 0)
    def _():
        tk = TIE_CHUNK
        room = (n_sel - chi).astype(_F32)
        before = (lax.broadcasted_iota(_I32, (tk, tk), 0) > lax.broadcasted_iota(_I32, (tk, tk), 1))
        before = jnp.where(before, 1.0, 0.0).astype(_BF16)

        def body(k0, rows, seen):
            for h in range(rows // tk):
                x = sc_ref[pl.ds(k0 + h * tk, tk), :]
                tie = (x >= t_lo) & ~(x >= t_hi)
                tf = jnp.where(tie, 1.0, 0.0)
                rank = seen + jnp.dot(before, tf.astype(_BF16), preferred_element_type=_F32)
                sc_ref[pl.ds(k0 + h * tk, tk), :] = jnp.where(tie & (rank >= room), -jnp.inf, x)
                seen = seen + jnp.sum(tf, axis=0, keepdims=True)
            return seen

        over_keys(body, jnp.zeros((1, LANES), _F32))

    thr = jnp.maximum(t_lo, F32_LOWEST)

    acc_e_ref[...] = jnp.zeros_like(acc_e_ref)
    acc_o_ref[...] = jnp.zeros_like(acc_o_ref)
    vt_rows = bvt_ref.shape[2]

    def masked_logits(piece):
        k0, rows = piece
        sel = sc_ref[pl.ds(k0, rows), :] >= thr
        lge = lax.dot_general(ka_ref[pl.ds(k0, rows), :], bqm_ref[0:half * tq, :], _NT,
                              preferred_element_type=_F32)
        lgo = lax.dot_general(kb_ref[pl.ds(k0, rows), :], bqm_ref[half * tq:nh * tq, :], _NT,
                              preferred_element_type=_F32)
        tiles = lambda lg: [jnp.where(sel, lg[:, LANES * j:LANES * (j + 1)], NEG_BIG) for j in range(half)]
        vt = jnp.concatenate([bvt_ref[k0 // vt_rows + j] for j in range(rows // vt_rows)], axis=1)
        return ((acc_e_ref, tiles(lge), vt[0:LANES, :]), (acc_o_ref, tiles(lgo), vt[LANES:2 * LANES, :]))

    @pl.when(bounded)
    def _():
        def weights(groups):
            return [(ref, jnp.concatenate([jnp.exp2(t).astype(_BF16) for t in tiles], axis=1), v)
                    for ref, tiles, v in groups]

        def body(k0, rows, carry, groups):
            for ref, p, v in groups:
                ref[...] += jnp.dot(v, p, preferred_element_type=_F32)
            return carry
        over_keys(body, jnp.int32(0), stages=(masked_logits, weights))

    @pl.when(jnp.logical_not(bounded))
    def _():
        def body(k0, rows, m):
            m_out = []
            for g, (ref, tiles, v) in enumerate(masked_logits((k0, rows))):
                ps, alphas = [], []
                for j, t in enumerate(tiles):
                    sl = slice(LANES * (g * half + j), LANES * (g * half + j + 1))
                    m_new = jnp.maximum(m[:, sl], jnp.max(t, axis=0, keepdims=True))
                    ps.append(jnp.exp2(t - m_new).astype(_BF16))
                    alphas.append(jnp.exp2(m[:, sl] - m_new))
                    m_out.append(m_new)
                ref[...] = ref[...] * jnp.concatenate(alphas, axis=1) + jnp.dot(
                    v, jnp.concatenate(ps, axis=1), preferred_element_type=_F32)
            return jnp.concatenate(m_out, axis=1)
        over_keys(body, jnp.full((1, nh * LANES), NEG_BIG, _F32))

    row = lax.broadcasted_iota(_I32, (LANES, LANES), 0)
    for j in range(half):
        sl = slice(LANES * j, LANES * (j + 1))
        even = acc_e_ref[:, sl] / acc_e_ref[HEAD_DIM:HEAD_DIM + 1, sl]
        odd = acc_o_ref[:, sl] / acc_o_ref[0:1, sl]
        tile = jnp.where(row < HEAD_DIM, even, odd)
        out_ref[:, sl] = (tile.T * g_ref[:, sl]).astype(out_ref.dtype)


def _dsa_attention(iq, bq, iwt, gates, ki2, bvt, batch, seq):
    nb = seq // BLOCK
    n = batch * seq
    nvt = seq // PROJ_ROWS
    n_sel = min(TOPK_MAX, seq // 4)
    blk = lambda w, col=0: pl.BlockSpec((BLOCK, w), lambda b, i: (b * nb + i, col))
    return pl.pallas_call(
        functools.partial(_dsa_kernel, n_sel=n_sel),
        grid=(batch, nb),
        in_specs=[blk(256), blk(512),
                  pl.BlockSpec((16, BLOCK), lambda b, i: (0, b * nb + i)),
                  blk(512, 1),
                  pl.BlockSpec((seq, LANES), lambda b, i: (b, 0)),
                  pl.BlockSpec((seq, LANES), lambda b, i: (b, 1)),
                  pl.BlockSpec((nvt, 256, PROJ_ROWS), lambda b, i: (b, 0, 0))],
        out_specs=blk(512),
        out_shape=jax.ShapeDtypeStruct((n, 512), _BF16),
        scratch_shapes=[pltpu.VMEM((seq, LANES), _F32),
                        pltpu.VMEM((seq, LANES), _BF16),
                        pltpu.VMEM((IDX_HEADS * BLOCK, LANES), _BF16),
                        pltpu.VMEM((DSA_Q_HEADS * BLOCK, LANES), _BF16),
                        pltpu.VMEM((seq, LANES), _BF16),
                        pltpu.VMEM((seq, LANES), _BF16),
                        pltpu.VMEM((8, LANES), _F32),
                        pltpu.VMEM((LANES, 512), _F32),
                        pltpu.VMEM((LANES, 512), _F32)],
        compiler_params=pltpu.CompilerParams(dimension_semantics=("arbitrary", "arbitrary"),
                                             vmem_limit_bytes=VMEM_LIMIT),
        name="dsa_attention",
    )(iq, bq, iwt, gates, ki2, ki2, bvt)


def _out_kernel(a_ref, b_ref, x_ref, w_ref, bo_ref, gain_ref, bias_ref, out_ref, *, alpha):
    half = a_ref.shape[1]
    y = jnp.dot(a_ref[...], w_ref[0:half, :], preferred_element_type=_F32)
    y = y + jnp.dot(b_ref[...], w_ref[half:2 * half, :], preferred_element_type=_F32)
    z = alpha * x_ref[...] + (y + bo_ref[...])
    mu = jnp.mean(z, axis=-1, keepdims=True)
    zc = z - mu
    var = jnp.mean(zc * zc, axis=-1, keepdims=True)
    out_ref[...] = zc * lax.rsqrt(var + LN_EPS) * gain_ref[...] + bias_ref[...]


def _out_projection(a, b, x2, w_out, b_out, gain, bias, alpha):
    n, d = x2.shape
    tm = PROJ_ROWS
    row = lambda w: pl.BlockSpec((tm, w), lambda i: (i, 0))
    full = lambda a_: pl.BlockSpec(a_.shape, lambda i: (0,) * a_.ndim)
    consts = (w_out.astype(_BF16), b_out[None, :], gain[None, :], bias[None, :])
    return pl.pallas_call(
        functools.partial(_out_kernel, alpha=alpha),
        grid=(n // tm,),
        in_specs=[row(a.shape[1]), row(b.shape[1]), row(d)] + [full(c) for c in consts],
        out_specs=row(d),
        out_shape=jax.ShapeDtypeStruct((n, d), x2.dtype),
        compiler_params=pltpu.CompilerParams(dimension_semantics=("arbitrary",),
                                             vmem_limit_bytes=VMEM_LIMIT),
        name="out_projection",
    )(a, b, x2, *consts)


def _layer(h, pos2, w_in, b_in, sinks, w_out, b_out, gain, bias, alpha):
    batch, seq, d = h.shape
    x2 = h.reshape(batch * seq, d)
    aq, ak2, bq, iq, ki2, gates, bvt, avt, iwt = _projection(x2, pos2, w_in, b_in)
    a = _swa_attention(aq, ak2, avt, gates, sinks, batch, seq)
    b = _dsa_attention(iq, bq, iwt, gates, ki2, bvt, batch, seq)
    return _out_projection(a, b, x2, w_out, b_out, gain, bias, alpha).reshape(batch, seq, d)


def kernel(x, positions, w_in, b_in, swa_sinks, w_out, b_out, ln_gain, ln_bias):
    batch, seq, d = x.shape
    depth = w_in.shape[0]
    assert d == 1024 and seq % KEY_CHUNK == 0 and (batch * seq) % PROJ_ROWS == 0
    assert KEY_CHUNK % PROJ_ROWS == 0 and KEY_CHUNK % TIE_CHUNK == 0
    alpha = (2.0 * depth) ** 0.25
    pos2 = positions.reshape(batch * seq, 1)
    h = x
    for layer in range(depth):
        h = _layer(h, pos2, w_in[layer], b_in[layer], swa_sinks[layer], w_out[layer], b_out[layer],
                   ln_gain[layer], ln_bias[layer], alpha)
    return h
```

```python
import functools

import numpy as np
import jax
import jax.numpy as jnp
from jax import lax
from jax.experimental import pallas as pl
from jax.experimental.pallas import tpu as pltpu

HEAD_DIM = 64
SWA_Q_HEADS = 8
SWA_KV_HEADS = 2
DSA_Q_HEADS = 8
IDX_HEADS = 4
IDX_DIM = 64
WINDOW = 128
BLOCK = 128
TOPK_MAX = 256
ROPE_THETA = 10000.0
LN_EPS = 1e-5

LANES = 128
PROJ_ROWS = 512
OUT_ROWS = 1024
KEY_CHUNK = 1024
TIE_CHUNK = 512
VMEM_LIMIT = 56 * 1024 * 1024

LOG2E = 1.4426950408889634
NEG_BIG = -1e30
SHIFT_LIMIT = 60.0
F32_LOWEST = -3.4028234663852886e38
INT_MAX = 2 ** 31 - 1
KEY_NEG_INF = 0x807FFFFF - 2 ** 32
KEY_MIN_NORMAL = 0x00800000
SEARCH_FIRST = 6
SEARCH_GROUP = 2

_F32 = jnp.float32
_BF16 = jnp.bfloat16
_I32 = jnp.int32
_NT = (((1,), (1,)), ((), ()))


def _split_columns(m):
    swa_w = SWA_Q_HEADS * HEAD_DIM
    kv_w = SWA_KV_HEADS * HEAD_DIM
    dsa_w = DSA_Q_HEADS * HEAD_DIM
    sizes = (swa_w, kv_w, kv_w, swa_w, dsa_w, HEAD_DIM, HEAD_DIM, dsa_w,
             IDX_HEADS * IDX_DIM, IDX_DIM, IDX_HEADS)
    o = np.cumsum((0,) + sizes)
    aq, ak, av, ag, bq, bk, bv, bg, iq, ik, iw = [m[:, o[i]:o[i + 1]] for i in range(len(sizes))]

    def dup(c):
        return jnp.concatenate([c[:, j:j + HEAD_DIM] for j in range(0, c.shape[1], HEAD_DIM) for _ in (0, 1)],
                               axis=1)

    def pairs(c):
        r = c.reshape(c.shape[0], -1, 2, 2, HEAD_DIM // 2)
        return jnp.swapaxes(r, 2, 3).reshape(c.shape[0], -1)

    roped = pairs(jnp.concatenate([aq, dup(ak), bq, iq, dup(bk), dup(ik)], axis=1))
    return jnp.concatenate([roped, ag, bg], axis=1), (bv, av[:, :HEAD_DIM], av[:, HEAD_DIM:]), iw


def _proj_kernel(x_ref, pos_ref, inv_ref, sgn_ref, wm_ref, bm_ref,
                 wvt_ref, bvt_ref, wiw_ref, biw_ref,
                 aq_ref, ak2_ref, bq_ref, iq_ref, ki2_ref, g_ref, bvt_out_ref, avt_out_ref, iwt_ref):
    xb = x_ref[...].astype(_BF16)
    ang = pos_ref[...].astype(_F32) * inv_ref[...]
    cos = jnp.cos(ang)
    sin = jnp.sin(ang) * sgn_ref[...]

    q_scale = 0.125 * LOG2E
    roped_dst = ((aq_ref, 0, q_scale), (aq_ref, 256, q_scale), (ak2_ref, 0, 1.0), (bq_ref, 0, q_scale),
                 (bq_ref, 256, q_scale), (iq_ref, 0, 1.0), (ki2_ref, 0, 1.0))
    for g, (dst, off, scale) in enumerate(roped_dst):
        c0 = 256 * g
        hm = jnp.dot(xb, wm_ref[:, c0:c0 + 256], preferred_element_type=_F32) + bm_ref[:, c0:c0 + 256]
        for s in range(2):
            sl = slice(LANES * s, LANES * (s + 1))
            o = hm[:, sl] * cos + pltpu.roll(hm[:, sl], LANES // 2, axis=1) * sin
            if scale != 1.0:
                o = o * scale
            dst[:, off + LANES * s:off + LANES * (s + 1)] = o.astype(dst.dtype)

    n_roped = 256 * len(roped_dst)
    for g in range(4):
        c0 = n_roped + 256 * g
        h = jnp.dot(xb, wm_ref[:, c0:c0 + 256], preferred_element_type=_F32) + bm_ref[:, c0:c0 + 256]
        g_ref[:, 256 * g:256 * (g + 1)] = h * (1.0 / (1.0 + jnp.exp(-h)))

    vt = lax.dot_general(wvt_ref[...], xb, _NT, preferred_element_type=_F32) + bvt_ref[...]
    n_b = bvt_out_ref.shape[1]
    bvt_out_ref[0] = vt[0:n_b].astype(bvt_out_ref.dtype)
    avt_out_ref[0] = vt[n_b:].astype(avt_out_ref.dtype)
    iwt = lax.dot_general(wiw_ref[...], xb, _NT, preferred_element_type=_F32) + biw_ref[...]
    iwt_ref[...] = iwt * (IDX_HEADS ** -0.5 * IDX_DIM ** -0.5)


def _projection(x2, pos2, w_in, b_in):
    n, d = x2.shape
    tm = PROJ_ROWS
    wm, wvs, wiw = _split_columns(w_in)
    bm, bvs, biw = _split_columns(b_in[None, :])
    wm = wm.astype(_BF16)
    zpad = jnp.zeros((HEAD_DIM, d), w_in.dtype)
    one_row = jnp.zeros((HEAD_DIM,), b_in.dtype).at[0].set(1.0)
    wvt = jnp.concatenate([p for wv in wvs for p in (wv.T, zpad, zpad, wv.T)], axis=0).astype(_BF16)
    bvt = jnp.concatenate([p for bv in bvs for p in (bv[0], one_row, one_row, bv[0])])[:, None]
    wiw_t = jnp.concatenate([wiw.T, jnp.zeros((16 - IDX_HEADS, d), w_in.dtype)], axis=0).astype(_BF16)
    biw_t = jnp.concatenate([biw[0], jnp.zeros((16 - IDX_HEADS,), b_in.dtype)])[:, None]
    half = HEAD_DIM // 2
    inv = ROPE_THETA ** (-jnp.arange(0, HEAD_DIM, 2, dtype=_F32) / HEAD_DIM)
    inv128 = jnp.tile(inv, LANES // half)[None, :]
    sgn128 = jnp.concatenate([-jnp.ones((LANES // 2,), _F32), jnp.ones((LANES // 2,), _F32)])[None, :]

    row = lambda w: pl.BlockSpec((tm, w), lambda i: (i, 0))
    full = lambda a: pl.BlockSpec(a.shape, lambda i: (0,) * a.ndim, pipeline_mode=pl.Buffered(1))
    out_shape = (
        jax.ShapeDtypeStruct((n, 512), _BF16),
        jax.ShapeDtypeStruct((n, 256), _BF16),
        jax.ShapeDtypeStruct((n, 512), _BF16),
        jax.ShapeDtypeStruct((n, 256), _BF16),
        jax.ShapeDtypeStruct((n, 256), _BF16),
        jax.ShapeDtypeStruct((n, 1024), _F32),
        jax.ShapeDtypeStruct((n // tm, 256, tm), _BF16),
        jax.ShapeDtypeStruct((n // tm, 512, tm), _BF16),
        jax.ShapeDtypeStruct((16, n), _F32),
    )
    out_specs = (row(512), row(256), row(512), row(256), row(256), row(1024),
                 pl.BlockSpec((1, 256, tm), lambda i: (i, 0, 0)),
                 pl.BlockSpec((1, 512, tm), lambda i: (i, 0, 0)),
                 pl.BlockSpec((16, tm), lambda i: (0, i)))
    consts = (inv128, sgn128, wm, bm, wvt, bvt, wiw_t, biw_t)
    return pl.pallas_call(
        _proj_kernel,
        grid=(n // tm,),
        in_specs=[row(d), row(1)] + [full(a) for a in consts],
        out_specs=out_specs,
        out_shape=out_shape,
        compiler_params=pltpu.CompilerParams(dimension_semantics=("arbitrary",),
                                             vmem_limit_bytes=VMEM_LIMIT),
        name="in_projection",
    )(x2, pos2, *consts)


def _swa_kernel(q_ref, kc_ref, kp_ref, vc_ref, vp_ref, g_ref, sink_ref, out_ref):
    i = pl.program_id(1)
    tq = BLOCK
    n_sub = q_ref.shape[0] // tq
    group = SWA_Q_HEADS // SWA_KV_HEADS
    lane = lax.broadcasted_iota(_I32, (tq, LANES), 1)
    lo_half = (lane & (HEAD_DIM // 2)) == 0
    r = lax.broadcasted_iota(_I32, (2 * tq, LANES), 0)
    c = lax.broadcasted_iota(_I32, (2 * tq, LANES), 1)
    in_window = (c < r) & (r <= c + WINDOW)
    row = lax.broadcasted_iota(_I32, (LANES, LANES), 0)

    kwin = jnp.concatenate([kp_ref[...], kc_ref[...]], axis=0)
    vwin = jnp.concatenate([vp_ref[0], vc_ref[0]], axis=1)
    combos = [(g, j, odd) for g in range(SWA_KV_HEADS) for j in range(n_sub) for odd in (0, 1)]
    rows = lambda j: slice(tq * j, tq * (j + 1))
    win = lambda j: slice(tq * j, tq * (j + 2))
    slab_of = lambda g, s: slice(LANES * (g * (group // 2) + s), LANES * (g * (group // 2) + s + 1))

    logits = {}
    for g, j, odd in combos:
        qs = [jnp.where(lo_half if odd == 0 else ~lo_half, q_ref[rows(j), slab_of(g, s)].astype(_F32),
                        0.0).astype(_BF16) for s in range(group // 2)]
        logits[g, j, odd] = lax.dot_general(kwin[win(j), LANES * g:LANES * (g + 1)], jnp.concatenate(qs, axis=0),
                                            _NT, preferred_element_type=_F32)
    probs = {}
    for g, j, odd in combos:
        valid = in_window if j > 0 else in_window & ((r >= tq) | (i > 0))
        ps, sink_terms = [], []
        for s in range(group // 2):
            lg = jnp.where(valid, logits[g, j, odd][:, LANES * s:LANES * (s + 1)], -jnp.inf)
            sink = sink_ref[group * g + 2 * s + odd:group * g + 2 * s + odd + 1, :]
            m = jnp.maximum(jnp.max(lg, axis=0, keepdims=True), sink)
            ps.append(jnp.exp2(lg - m).astype(_BF16))
            sink_terms.append(jnp.exp2(sink - m))
        probs[g, j, odd] = jnp.concatenate(ps, axis=1), jnp.concatenate(sink_terms, axis=1)
    tiles = {}
    for g, j, odd in combos:
        p, sink_term = probs[g, j, odd]
        v = vwin[LANES * (2 * g + odd):LANES * (2 * g + odd + 1), win(j)]
        acc = jnp.dot(v, p, preferred_element_type=_F32)
        l_row = HEAD_DIM if odd == 0 else 0
        tiles[g, j, odd] = acc / (acc[l_row:l_row + 1, :] + sink_term)
    for g in range(SWA_KV_HEADS):
        for j in range(n_sub):
            for s in range(group // 2):
                sl = slice(LANES * s, LANES * (s + 1))
                tile = jnp.where(row < HEAD_DIM, tiles[g, j, 0][:, sl], tiles[g, j, 1][:, sl])
                out_ref[rows(j), slab_of(g, s)] = (tile.T * g_ref[rows(j), slab_of(g, s)]).astype(out_ref.dtype)


def _swa_attention(aq, ak2, avt, gates, sinks, batch, seq):
    tm = PROJ_ROWS
    ns = seq // tm
    n = batch * seq
    per = tm // BLOCK
    sink_b = jnp.broadcast_to((sinks.astype(_F32) * LOG2E)[:, None], (SWA_Q_HEADS, LANES))
    cur = lambda w: pl.BlockSpec((tm, w), lambda b, i: (b * ns + i, 0))
    return pl.pallas_call(
        _swa_kernel,
        grid=(batch, ns),
        in_specs=[cur(512), cur(256),
                  pl.BlockSpec((BLOCK, 256), lambda b, i: (b * ns * per + jnp.maximum(i * per - 1, 0), 0)),
                  pl.BlockSpec((1, 512, tm), lambda b, i: (b * ns + i, 0, 0)),
                  pl.BlockSpec((1, 512, BLOCK), lambda b, i: (b * ns + jnp.maximum(i - 1, 0), 0, per - 1)),
                  cur(512),
                  pl.BlockSpec((SWA_Q_HEADS, LANES), lambda b, i: (0, 0))],
        out_specs=cur(512),
        out_shape=jax.ShapeDtypeStruct((n, 512), _BF16),
        compiler_params=pltpu.CompilerParams(dimension_semantics=("arbitrary", "arbitrary"),
                                             vmem_limit_bytes=VMEM_LIMIT),
        name="swa_attention",
    )(aq, ak2, ak2, avt, avt, gates, sink_b)


_DSA_HEAD_ORDER = (0, 2, 4, 6, 1, 3, 5, 7)


def _key_to_float(k):
    return lax.bitcast_convert_type(jnp.where(k < 0, k ^ 0x7FFFFFFF, k), _F32)


def _key16_to_float(k):
    return lax.bitcast_convert_type(jnp.where(k <= 0, k ^ 0x7FFF ^ jnp.where(k == 0, -1, 0), k) << 16, _F32)


def _midpoint(a, b):
    return (a >> 1) + (b >> 1) + (a & b & 1)


def _dsa_kernel(iq_ref, bq_ref, iwt_ref, g_ref, bk2_ref, ik2_ref, bvt_ref, out_ref,
                sc_ref, sc16_ref, iqm_ref, bqm_ref, ka_ref, kb_ref, kmax_ref, acc_e_ref, acc_o_ref, *, n_sel):
    tq, kc = BLOCK, KEY_CHUNK
    nh, half = DSA_Q_HEADS, DSA_Q_HEADS // 2
    i = pl.program_id(1)
    q0 = i * tq
    lane = lax.broadcasted_iota(_I32, (tq, LANES), 1)
    lo_half = (lane & (HEAD_DIM // 2)) == 0
    one_even, one_odd = HEAD_DIM // 2, 0

    kh = kc // 2
    rem = (q0 + tq) % kc
    n_whole = (q0 + tq) // kc + jnp.where(rem > kh, 1, 0)
    has_half = (rem > 0) & (rem <= kh)
    k_half = n_whole * kc
    n_rows = k_half + jnp.where(has_half, kh, 0)

    def over_keys(body, carry, stages=()):
        def prepared(pieces):
            states = list(pieces)
            for stage in stages:
                states = [stage(s) for s in states]
            return states

        def run(pieces, cr):
            for (k0, rows), state in zip(pieces, prepared(pieces)):
                cr = body(k0, rows, cr, state) if stages else body(k0, rows, cr)
            return cr

        def two(c, cr):
            k0 = pl.multiple_of(c * 2 * kc, 2 * kc)
            return run([(k0, kc), (k0 + kc, kc)], cr)

        carry = lax.fori_loop(0, n_whole // 2, two, carry)
        carry = lax.cond(n_whole % 2 == 1, lambda cr: run([(pl.multiple_of((n_whole - 1) * kc, kc), kc)], cr),
                         lambda cr: cr, carry)
        return lax.cond(has_half, lambda cr: run([(pl.multiple_of(k_half, kh), kh)], cr), lambda cr: cr, carry)

    @pl.when(i == 0)
    def _():
        klane = lax.broadcasted_iota(_I32, (kc, LANES), 1)
        klo = (klane & (HEAD_DIM // 2)) == 0

        def body(c, mx):
            k0 = pl.multiple_of(c * kc, kc)
            k = bk2_ref[pl.ds(k0, kc), :].astype(_F32)
            ka_ref[pl.ds(k0, kc), :] = jnp.where(klane == one_even, 1.0, k).astype(_BF16)
            kb_ref[pl.ds(k0, kc), :] = jnp.where(klane == one_odd, 1.0, k).astype(_BF16)
            return jnp.maximum(mx, jnp.sum(jnp.where(klo, k * k, 0.0), axis=1, keepdims=True))

        mx = lax.fori_loop(0, sc_ref.shape[0] // kc, body, jnp.zeros((kc, 1), _F32))
        kmax_ref[...] = jnp.broadcast_to(jnp.sqrt(jnp.max(mx, axis=0, keepdims=True)), kmax_ref.shape)

    def masked_half(ref, h):
        slab = ref[:, LANES * (h // 2):LANES * (h // 2 + 1)].astype(_F32)
        return jnp.where(lo_half if h % 2 == 0 else ~lo_half, slab, 0.0)

    for h in range(IDX_HEADS):
        iqm_ref[h * tq:(h + 1) * tq, :] = masked_half(iq_ref, h).astype(_BF16)
    kmax = kmax_ref[0:1, 0:1]
    shift_max = jnp.zeros((tq, 1), _F32)
    for p, h in enumerate(_DSA_HEAD_ORDER):
        qm = masked_half(bq_ref, h)
        shift = jnp.sqrt(jnp.sum(qm * qm, axis=1, keepdims=True)) * kmax
        shift_max = jnp.maximum(shift_max, shift)
        one_lane = one_even if h % 2 == 0 else one_odd
        bqm_ref[p * tq:(p + 1) * tq, :] = jnp.where(lane == one_lane, -shift, qm).astype(_BF16)
    bounded = jnp.max(shift_max) <= SHIFT_LIMIT

    w = [iwt_ref[h:h + 1, :] for h in range(IDX_HEADS)]
    def score_heads(k0, rows):
        return lax.dot_general(ik2_ref[pl.ds(k0, rows), :], iqm_ref[...], _NT, preferred_element_type=_F32)

    def score_piece(k0, rows, diagonal, z=None):
        z = score_heads(k0, rows) if z is None else z
        sc = w[0] * jnp.maximum(z[:, 0:LANES], 0.0)
        for h in range(1, IDX_HEADS):
            sc = sc + w[h] * jnp.maximum(z[:, LANES * h:LANES * (h + 1)], 0.0)
        if diagonal:
            kidx = k0 + lax.broadcasted_iota(_I32, (rows, LANES), 0)
            qidx = q0 + lax.broadcasted_iota(_I32, (rows, LANES), 1)
            sc = jnp.where(kidx <= qidx, sc, -jnp.inf)
        sc_ref[pl.ds(k0, rows), :] = sc
        top = lax.bitcast_convert_type(sc, _I32) & jnp.int32(-65536)
        sc16_ref[pl.ds(k0, rows), :] = lax.bitcast_convert_type(top, _F32).astype(_BF16)

    def score_two(c, carry):
        k0 = pl.multiple_of(c * 2 * kc, 2 * kc)
        za, zb = score_heads(k0, kc), score_heads(k0 + kc, kc)
        score_piece(k0, kc, False, za)
        score_piece(k0 + kc, kc, False, zb)
        return carry

    n_plain = n_whole - jnp.where(has_half, 0, 1)
    lax.fori_loop(0, n_plain // 2, score_two, 0)

    @pl.when(n_plain % 2 == 1)
    def _():
        score_piece(pl.multiple_of((n_plain - 1) * kc, kc), kc, False)

    @pl.when(has_half)
    def _():
        score_piece(pl.multiple_of(k_half, kh), kh, True)

    @pl.when(jnp.logical_not(has_half))
    def _():
        score_piece(pl.multiple_of((n_whole - 1) * kc, kc), kc, True)

    def count_ge(t):
        def body(k0, rows, acc):
            for h in range(0, rows, 512):
                ind = jnp.where(sc_ref[pl.ds(k0 + h, 512), :] >= t, 1, 0)
                acc = acc + jnp.sum(ind.reshape(64, 8, LANES), axis=0)
            return acc
        return jnp.sum(over_keys(body, jnp.zeros((8, LANES), _I32)), axis=0, keepdims=True)

    def count_ge16(t):
        t = t.astype(_BF16)

        def body(k0, rows, acc):
            d = pltpu.bitcast(sc16_ref[pl.ds(k0, rows), :] - t, _I32)
            neg = lax.shift_right_logical(d, 15) & 0x00010001
            return acc + jnp.sum(neg.reshape(rows // 16, 8, LANES), axis=0)
        acc = over_keys(body, jnp.zeros((8, LANES), _I32))
        below = jnp.sum((acc & 0xFFFF) + lax.shift_right_logical(acc, 16), axis=0, keepdims=True)
        return n_rows - below

    def search_pass(count, to_thr, st):
        lo, hi, clo, chi, t, thr = st
        t_up, t_dn = _midpoint(t, hi), _midpoint(lo, t)
        thr_up, thr_dn = to_thr(t_up), to_thr(t_dn)
        cnt = count(thr)
        ok = cnt >= n_sel
        return (jnp.where(ok, t, lo), jnp.where(ok, hi, t), jnp.where(ok, cnt, clo), jnp.where(ok, chi, cnt),
                jnp.where(ok, t_up, t_dn), jnp.where(ok, thr_up, thr_dn))

    vec = lambda v: jnp.full((1, LANES), v, _I32)
    st = (vec(KEY_NEG_INF >> 16), vec(1 << 15), vec(0) + n_rows, vec(0), vec(0), _key16_to_float(vec(0)))
    st = lax.fori_loop(0, 16, lambda j, s: search_pass(count_ge16, _key16_to_float, s), st)
    hint_lo = jnp.maximum(st[0] << 16, KEY_NEG_INF)
    hint_hi = jnp.where(st[1] >= (1 << 15), INT_MAX, st[1] << 16)
    hint_hi = jnp.where((hint_hi > 0) & (hint_hi < KEY_MIN_NORMAL), KEY_MIN_NORMAL, hint_hi)
    c1 = count_ge(_key_to_float(hint_lo))
    ok1 = c1 >= n_sel
    lo, clo = jnp.where(ok1, hint_lo, KEY_NEG_INF), jnp.where(ok1, c1, n_rows)
    hi, chi = jnp.where(ok1, INT_MAX, hint_lo), jnp.where(ok1, 0, c1)
    c2 = count_ge(_key_to_float(hint_hi))
    up2 = (c2 >= n_sel) & (hint_hi > lo)
    dn2 = (c2 < n_sel) & (hint_hi < hi)
    lo, clo = jnp.where(up2, hint_hi, lo), jnp.where(up2, c2, clo)
    hi, chi = jnp.where(dn2, hint_hi, hi), jnp.where(dn2, c2, chi)

    def unsettled(lo, hi, clo):
        return ~((clo == n_sel) | (hi - 1 <= lo) | ((lo >= 0) & (hi <= KEY_MIN_NORMAL)))

    def fine_passes(n, s):
        for _ in range(n):
            s = search_pass(count_ge, _key_to_float, s)
        return s

    any_unsettled = lambda s: jnp.max(jnp.where(unsettled(*s[:3]), 1, 0))

    def fine_group(c):
        s = fine_passes(SEARCH_GROUP, c[1:7])
        return (c[0] + 1,) + s + (any_unsettled(s),)

    t0 = _midpoint(lo, hi)
    s = (lo, hi, clo, chi, t0, _key_to_float(t0))
    s = lax.cond(any_unsettled(s) > 0, lambda s: fine_passes(SEARCH_FIRST, s), lambda s: s, s)
    init = (jnp.int32(0),) + s + (any_unsettled(s),)
    _, lo, hi, clo, chi, _, _, _ = lax.while_loop(lambda c: (c[7] > 0) & (c[0] < 24), fine_group, init)
    t_lo = _key_to_float(lo)
    t_hi = _key_to_float(hi)

    need = (clo > n_sel) & (lo > KEY_NEG_INF)

    @pl.when(jnp.max(jnp.where(need, 1, 0)) > 0)
    def _():
        tk = TIE_CHUNK
        room = (n_sel - chi).astype(_F32)
        before = (lax.broadcasted_iota(_I32, (tk, tk), 0) > lax.broadcasted_iota(_I32, (tk, tk), 1))
        before = jnp.where(before, 1.0, 0.0).astype(_BF16)

        def body(k0, rows, seen):
            for h in range(rows // tk):
                x = sc_ref[pl.ds(k0 + h * tk, tk), :]
                tie = (x >= t_lo) & ~(x >= t_hi)
                tf = jnp.where(tie, 1.0, 0.0)
                rank = seen + jnp.dot(before, tf.astype(_BF16), preferred_element_type=_F32)
                sc_ref[pl.ds(k0 + h * tk, tk), :] = jnp.where(tie & (rank >= room), -jnp.inf, x)
                seen = seen + jnp.sum(tf, axis=0, keepdims=True)
            return seen

        over_keys(body, jnp.zeros((1, LANES), _F32))

    thr = jnp.maximum(t_lo, F32_LOWEST)

    acc_e_ref[...] = jnp.zeros_like(acc_e_ref)
    acc_o_ref[...] = jnp.zeros_like(acc_o_ref)
    vt_rows = bvt_ref.shape[2]

    def masked_logits(piece):
        k0, rows = piece
        sel = sc_ref[pl.ds(k0, rows), :] >= thr
        lge = lax.dot_general(ka_ref[pl.ds(k0, rows), :], bqm_ref[0:half * tq, :], _NT,
                              preferred_element_type=_F32)
        lgo = lax.dot_general(kb_ref[pl.ds(k0, rows), :], bqm_ref[half * tq:nh * tq, :], _NT,
                              preferred_element_type=_F32)
        tiles = lambda lg: [jnp.where(sel, lg[:, LANES * j:LANES * (j + 1)], NEG_BIG) for j in range(half)]
        vt = jnp.concatenate([bvt_ref[k0 // vt_rows + j] for j in range(rows // vt_rows)], axis=1)
        return ((acc_e_ref, tiles(lge), vt[0:LANES, :]), (acc_o_ref, tiles(lgo), vt[LANES:2 * LANES, :]))

    @pl.when(bounded)
    def _():
        def weights(groups):
            return [(ref, jnp.concatenate([jnp.exp2(t).astype(_BF16) for t in tiles], axis=1), v)
                    for ref, tiles, v in groups]

        def body(k0, rows, carry, groups):
            for ref, p, v in groups:
                ref[...] += jnp.dot(v, p, preferred_element_type=_F32)
            return carry
        over_keys(body, jnp.int32(0), stages=(masked_logits, weights))

    @pl.when(jnp.logical_not(bounded))
    def _():
        def body(k0, rows, m):
            m_out = []
            for g, (ref, tiles, v) in enumerate(masked_logits((k0, rows))):
                ps, alphas = [], []
                for j, t in enumerate(tiles):
                    sl = slice(LANES * (g * half + j), LANES * (g * half + j + 1))
                    m_new = jnp.maximum(m[:, sl], jnp.max(t, axis=0, keepdims=True))
                    ps.append(jnp.exp2(t - m_new).astype(_BF16))
                    alphas.append(jnp.exp2(m[:, sl] - m_new))
                    m_out.append(m_new)
                ref[...] = ref[...] * jnp.concatenate(alphas, axis=1) + jnp.dot(
                    v, jnp.concatenate(ps, axis=1), preferred_element_type=_F32)
            return jnp.concatenate(m_out, axis=1)
        over_keys(body, jnp.full((1, nh * LANES), NEG_BIG, _F32))

    row = lax.broadcasted_iota(_I32, (LANES, LANES), 0)
    for j in range(half):
        sl = slice(LANES * j, LANES * (j + 1))
        even = acc_e_ref[:, sl] / acc_e_ref[HEAD_DIM:HEAD_DIM + 1, sl]
        odd = acc_o_ref[:, sl] / acc_o_ref[0:1, sl]
        tile = jnp.where(row < HEAD_DIM, even, odd)
        out_ref[:, sl] = (tile.T * g_ref[:, sl]).astype(out_ref.dtype)


def _dsa_attention(iq, bq, iwt, gates, ki2, bvt, batch, seq):
    nb = seq // BLOCK
    n = batch * seq
    nvt = seq // PROJ_ROWS
    n_sel = min(TOPK_MAX, seq // 4)
    blk = lambda w, col=0: pl.BlockSpec((BLOCK, w), lambda b, i: (b * nb + i, col))
    return pl.pallas_call(
        functools.partial(_dsa_kernel, n_sel=n_sel),
        grid=(batch, nb),
        in_specs=[blk(256), blk(512),
                  pl.BlockSpec((16, BLOCK), lambda b, i: (0, b * nb + i)),
                  blk(512, 1),
                  pl.BlockSpec((seq, LANES), lambda b, i: (b, 0)),
                  pl.BlockSpec((seq, LANES), lambda b, i: (b, 1)),
                  pl.BlockSpec((nvt, 256, PROJ_ROWS), lambda b, i: (b, 0, 0))],
        out_specs=blk(512),
        out_shape=jax.ShapeDtypeStruct((n, 512), _BF16),
        scratch_shapes=[pltpu.VMEM((seq, LANES), _F32),
                        pltpu.VMEM((seq, LANES), _BF16),
                        pltpu.VMEM((IDX_HEADS * BLOCK, LANES), _BF16),
                        pltpu.VMEM((DSA_Q_HEADS * BLOCK, LANES), _BF16),
                        pltpu.VMEM((seq, LANES), _BF16),
                        pltpu.VMEM((seq, LANES), _BF16),
                        pltpu.VMEM((8, LANES), _F32),
                        pltpu.VMEM((LANES, 512), _F32),
                        pltpu.VMEM((LANES, 512), _F32)],
        compiler_params=pltpu.CompilerParams(dimension_semantics=("arbitrary", "arbitrary"),
                                             vmem_limit_bytes=VMEM_LIMIT),
        name="dsa_attention",
    )(iq, bq, iwt, gates, ki2, ki2, bvt)


def _out_kernel(a_ref, b_ref, x_ref, w_ref, bo_ref, gain_ref, bias_ref, out_ref, *, alpha):
    half = a_ref.shape[1]
    y = jnp.dot(a_ref[...], w_ref[0:half, :], preferred_element_type=_F32)
    y = y + jnp.dot(b_ref[...], w_ref[half:2 * half, :], preferred_element_type=_F32)
    z = alpha * x_ref[...] + (y + bo_ref[...])
    mu = jnp.mean(z, axis=-1, keepdims=True)
    zc = z - mu
    var = jnp.mean(zc * zc, axis=-1, keepdims=True)
    out_ref[...] = zc * lax.rsqrt(var + LN_EPS) * gain_ref[...] + bias_ref[...]


def _out_projection(a, b, x2, w_out, b_out, gain, bias, alpha):
    n, d = x2.shape
    tm = OUT_ROWS
    row = lambda w: pl.BlockSpec((tm, w), lambda i: (i, 0))
    full = lambda a_: pl.BlockSpec(a_.shape, lambda i: (0,) * a_.ndim, pipeline_mode=pl.Buffered(1))
    consts = (w_out.astype(_BF16), b_out[None, :], gain[None, :], bias[None, :])
    return pl.pallas_call(
        functools.partial(_out_kernel, alpha=alpha),
        grid=(n // tm,),
        in_specs=[row(a.shape[1]), row(b.shape[1]), row(d)] + [full(c) for c in consts],
        out_specs=row(d),
        out_shape=jax.ShapeDtypeStruct((n, d), x2.dtype),
        compiler_params=pltpu.CompilerParams(dimension_semantics=("arbitrary",),
                                             vmem_limit_bytes=VMEM_LIMIT),
        name="out_projection",
    )(a, b, x2, *consts)


def _layer(h, pos2, w_in, b_in, sinks, w_out, b_out, gain, bias, alpha):
    batch, seq, d = h.shape
    x2 = h.reshape(batch * seq, d)
    aq, ak2, bq, iq, ki2, gates, bvt, avt, iwt = _projection(x2, pos2, w_in, b_in)
    a = _swa_attention(aq, ak2, avt, gates, sinks, batch, seq)
    b = _dsa_attention(iq, bq, iwt, gates, ki2, bvt, batch, seq)
    return _out_projection(a, b, x2, w_out, b_out, gain, bias, alpha).reshape(batch, seq, d)


def kernel(x, positions, w_in, b_in, swa_sinks, w_out, b_out, ln_gain, ln_bias):
    batch, seq, d = x.shape
    depth = w_in.shape[0]
    assert d == 1024 and seq % KEY_CHUNK == 0 and (batch * seq) % OUT_ROWS == 0 and OUT_ROWS % PROJ_ROWS == 0
    assert KEY_CHUNK % PROJ_ROWS == 0 and KEY_CHUNK % TIE_CHUNK == 0
    alpha = (2.0 * depth) ** 0.25
    pos2 = positions.reshape(batch * seq, 1)
    h = x
    for layer in range(depth):
        h = _layer(h, pos2, w_in[layer], b_in[layer], swa_sinks[layer], w_out[layer], b_out[layer],
                   ln_gain[layer], ln_bias[layer], alpha)
    return h
```
